```python
import jax, jax.numpy as jnp
from jax import lax
import numpy as np

D_MODEL = 1024
BATCH = 2
SEQ = 8192
DEPTH = 2

N_EVEN = (DEPTH + 1) // 2
N_ODD = DEPTH // 2
BLOCK = 128
D_FF = 2816
EPS = 1e-6
SB_HEADS = 8
SB_HEAD_DIM = 64
SB_WIDTH = SB_HEADS * SB_HEAD_DIM
SG_GROUPS = 8
SG_GROUP_DIM = 64
SG_WIDTH = SG_GROUPS * SG_GROUP_DIM
SG_CHUNK = 128
EVEN_IN = 3 * SB_WIDTH + 2 * SG_WIDTH
EVEN_MIX = SB_WIDTH + SG_WIDTH
MLA_HEADS = 16
MLA_NOPE = 64
MLA_ROPE = 32
MLA_QK = MLA_NOPE + MLA_ROPE
MLA_V = 64
MLA_Q_LORA = 512
MLA_KV_LORA = 256
MLA_IN = MLA_Q_LORA + MLA_KV_LORA + MLA_ROPE
MLA_MIX = MLA_HEADS * MLA_V
ROPE_THETA = 10000.0
MEM_TOKENS = 256
MEM_HEADS = 4
MEM_HEAD_DIM = D_MODEL // MEM_HEADS

kernel_name = 'hybrid_sb_gmlp_mla_macaron_trunk'


def _rmsnorm(x, g):
    xf = x.astype(jnp.float32)
    y = xf * lax.rsqrt(jnp.mean(xf * xf, axis=-1, keepdims=True) + EPS)
    return (y * g.astype(jnp.float32)).astype(x.dtype)


def _layernorm(x, g, b):
    xf = x.astype(jnp.float32)
    mu = jnp.mean(xf, axis=-1, keepdims=True)
    var = jnp.mean(jnp.square(xf - mu), axis=-1, keepdims=True)
    y = (xf - mu) * lax.rsqrt(var + EPS)
    return (y * g.astype(jnp.float32) + b.astype(jnp.float32)).astype(x.dtype)


def _swiglu(h, w_gu, w_down):
    gate, up = jnp.split(h @ w_gu, 2, axis=-1)
    return (jax.nn.silu(gate) * up) @ w_down


def _to_blocks(t):
    b, s, h, d = t.shape
    return t.reshape(b, s // BLOCK, BLOCK, h, d).transpose(1, 0, 2, 3, 4)


def _from_blocks(t):
    nb, b, l, h, d = t.shape
    return t.transpose(1, 0, 2, 3, 4).reshape(b, nb * l, h, d)


def _rope(t, positions):
    half = t.shape[-1] // 2
    inv_freq = ROPE_THETA ** (-jnp.arange(half, dtype=jnp.float32) / half)
    ang = positions.astype(jnp.float32)[:, :, None, None] * inv_freq
    cos, sin = jnp.cos(ang), jnp.sin(ang)
    tf = t.astype(jnp.float32)
    t1, t2 = tf[..., :half], tf[..., half:]
    return jnp.concatenate([t1 * cos - t2 * sin, t1 * sin + t2 * cos], axis=-1).astype(t.dtype)


def _stick_breaking_attention(q, k, v):
    s_len = q.shape[1]
    scale = SB_HEAD_DIM ** -0.5
    k_pos = jnp.arange(s_len)

    def block(args):
        q_blk, i = args
        z = jnp.einsum('bqhd,bkhd->bhqk', q_blk, k).astype(jnp.float32) * scale
        q_pos = i * BLOCK + jnp.arange(BLOCK)
        strict = (k_pos[None, :] < q_pos[:, None])[None, None]
        log_stay = jnp.where(strict, jax.nn.log_sigmoid(-z), 0.0)
        log_rest = lax.cumsum(log_stay, axis=3, reverse=True) - log_stay
        w = jnp.where(strict, jnp.exp(jax.nn.log_sigmoid(z) + log_rest), 0.0)
        return jnp.einsum('bhqk,bkhd->bqhd', w.astype(v.dtype), v)

    out = lax.map(block, (_to_blocks(q), jnp.arange(s_len // BLOCK)))
    return _from_blocks(out)


def _causal_softmax_attention(q, k, v):
    s_len = q.shape[1]
    scale = q.shape[-1] ** -0.5
    k_pos = jnp.arange(s_len)

    def block(args):
        q_blk, i = args
        sc = jnp.einsum('bqhd,bkhd->bhqk', q_blk, k).astype(jnp.float32) * scale
        q_pos = i * BLOCK + jnp.arange(BLOCK)
        causal = (k_pos[None, :] <= q_pos[:, None])[None, None]
        p = jax.nn.softmax(jnp.where(causal, sc, -jnp.inf), axis=-1)
        return jnp.einsum('bhqk,bkhd->bqhd', p.astype(v.dtype), v)

    out = lax.map(block, (_to_blocks(q), jnp.arange(s_len // BLOCK)))
    return _from_blocks(out)


def _even_mixer(h, w_in, ln_g, ln_b, sgu_w, sgu_b, w_out):
    b, s, _ = h.shape
    q, k, v, z = jnp.split(h @ w_in, [SB_WIDTH, 2 * SB_WIDTH, 3 * SB_WIDTH], axis=-1)
    heads = lambda t: t.reshape(b, s, SB_HEADS, SB_HEAD_DIM)
    o_sb = _stick_breaking_attention(heads(q), heads(k), heads(v)).reshape(b, s, SB_WIDTH)
    u, g = jnp.split(jax.nn.gelu(z), 2, axis=-1)
    g = _layernorm(g, ln_g, ln_b).reshape(b, s // SG_CHUNK, SG_CHUNK, SG_GROUPS, SG_GROUP_DIM)
    tri = jnp.tril(jnp.ones((SG_CHUNK, SG_CHUNK), dtype=sgu_w.dtype))
    mixed = jnp.einsum('gts,bcsgd->bctgd', sgu_w * tri, g) + sgu_b.T[None, None, :, :, None]
    o_sg = u * mixed.reshape(b, s, SG_WIDTH)
    return jnp.concatenate([o_sb, o_sg], axis=-1) @ w_out


def _mla_mixer(h, positions, w_in, q_lora_g, kv_lora_g, w_uq, w_ukv, q_g, k_g, w_out):
    b, s, _ = h.shape
    c_q, c_kv, k_r = jnp.split(h @ w_in, [MLA_Q_LORA, MLA_Q_LORA + MLA_KV_LORA], axis=-1)
    q = (_rmsnorm(c_q, q_lora_g) @ w_uq).reshape(b, s, MLA_HEADS, MLA_QK)
    kv = (_rmsnorm(c_kv, kv_lora_g) @ w_ukv).reshape(b, s, MLA_HEADS, MLA_NOPE + MLA_V)
    k_nope, v = kv[..., :MLA_NOPE], kv[..., MLA_NOPE:]
    k_r = jnp.broadcast_to(k_r[:, :, None, :], (b, s, MLA_HEADS, MLA_ROPE))
    k = jnp.concatenate([k_nope, k_r], axis=-1)
    q = _rmsnorm(q, q_g)
    k = _rmsnorm(k, k_g)
    q = jnp.concatenate([q[..., :MLA_NOPE], _rope(q[..., MLA_NOPE:], positions)], axis=-1)
    k = jnp.concatenate([k[..., :MLA_NOPE], _rope(k[..., MLA_NOPE:], positions)], axis=-1)
    o = _causal_softmax_attention(q, k, v).reshape(b, s, MLA_MIX)
    return o @ w_out


def _memory_cross_attention(hq, hm, wq, wkv, q_g, k_g, wo):
    b, s, _ = hq.shape
    m = hm.shape[1]
    q = _rmsnorm((hq @ wq).reshape(b, s, MEM_HEADS, MEM_HEAD_DIM), q_g)
    k, v = jnp.split((hm @ wkv).reshape(b, m, MEM_HEADS, 2 * MEM_HEAD_DIM), 2, axis=-1)
    k = _rmsnorm(k, k_g)
    sc = jnp.einsum('bqhd,bmhd->bhqm', q, k).astype(jnp.float32) * (MEM_HEAD_DIM ** -0.5)
    p = jax.nn.softmax(sc, axis=-1)
    o = jnp.einsum('bhqm,bmhd->bqhd', p.astype(v.dtype), v).reshape(b, s, D_MODEL)
    return o @ wo


def _w(k, shape, fan_in):
    return jax.random.normal(k, shape, jnp.float32) * (fan_in ** -0.5)


def _gain(k, shape):
    return 1.0 + 0.02 * jax.random.normal(k, shape, jnp.float32)


def setup_inputs(seed: int = 0) -> dict:
    key = jax.random.key(seed)
    ks = list(jax.random.split(key, 32))
    nk = ks.pop
    inp = {}
    inp['x'] = jax.random.normal(nk(), (BATCH, SEQ, D_MODEL), jnp.float32)
    inp['mem'] = jax.random.normal(nk(), (BATCH, MEM_TOKENS, D_MODEL), jnp.float32)
    inp['positions'] = jnp.broadcast_to(jnp.arange(SEQ, dtype=jnp.int32)[None, :], (BATCH, SEQ))
    inp['ffn_pre_norm'] = _gain(nk(), (DEPTH, D_MODEL))
    inp['ffn_pre_w_gu'] = _w(nk(), (DEPTH, D_MODEL, 2 * D_FF), D_MODEL)
    inp['ffn_pre_w_down'] = _w(nk(), (DEPTH, D_FF, D_MODEL), D_FF)
    inp['mix_norm'] = _gain(nk(), (DEPTH, D_MODEL))
    inp['sbg_w_in'] = _w(nk(), (N_EVEN, D_MODEL, EVEN_IN), D_MODEL)
    inp['sgu_ln_gain'] = _gain(nk(), (N_EVEN, SG_WIDTH))
    inp['sgu_ln_bias'] = 0.02 * jax.random.normal(nk(), (N_EVEN, SG_WIDTH), jnp.float32)
    inp['sgu_w'] = _w(nk(), (N_EVEN, SG_GROUPS, SG_CHUNK, SG_CHUNK), SG_CHUNK)
    inp['sgu_b'] = 1.0 + 0.1 * jax.random.normal(nk(), (N_EVEN, SG_GROUPS, SG_CHUNK), jnp.float32)
    inp['sbg_w_out'] = _w(nk(), (N_EVEN, EVEN_MIX, D_MODEL), EVEN_MIX)
    inp['mla_w_in'] = _w(nk(), (N_ODD, D_MODEL, MLA_IN), D_MODEL)
    inp['mla_q_lora_gain'] = _gain(nk(), (N_ODD, MLA_Q_LORA))
    inp['mla_kv_lora_gain'] = _gain(nk(), (N_ODD, MLA_KV_LORA))
    inp['mla_w_uq'] = _w(nk(), (N_ODD, MLA_Q_LORA, MLA_HEADS * MLA_QK), MLA_Q_LORA)
    inp['mla_w_ukv'] = _w(nk(), (N_ODD, MLA_KV_LORA, MLA_HEADS * (MLA_NOPE + MLA_V)), MLA_KV_LORA)
    inp['mla_q_gain'] = _gain(nk(), (N_ODD, MLA_QK))
    inp['mla_k_gain'] = _gain(nk(), (N_ODD, MLA_QK))
    inp['mla_w_out'] = _w(nk(), (N_ODD, MLA_MIX, D_MODEL), MLA_MIX)
    inp['xmem_norm'] = _gain(nk(), (DEPTH, D_MODEL))
    inp['xmem_mem_norm'] = _gain(nk(), (DEPTH, D_MODEL))
    inp['xmem_wq'] = _w(nk(), (DEPTH, D_MODEL, D_MODEL), D_MODEL)
    inp['xmem_wkv'] = _w(nk(), (DEPTH, D_MODEL, 2 * D_MODEL), D_MODEL)
    inp['xmem_q_gain'] = _gain(nk(), (DEPTH, MEM_HEAD_DIM))
    inp['xmem_k_gain'] = _gain(nk(), (DEPTH, MEM_HEAD_DIM))
    inp['xmem_wo'] = _w(nk(), (DEPTH, D_MODEL, D_MODEL), D_MODEL)
    inp['ffn_post_norm'] = _gain(nk(), (DEPTH, D_MODEL))
    inp['ffn_post_w_gu'] = _w(nk(), (DEPTH, D_MODEL, 2 * D_FF), D_MODEL)
    inp['ffn_post_w_down'] = _w(nk(), (DEPTH, D_FF, D_MODEL), D_FF)
    return inp


def reference(x, mem, positions,
              ffn_pre_norm, ffn_pre_w_gu, ffn_pre_w_down,
              mix_norm,
              sbg_w_in, sgu_ln_gain, sgu_ln_bias, sgu_w, sgu_b, sbg_w_out,
              mla_w_in, mla_q_lora_gain, mla_kv_lora_gain, mla_w_uq, mla_w_ukv,
              mla_q_gain, mla_k_gain, mla_w_out,
              xmem_norm, xmem_mem_norm, xmem_wq, xmem_wkv, xmem_q_gain, xmem_k_gain, xmem_wo,
              ffn_post_norm, ffn_post_w_gu, ffn_post_w_down):
    for layer in range(DEPTH):
        x = x + 0.5 * _swiglu(_rmsnorm(x, ffn_pre_norm[layer]),
                              ffn_pre_w_gu[layer], ffn_pre_w_down[layer])
        h = _rmsnorm(x, mix_norm[layer])
        if layer % 2 == 0:
            e = layer // 2
            x = x + _even_mixer(h, sbg_w_in[e], sgu_ln_gain[e], sgu_ln_bias[e],
                                sgu_w[e], sgu_b[e], sbg_w_out[e])
        else:
            o = layer // 2
            x = x + _mla_mixer(h, positions, mla_w_in[o], mla_q_lora_gain[o],
                               mla_kv_lora_gain[o], mla_w_uq[o], mla_w_ukv[o],
                               mla_q_gain[o], mla_k_gain[o], mla_w_out[o])
        x = x + _memory_cross_attention(_rmsnorm(x, xmem_norm[layer]),
                                        _rmsnorm(mem, xmem_mem_norm[layer]),
                                        xmem_wq[layer], xmem_wkv[layer],
                                        xmem_q_gain[layer], xmem_k_gain[layer], xmem_wo[layer])
        x = x + 0.5 * _swiglu(_rmsnorm(x, ffn_post_norm[layer]),
                              ffn_post_w_gu[layer], ffn_post_w_down[layer])
    return x
```

```python
import functools

import jax
import jax.numpy as jnp
from jax import lax
from jax.experimental import pallas as pl
from jax.experimental.pallas import tpu as pltpu

EPS = 1e-6
ROPE_THETA = 10000.0
LANES = 128
VMEM_LIMIT_BYTES = 56 * 1024 * 1024

SB_HEADS, SB_HEAD_DIM = 8, 64
SB_WIDTH = SB_HEADS * SB_HEAD_DIM
SG_GROUPS, SG_GROUP_DIM, SG_CHUNK = 8, 64, 128
SG_WIDTH = SG_GROUPS * SG_GROUP_DIM
MLA_HEADS, MLA_NOPE, MLA_ROPE, MLA_V = 16, 64, 32, 64
MLA_QK = MLA_NOPE + MLA_ROPE
MLA_Q_LORA, MLA_KV_LORA = 512, 256
MEM_HEADS = 4

BF16 = jnp.bfloat16
F32 = jnp.float32


def _params(*semantics):
    return pltpu.CompilerParams(dimension_semantics=semantics,
                                vmem_limit_bytes=VMEM_LIMIT_BYTES)


def _dot(a, b):
    return jnp.dot(a, b, preferred_element_type=F32)


def _dot_nt(a, b):
    return lax.dot_general(a, b, (((1,), (1,)), ((), ())), preferred_element_type=F32)


def _rms(x, gain, n=None):
    n = x.shape[-1] if n is None else n
    ms = jnp.sum(x * x, axis=-1, keepdims=True) * (1.0 / n)
    return x * lax.rsqrt(ms + EPS) * gain


def _ffn_kernel(x_ref, g_ref, wg_ref, wu_ref, wd_ref, o_ref, h_ref, acc_ref):
    f = pl.program_id(1)

    @pl.when(f == 0)
    def _():
        h_ref[...] = _rms(x_ref[...], g_ref[...]).astype(BF16)
        acc_ref[...] = jnp.zeros_like(acc_ref)

    h = h_ref[...]
    gate = _dot(h, wg_ref[...])
    up = _dot(h, wu_ref[...])
    act = (gate * jax.nn.sigmoid(gate) * up).astype(BF16)
    acc_ref[...] += _dot(act, wd_ref[...])

    @pl.when(f == pl.num_programs(1) - 1)
    def _():
        o_ref[...] = x_ref[...] + 0.5 * acc_ref[...]


def _ffn(x2, gain, w_gu, w_down, *, tm, tf):
    t, d = x2.shape
    d_ff = w_down.shape[0]
    nf = d_ff // tf
    return pl.pallas_call(
        _ffn_kernel,
        grid=(t // tm, nf),
        in_specs=[
            pl.BlockSpec((tm, d), lambda i, f: (i, 0)),
            pl.BlockSpec((1, d), lambda i, f: (0, 0)),
            pl.BlockSpec((d, tf), lambda i, f: (0, f)),
            pl.BlockSpec((d, tf), lambda i, f: (0, f + nf)),
            pl.BlockSpec((tf, d), lambda i, f: (f, 0)),
        ],
        out_specs=pl.BlockSpec((tm, d), lambda i, f: (i, 0)),
        out_shape=jax.ShapeDtypeStruct((t, d), F32),
        scratch_shapes=[pltpu.VMEM((tm, d), BF16), pltpu.VMEM((tm, d), F32)],
        compiler_params=_params("parallel", "arbitrary"),
        name="ffn",
    )(x2, gain.reshape(1, d), w_gu, w_gu, w_down)


def _gelu_tanh(x):
    c = 0.7978845608028654
    return 0.5 * x * (1.0 + jnp.tanh(c * (x + 0.044715 * (x * x * x))))


def _even_prep_kernel(x_ref, g_ref, win_ref, lng_ref, lnb_ref, sw_ref, sb_ref,
                      q_ref, k_ref, v_ref, osg_ref):
    tm = x_ref.shape[0]
    w = SB_WIDTH
    h = _rms(x_ref[...], g_ref[...]).astype(BF16)
    q_ref[...] = (_dot(h, win_ref[:, 0:w]) * (SB_HEAD_DIM ** -0.5)).astype(BF16)
    k_ref[...] = _dot(h, win_ref[:, w:2 * w]).astype(BF16)
    v_ref[...] = _dot(h, win_ref[:, 2 * w:3 * w]).astype(BF16)
    u = _gelu_tanh(_dot(h, win_ref[:, 3 * w:3 * w + SG_WIDTH]))
    g = _gelu_tanh(_dot(h, win_ref[:, 3 * w + SG_WIDTH:3 * w + 2 * SG_WIDTH]))
    mu = jnp.mean(g, axis=-1, keepdims=True)
    gc = g - mu
    var = jnp.mean(gc * gc, axis=-1, keepdims=True)
    gn = (gc * lax.rsqrt(var + EPS) * lng_ref[...] + lnb_ref[...]).astype(BF16)

    row = lax.broadcasted_iota(jnp.int32, (SG_CHUNK, SG_CHUNK), 0)
    col = lax.broadcasted_iota(jnp.int32, (SG_CHUNK, SG_CHUNK), 1)
    tri = col <= row
    first_group = lax.broadcasted_iota(jnp.int32, (SG_CHUNK, LANES), 1) < SG_GROUP_DIM
    for p in range(SG_GROUPS // 2):
        lanes = slice(p * LANES, (p + 1) * LANES)
        w0 = jnp.where(tri, sw_ref[2 * p], 0.0).astype(BF16)
        w1 = jnp.where(tri, sw_ref[2 * p + 1], 0.0).astype(BF16)
        bias = sb_ref[:, lanes]
        for c in range(tm // SG_CHUNK):
            rows = slice(c * SG_CHUNK, (c + 1) * SG_CHUNK)
            gp = gn[rows, lanes]
            mixed = jnp.where(first_group, _dot(w0, gp), _dot(w1, gp)) + bias
            osg_ref[rows, lanes] = (u[rows, lanes] * mixed).astype(BF16)


def _even_prep(x2, gain, w_in, ln_g, ln_b, sgu_w, sgu_bias_full, *, tm):
    t, d = x2.shape
    n_in = w_in.shape[1]
    out = jax.ShapeDtypeStruct((t, SB_WIDTH), BF16)
    row_spec = pl.BlockSpec((tm, SB_WIDTH), lambda i: (i, 0))
    return pl.pallas_call(
        _even_prep_kernel,
        grid=(t // tm,),
        in_specs=[
            pl.BlockSpec((tm, d), lambda i: (i, 0)),
            pl.BlockSpec((1, d), lambda i: (0, 0)),
            pl.BlockSpec((d, n_in), lambda i: (0, 0)),
            pl.BlockSpec((1, SG_WIDTH), lambda i: (0, 0)),
            pl.BlockSpec((1, SG_WIDTH), lambda i: (0, 0)),
            pl.BlockSpec((SG_GROUPS, SG_CHUNK, SG_CHUNK), lambda i: (0, 0, 0)),
            pl.BlockSpec((SG_CHUNK, SG_WIDTH), lambda i: (0, 0)),
        ],
        out_specs=[row_spec, row_spec, row_spec, row_spec],
        out_shape=[out, out, out, out],
        compiler_params=_params("parallel"),
        name="even_prep",
    )(x2, gain.reshape(1, d), w_in, ln_g.reshape(1, -1), ln_b.reshape(1, -1),
      sgu_w, sgu_bias_full)


def _sb_attn_kernel(q_ref, k_ref, v_ref, o_ref, acc_ref, r_ref, *, blk):
    i = pl.program_id(2)
    lane = lax.broadcasted_iota(jnp.int32, (1, LANES), 1)
    first_head = lane < SB_HEAD_DIM
    q = q_ref[0]
    zero = jnp.zeros_like(q)
    q_heads = (jnp.where(first_head, q, zero), jnp.where(first_head, zero, q))
    row = lax.broadcasted_iota(jnp.int32, (blk, blk), 0)
    col = lax.broadcasted_iota(jnp.int32, (blk, blk), 1)
    suffix = (row >= col).astype(BF16)
    strict = col < row

    def visit(j, diagonal):
        start = pl.multiple_of(j * blk, blk)
        kb = k_ref[0, pl.ds(start, blk), :]
        vb = v_ref[0, pl.ds(start, blk), :]
        for hd in range(2):
            z = _dot_nt(q_heads[hd], kb)
            log_stay = -(jnp.maximum(z, 0.0) + jnp.log(1.0 + jnp.exp(-jnp.abs(z))))
            if diagonal:
                log_stay = jnp.where(strict, log_stay, 0.0)
            tail = _dot(log_stay.astype(BF16), suffix)
            wgt = jnp.exp(z + tail + r_ref[hd])
            if diagonal:
                wgt = jnp.where(strict, wgt, 0.0)
            pv = _dot(wgt.astype(BF16), vb)
            block_sum = jnp.sum(log_stay, axis=-1, keepdims=True)
            if diagonal:
                acc_ref[hd] = pv
                r_ref[hd] = block_sum
            else:
                acc_ref[hd] += pv
                r_ref[hd] += block_sum

    r_ref[...] = jnp.zeros_like(r_ref)
    visit(i, True)

    def body(it, carry):
        visit(i - 1 - it, False)
        return carry

    lax.fori_loop(0, i, body, 0)
    o_ref[0] = jnp.where(first_head, acc_ref[0], acc_ref[1]).astype(BF16)


def _sb_attention(q, k, v, *, blk):
    b, s, w = q.shape
    pairs = w // LANES
    return pl.pallas_call(
        functools.partial(_sb_attn_kernel, blk=blk),
        grid=(b, pairs, s // blk),
        in_specs=[
            pl.BlockSpec((1, blk, LANES), lambda bi, p, i: (bi, i, p)),
            pl.BlockSpec((1, s, LANES), lambda bi, p, i: (bi, 0, p)),
            pl.BlockSpec((1, s, LANES), lambda bi, p, i: (bi, 0, p)),
        ],
        out_specs=pl.BlockSpec((1, blk, LANES), lambda bi, p, i: (bi, i, p)),
        out_shape=jax.ShapeDtypeStruct((b, s, w), BF16),
        scratch_shapes=[pltpu.VMEM((2, blk, LANES), F32), pltpu.VMEM((2, blk, 1), F32)],
        compiler_params=_params("parallel", "parallel", "arbitrary"),
        name="sb_attn",
    )(q, k, v)


def _mla_prep_kernel(x_ref, pos_ref, g_ref, win_ref, qlg_ref, kvlg_ref, wuq_ref, wuk_ref,
                     wuv_ref, qg_ref, kg_ref, freq_ref, sin_lo_ref, sin_hi_ref,
                     q_ref, k_ref, v_ref):
    h = _rms(x_ref[0], g_ref[...]).astype(BF16)
    c_q = _dot(h, win_ref[:, 0:MLA_Q_LORA])
    c_kv = _dot(h, win_ref[:, MLA_Q_LORA:MLA_Q_LORA + MLA_KV_LORA])
    k_rope = _dot(h, win_ref[:, MLA_Q_LORA + MLA_KV_LORA:])
    cqn = _rms(c_q, qlg_ref[...]).astype(BF16)
    ckvn = _rms(c_kv, kvlg_ref[...]).astype(BF16)
    v_ref[0] = _dot(ckvn, wuv_ref[...]).astype(BF16)

    angle = pos_ref[0].astype(F32) * freq_ref[...]
    cos = jnp.cos(angle)
    sin = jnp.sin(angle)
    sin_lo = sin * sin_lo_ref[...]
    sin_hi = sin * sin_hi_ref[...]
    half = MLA_ROPE // 2

    def norm_rope(t, gain):
        tn = _rms(t, gain, MLA_QK)
        return (tn * cos + pltpu.roll(tn, LANES - half, 1) * sin_lo
                + pltpu.roll(tn, half, 1) * sin_hi)

    scale = MLA_QK ** -0.5
    for pair in range(MLA_HEADS // 2):
        cols = slice(2 * pair * LANES, (2 * pair + 2) * LANES)
        q2 = _dot(cqn, wuq_ref[:, cols])
        k2 = _dot(ckvn, wuk_ref[:, cols])
        for sub in range(2):
            hd = 2 * pair + sub
            lanes = slice(sub * LANES, (sub + 1) * LANES)
            q_ref[0, hd] = (norm_rope(q2[:, lanes], qg_ref[...]) * scale).astype(BF16)
            k_ref[0, hd] = norm_rope(k2[:, lanes] + k_rope, kg_ref[...]).astype(BF16)


def _mla_prep(x, positions, gain, w_in, qlg, kvlg, wuq, wuk, wuv, qg, kg, tables, *, tm):
    b, s, d = x.shape
    const2 = lambda bi, i: (0, 0)
    full = lambda a: pl.BlockSpec(a.shape, const2)
    vec = lambda a: a.reshape(1, -1)
    small = [vec(gain), w_in, vec(qlg), vec(kvlg), wuq, wuk, wuv, vec(qg), vec(kg), *tables]
    head_out = jax.ShapeDtypeStruct((b, MLA_HEADS, s, LANES), BF16)
    head_spec = pl.BlockSpec((1, MLA_HEADS, tm, LANES), lambda bi, i: (bi, 0, i, 0))
    return pl.pallas_call(
        _mla_prep_kernel,
        grid=(b, s // tm),
        in_specs=[
            pl.BlockSpec((1, tm, d), lambda bi, i: (bi, i, 0)),
            pl.BlockSpec((1, tm, 1), lambda bi, i: (bi, i, 0)),
            *[full(a) for a in small],
        ],
        out_specs=[head_spec, head_spec,
                   pl.BlockSpec((1, tm, MLA_HEADS * MLA_V), lambda bi, i: (bi, i, 0))],
        out_shape=[head_out, head_out,
                   jax.ShapeDtypeStruct((b, s, MLA_HEADS * MLA_V), BF16)],
        compiler_params=_params("parallel", "parallel"),
        name="mla_prep",
    )(x, positions.reshape(b, s, 1), *small)


def _mla_attn_kernel(q_ref, k_ref, v_ref, o_ref, acc_ref, m_ref, l_ref, *, blk):
    i = pl.program_id(2)
    row = lax.broadcasted_iota(jnp.int32, (blk, blk), 0)
    col = lax.broadcasted_iota(jnp.int32, (blk, blk), 1)
    causal = col <= row

    def visit(j, diagonal):
        start = pl.multiple_of(j * blk, blk)
        vb = v_ref[0, pl.ds(start, blk), :]
        for hd in range(2):
            sc = _dot_nt(q_ref[0, hd], k_ref[0, hd, pl.ds(start, blk), :])
            if diagonal:
                sc = jnp.where(causal, sc, -1e30)
                m_new = jnp.max(sc, axis=-1, keepdims=True)
                p = jnp.exp(sc - m_new)
                l_ref[hd] = jnp.sum(p, axis=-1, keepdims=True)
                acc_ref[hd] = _dot(p.astype(BF16), vb)
            else:
                m_old = m_ref[hd]
                m_new = jnp.maximum(m_old, jnp.max(sc, axis=-1, keepdims=True))
                alpha = jnp.exp(m_old - m_new)
                p = jnp.exp(sc - m_new)
                l_ref[hd] = alpha * l_ref[hd] + jnp.sum(p, axis=-1, keepdims=True)
                acc_ref[hd] = alpha * acc_ref[hd] + _dot(p.astype(BF16), vb)
            m_ref[hd] = m_new

    visit(i, True)

    def body(it, carry):
        visit(i - 1 - it, False)
        return carry

    lax.fori_loop(0, i, body, 0)
    first_head = lax.broadcasted_iota(jnp.int32, (1, LANES), 1) < MLA_V
    out = jnp.where(first_head, acc_ref[0] / l_ref[0], acc_ref[1] / l_ref[1])
    o_ref[0] = out.astype(BF16)


def _mla_attention(q, k, v, *, blk):
    b, heads, s, _ = q.shape
    return pl.pallas_call(
        functools.partial(_mla_attn_kernel, blk=blk),
        grid=(b, heads // 2, s // blk),
        in_specs=[
            pl.BlockSpec((1, 2, blk, LANES), lambda bi, p, i: (bi, p, i, 0)),
            pl.BlockSpec((1, 2, s, LANES), lambda bi, p, i: (bi, p, 0, 0)),
            pl.BlockSpec((1, s, LANES), lambda bi, p, i: (bi, 0, p)),
        ],
        out_specs=pl.BlockSpec((1, blk, LANES), lambda bi, p, i: (bi, i, p)),
        out_shape=jax.ShapeDtypeStruct((b, s, heads * MLA_V), BF16),
        scratch_shapes=[pltpu.VMEM((2, blk, LANES), F32), pltpu.VMEM((2, blk, 1), F32),
                        pltpu.VMEM((2, blk, 1), F32)],
        compiler_params=_params("parallel", "parallel", "arbitrary"),
        name="mla_attn",
    )(q, k, v)


def _mem_kv_kernel(mem_ref, g_ref, wkv_ref, kg_ref, k_ref, v_ref):
    hm = _rms(mem_ref[0], g_ref[...]).astype(BF16)
    hd_dim = kg_ref.shape[1]
    for hd in range(MEM_HEADS):
        kcols = slice(2 * hd * hd_dim, (2 * hd + 1) * hd_dim)
        vcols = slice((2 * hd + 1) * hd_dim, (2 * hd + 2) * hd_dim)
        out = slice(hd * hd_dim, (hd + 1) * hd_dim)
        k_ref[0, :, out] = _rms(_dot(hm, wkv_ref[:, kcols]), kg_ref[...]).astype(BF16)
        v_ref[0, :, out] = _dot(hm, wkv_ref[:, vcols]).astype(BF16)


def _mem_kv(mem, gain, wkv, k_gain):
    b, m, d = mem.shape
    out = jax.ShapeDtypeStruct((b, m, d), BF16)
    spec = pl.BlockSpec((1, m, d), lambda bi: (bi, 0, 0))
    return pl.pallas_call(
        _mem_kv_kernel,
        grid=(b,),
        in_specs=[spec,
                  pl.BlockSpec((1, d), lambda bi: (0, 0)),
                  pl.BlockSpec(wkv.shape, lambda bi: (0, 0)),
                  pl.BlockSpec((1, k_gain.shape[0]), lambda bi: (0, 0))],
        out_specs=[spec, spec],
        out_shape=[out, out],
        compiler_params=_params("parallel"),
        name="mem_kv",
    )(mem, gain.reshape(1, d), wkv, k_gain.reshape(1, -1))


def _mix_xattn_kernel(*refs, n_act):
    x_ref = refs[0]
    act_refs = refs[1:1 + n_act]
    w_ref, g_ref, wq_ref, qg_ref, k_ref, v_ref, wo_ref, o_ref = refs[1 + n_act:]
    act = jnp.concatenate([a_ref[0] for a_ref in act_refs], axis=-1)
    x1 = x_ref[0] + _dot(act, w_ref[...])
    h = _rms(x1, g_ref[...]).astype(BF16)
    hd_dim = qg_ref.shape[1]
    out = x1
    for hd in range(MEM_HEADS):
        cols = slice(hd * hd_dim, (hd + 1) * hd_dim)
        qn = (_rms(_dot(h, wq_ref[:, cols]), qg_ref[...]) * (hd_dim ** -0.5)).astype(BF16)
        sc = _dot_nt(qn, k_ref[0, :, cols])
        p = jnp.exp(sc - jnp.max(sc, axis=-1, keepdims=True))
        denom = jnp.sum(p, axis=-1, keepdims=True)
        o_h = (_dot(p.astype(BF16), v_ref[0, :, cols]) / denom).astype(BF16)
        out = out + _dot(o_h, wo_ref[cols, :])
    o_ref[0] = out


def _mix_xattn(x, acts, w_mix, gain, wq, q_gain, mem_k, mem_v, wo, *, tm):
    b, s, d = x.shape
    m = mem_k.shape[1]
    const2 = lambda bi, i: (0, 0)
    row = lambda width: pl.BlockSpec((1, tm, width), lambda bi, i: (bi, i, 0))
    mem_spec = pl.BlockSpec((1, m, d), lambda bi, i: (bi, 0, 0))
    return pl.pallas_call(
        functools.partial(_mix_xattn_kernel, n_act=len(acts)),
        grid=(b, s // tm),
        in_specs=[
            row(d),
            *[row(a.shape[-1]) for a in acts],
            pl.BlockSpec(w_mix.shape, const2),
            pl.BlockSpec((1, d), const2),
            pl.BlockSpec(wq.shape, const2),
            pl.BlockSpec((1, q_gain.shape[0]), const2),
            mem_spec, mem_spec,
            pl.BlockSpec(wo.shape, const2),
        ],
        out_specs=row(d),
        out_shape=jax.ShapeDtypeStruct((b, s, d), F32),
        compiler_params=_params("parallel", "parallel"),
        name="mix_xattn",
    )(x, *acts, w_mix, gain.reshape(1, d), wq, q_gain.reshape(1, -1), mem_k, mem_v, wo)


def _pad_last(a, width):
    return jnp.pad(a, [(0, 0)] * (a.ndim - 1) + [(0, width - a.shape[-1])])


def _mla_weights(w_in, w_uq, w_ukv, q_gain, k_gain):
    lat = MLA_Q_LORA + MLA_KV_LORA
    k_rope_cols = jnp.pad(w_in[:, lat:], ((0, 0), (MLA_NOPE, LANES - MLA_QK)))
    w_in_ext = jnp.concatenate([w_in[:, :lat], k_rope_cols], axis=1)
    wuq = _pad_last(w_uq.reshape(MLA_Q_LORA, MLA_HEADS, MLA_QK), LANES)
    wukv = w_ukv.reshape(MLA_KV_LORA, MLA_HEADS, MLA_NOPE + MLA_V)
    wuk = _pad_last(wukv[..., :MLA_NOPE], LANES)
    wuv = wukv[..., MLA_NOPE:]
    flat = lambda a: a.reshape(a.shape[0], -1).astype(BF16)
    return (w_in_ext.astype(BF16), flat(wuq), flat(wuk), flat(wuv),
            _pad_last(q_gain, LANES), _pad_last(k_gain, LANES))


def _rope_tables():
    half = MLA_ROPE // 2
    inv_freq = ROPE_THETA ** (-jnp.arange(half, dtype=F32) / half)
    zeros = jnp.zeros((half,), F32)
    ones = jnp.ones((half,), F32)
    place = lambda lo, hi: jnp.pad(jnp.concatenate([lo, hi]),
                                   (MLA_NOPE, LANES - MLA_QK)).reshape(1, LANES)
    return (place(inv_freq, inv_freq), place(-ones, zeros), place(zeros, ones))


def _tile(n, pref):
    return pref if n % pref == 0 else n


def kernel(x, mem, positions, ffn_pre_norm, ffn_pre_w_gu, ffn_pre_w_down, mix_norm, sbg_w_in, sgu_ln_gain, sgu_ln_bias, sgu_w, sgu_b, sbg_w_out, mla_w_in, mla_q_lora_gain, mla_kv_lora_gain, mla_w_uq, mla_w_ukv, mla_q_gain, mla_k_gain, mla_w_out, xmem_norm, xmem_mem_norm, xmem_wq, xmem_wkv, xmem_q_gain, xmem_k_gain, xmem_wo, ffn_post_norm, ffn_post_w_gu, ffn_post_w_down):
    b, s, d = x.shape
    depth = ffn_pre_norm.shape[0]
    d_ff = ffn_pre_w_down.shape[1]
    t = b * s
    ffn_tm = _tile(t, 1024)
    ffn_tf = _tile(d_ff, 256)
    row_tm = _tile(s, 512)
    attn_blk = _tile(s, 256)
    bf = lambda a: a.astype(BF16)

    for layer in range(depth):
        x = _ffn(x.reshape(t, d), ffn_pre_norm[layer], bf(ffn_pre_w_gu[layer]),
                 bf(ffn_pre_w_down[layer]), tm=ffn_tm, tf=ffn_tf).reshape(b, s, d)
        if layer % 2 == 0:
            e = layer // 2
            bias_full = jnp.repeat(sgu_b[e].T, SG_GROUP_DIM, axis=1)
            q, k, v, o_sg = _even_prep(
                x.reshape(t, d), mix_norm[layer], bf(sbg_w_in[e]), sgu_ln_gain[e],
                sgu_ln_bias[e], sgu_w[e], bias_full, tm=_tile(t, 512))
            to3 = lambda a: a.reshape(b, s, -1)
            o_sb = _sb_attention(to3(q), to3(k), to3(v), blk=attn_blk)
            acts = (o_sb, to3(o_sg))
            w_mix = bf(sbg_w_out[e])
        else:
            o = layer // 2
            w_in, wuq, wuk, wuv, qg, kg = _mla_weights(
                mla_w_in[o], mla_w_uq[o], mla_w_ukv[o], mla_q_gain[o], mla_k_gain[o])
            q, k, v = _mla_prep(x, positions, mix_norm[layer], w_in, mla_q_lora_gain[o],
                                mla_kv_lora_gain[o], wuq, wuk, wuv, qg, kg, _rope_tables(),
                                tm=row_tm)
            acts = (_mla_attention(q, k, v, blk=attn_blk),)
            w_mix = bf(mla_w_out[o])
        mem_k, mem_v = _mem_kv(mem, xmem_mem_norm[layer], bf(xmem_wkv[layer]),
                               xmem_k_gain[layer])
        x = _mix_xattn(x, acts, w_mix, xmem_norm[layer], bf(xmem_wq[layer]), xmem_q_gain[layer],
                       mem_k, mem_v, bf(xmem_wo[layer]), tm=row_tm)
        x = _ffn(x.reshape(t, d), ffn_post_norm[layer], bf(ffn_post_w_gu[layer]),
                 bf(ffn_post_w_down[layer]), tm=ffn_tm, tf=ffn_tf).reshape(b, s, d)
    return x
```

```python
import functools

import jax
import jax.numpy as jnp
from jax import lax
from jax.experimental import pallas as pl
from jax.experimental.pallas import tpu as pltpu

EPS = 1e-6
ROPE_THETA = 10000.0
LANES = 128
VMEM_LIMIT_BYTES = 56 * 1024 * 1024

SB_HEADS, SB_HEAD_DIM = 8, 64
SB_WIDTH = SB_HEADS * SB_HEAD_DIM
SG_GROUPS, SG_GROUP_DIM, SG_CHUNK = 8, 64, 128
SG_WIDTH = SG_GROUPS * SG_GROUP_DIM
MLA_HEADS, MLA_NOPE, MLA_ROPE, MLA_V = 16, 64, 32, 64
MLA_QK = MLA_NOPE + MLA_ROPE
MLA_Q_LORA, MLA_KV_LORA = 512, 256
MEM_HEADS = 4

BF16 = jnp.bfloat16
F32 = jnp.float32
LOG2E = 1.4426950408889634
MASKED = -1e30


def _params(*semantics):
    return pltpu.CompilerParams(dimension_semantics=semantics,
                                vmem_limit_bytes=VMEM_LIMIT_BYTES)


def _dot(a, b):
    return jnp.dot(a, b, preferred_element_type=F32)


def _dot_nt(a, b):
    return lax.dot_general(a, b, (((1,), (1,)), ((), ())), preferred_element_type=F32)


def _rms(x, gain, n=None):
    n = x.shape[-1] if n is None else n
    ms = jnp.sum(x * x, axis=-1, keepdims=True) * (1.0 / n)
    return x * lax.rsqrt(ms + EPS) * gain


def _ffn_kernel(x_ref, g_ref, wg_ref, wu_ref, wd_ref, o_ref, h_ref, acc_ref):
    f = pl.program_id(1)

    @pl.when(f == 0)
    def _():
        h_ref[...] = _rms(x_ref[...], g_ref[...]).astype(BF16)
        acc_ref[...] = jnp.zeros_like(acc_ref)

    h = h_ref[...]
    gate = _dot(h, wg_ref[...])
    up = _dot(h, wu_ref[...])
    act = (gate * jax.nn.sigmoid(gate) * up).astype(BF16)
    acc_ref[...] += _dot(act, wd_ref[...])

    @pl.when(f == pl.num_programs(1) - 1)
    def _():
        o_ref[...] = x_ref[...] + 0.5 * acc_ref[...]


def _ffn(x2, gain, w_gu, w_down, *, tm, tf):
    t, d = x2.shape
    d_ff = w_down.shape[0]
    nf = d_ff // tf
    return pl.pallas_call(
        _ffn_kernel,
        grid=(t // tm, nf),
        in_specs=[
            pl.BlockSpec((tm, d), lambda i, f: (i, 0)),
            pl.BlockSpec((1, d), lambda i, f: (0, 0)),
            pl.BlockSpec((d, tf), lambda i, f: (0, f)),
            pl.BlockSpec((d, tf), lambda i, f: (0, f + nf)),
            pl.BlockSpec((tf, d), lambda i, f: (f, 0)),
        ],
        out_specs=pl.BlockSpec((tm, d), lambda i, f: (i, 0)),
        out_shape=jax.ShapeDtypeStruct((t, d), F32),
        scratch_shapes=[pltpu.VMEM((tm, d), BF16), pltpu.VMEM((tm, d), F32)],
        compiler_params=_params("parallel", "arbitrary"),
        name="ffn",
    )(x2, gain.reshape(1, d), w_gu, w_gu, w_down)


def _gelu_tanh(x):
    c = 0.7978845608028654
    return 0.5 * x * (1.0 + jnp.tanh(c * (x + 0.044715 * (x * x * x))))


def _even_prep_kernel(x_ref, g_ref, win_ref, lng_ref, lnb_ref, sw_ref, sb_ref,
                      qt_ref, k_ref, vt_ref, osg_ref):
    tm = x_ref.shape[1]
    kb = vt_ref.shape[3]
    w = SB_WIDTH
    h = _rms(x_ref[0], g_ref[...]).astype(BF16)
    qt_ref[0] = (_dot(h, win_ref[:, 0:w]) * (SB_HEAD_DIM ** -0.5 * LOG2E)).T.astype(BF16)
    k_ref[0] = _dot(h, win_ref[:, w:2 * w]).astype(BF16)
    v = _dot(h, win_ref[:, 2 * w:3 * w])
    for c in range(tm // kb):
        vt_ref[0, c] = v[c * kb:(c + 1) * kb, :].T.astype(BF16)
    osg_ref = osg_ref.at[0]
    u = _gelu_tanh(_dot(h, win_ref[:, 3 * w:3 * w + SG_WIDTH]))
    g = _gelu_tanh(_dot(h, win_ref[:, 3 * w + SG_WIDTH:3 * w + 2 * SG_WIDTH]))
    mu = jnp.mean(g, axis=-1, keepdims=True)
    gc = g - mu
    var = jnp.mean(gc * gc, axis=-1, keepdims=True)
    gn = (gc * lax.rsqrt(var + EPS) * lng_ref[...] + lnb_ref[...]).astype(BF16)

    row = lax.broadcasted_iota(jnp.int32, (SG_CHUNK, SG_CHUNK), 0)
    col = lax.broadcasted_iota(jnp.int32, (SG_CHUNK, SG_CHUNK), 1)
    tri = col <= row
    first_group = lax.broadcasted_iota(jnp.int32, (SG_CHUNK, LANES), 1) < SG_GROUP_DIM
    for p in range(SG_GROUPS // 2):
        lanes = slice(p * LANES, (p + 1) * LANES)
        w0 = jnp.where(tri, sw_ref[2 * p], 0.0).astype(BF16)
        w1 = jnp.where(tri, sw_ref[2 * p + 1], 0.0).astype(BF16)
        bias = sb_ref[:, lanes]
        for c in range(tm // SG_CHUNK):
            rows = slice(c * SG_CHUNK, (c + 1) * SG_CHUNK)
            gp = gn[rows, lanes]
            mixed = jnp.where(first_group, _dot(w0, gp), _dot(w1, gp)) + bias
            osg_ref[rows, lanes] = (u[rows, lanes] * mixed).astype(BF16)


def _even_prep(x, gain, w_in, ln_g, ln_b, sgu_w, sgu_bias_full, *, tm, kb):
    b, s, d = x.shape
    n_in = w_in.shape[1]
    w = SB_WIDTH
    const2 = lambda bi, i: (0, 0)
    row_out = jax.ShapeDtypeStruct((b, s, w), BF16)
    row_spec = pl.BlockSpec((1, tm, w), lambda bi, i: (bi, i, 0))
    return pl.pallas_call(
        _even_prep_kernel,
        grid=(b, s // tm),
        in_specs=[
            pl.BlockSpec((1, tm, d), lambda bi, i: (bi, i, 0)),
            pl.BlockSpec((1, d), const2),
            pl.BlockSpec((d, n_in), const2),
            pl.BlockSpec((1, SG_WIDTH), const2),
            pl.BlockSpec((1, SG_WIDTH), const2),
            pl.BlockSpec((SG_GROUPS, SG_CHUNK, SG_CHUNK), lambda bi, i: (0, 0, 0)),
            pl.BlockSpec((SG_CHUNK, SG_WIDTH), const2),
        ],
        out_specs=[pl.BlockSpec((1, w, tm), lambda bi, i: (bi, 0, i)),
                   row_spec,
                   pl.BlockSpec((1, tm // kb, w, kb), lambda bi, i: (bi, i, 0, 0)),
                   row_spec],
        out_shape=[jax.ShapeDtypeStruct((b, w, s), BF16), row_out,
                   jax.ShapeDtypeStruct((b, s // kb, w, kb), BF16), row_out],
        compiler_params=_params("parallel", "parallel"),
        name="even_prep",
    )(x, gain.reshape(1, d), w_in, ln_g.reshape(1, -1), ln_b.reshape(1, -1),
      sgu_w, sgu_bias_full)


def _sb_attn_kernel(qt_ref, k_ref, vt_ref, o_ref, acc_ref, r_ref, *, tq, kb):
    i = pl.program_id(2)
    n_sub = tq // kb
    hd_dim = SB_HEAD_DIM
    first_head = lax.broadcasted_iota(jnp.int32, (LANES, 1), 0) < hd_dim
    qt = qt_ref[0]
    zero = jnp.zeros_like(qt)
    qt_heads = (jnp.where(first_head, qt, zero), jnp.where(first_head, zero, qt))
    key = lax.broadcasted_iota(jnp.int32, (kb, tq), 0)
    query = lax.broadcasted_iota(jnp.int32, (kb, tq), 1)
    srow = lax.broadcasted_iota(jnp.int32, (kb, kb), 0)
    scol = lax.broadcasted_iota(jnp.int32, (kb, kb), 1)
    suffix = (scol >= srow).astype(BF16)

    def visit(g, diag_sub):
        start = pl.multiple_of(g * kb, kb)
        kblk = k_ref[0, pl.ds(start, kb), :]
        vt = vt_ref[0, g]
        visible = None if diag_sub is None else (key + diag_sub * kb) < query
        for hd in range(2):
            z = _dot(kblk, qt_heads[hd])
            sp = jnp.maximum(z, 0.0) + jnp.log2(1.0 + jnp.exp2(-jnp.abs(z)))
            if visible is not None:
                sp = jnp.where(visible, sp, 0.0)
            tail = _dot(suffix, sp.astype(BF16))
            wgt = jnp.exp2(z - tail - r_ref[hd])
            if visible is not None:
                wgt = jnp.where(visible, wgt, 0.0)
            acc_ref[hd] += _dot(vt[hd * hd_dim:(hd + 1) * hd_dim, :], wgt.astype(BF16))
            r_ref[hd] += jnp.sum(sp, axis=0, keepdims=True)

    acc_ref[...] = jnp.zeros_like(acc_ref)
    r_ref[...] = jnp.zeros_like(r_ref)
    for d in reversed(range(n_sub)):
        visit(i * n_sub + d, d)

    def body(it, carry):
        for u in reversed(range(n_sub)):
            visit((i - 1 - it) * n_sub + u, None)
        return carry

    lax.fori_loop(0, i, body, 0)
    out_t = jnp.concatenate([acc_ref[0], acc_ref[1]], axis=0)
    o_ref[0] = out_t.T.astype(BF16)


def _sb_attention(qt, k, vt, *, tq):
    b, w, s = qt.shape
    kb = vt.shape[3]
    return pl.pallas_call(
        functools.partial(_sb_attn_kernel, tq=tq, kb=kb),
        grid=(b, w // LANES, s // tq),
        in_specs=[
            pl.BlockSpec((1, LANES, tq), lambda bi, p, i: (bi, p, i)),
            pl.BlockSpec((1, s, LANES), lambda bi, p, i: (bi, 0, p)),
            pl.BlockSpec((1, s // kb, LANES, kb), lambda bi, p, i: (bi, 0, p, 0)),
        ],
        out_specs=pl.BlockSpec((1, tq, LANES), lambda bi, p, i: (bi, i, p)),
        out_shape=jax.ShapeDtypeStruct((b, s, w), BF16),
        scratch_shapes=[pltpu.VMEM((2, SB_HEAD_DIM, tq), F32), pltpu.VMEM((2, 1, tq), F32)],
        compiler_params=_params("parallel", "parallel", "arbitrary"),
        name="sb_attn",
    )(qt, k, vt)


def _mla_prep_kernel(x_ref, pos_ref, g_ref, win_ref, qlg_ref, kvlg_ref, wuq_ref, wuk_ref,
                     wuv_ref, qg_ref, kg_ref, freq_ref, sin_lo_ref, sin_hi_ref,
                     qt_ref, k_ref, vt_ref):
    tm = x_ref.shape[1]
    kb = vt_ref.shape[3]
    h = _rms(x_ref[0], g_ref[...]).astype(BF16)
    c_q = _dot(h, win_ref[:, 0:MLA_Q_LORA])
    c_kv = _dot(h, win_ref[:, MLA_Q_LORA:MLA_Q_LORA + MLA_KV_LORA])
    k_rope = _dot(h, win_ref[:, MLA_Q_LORA + MLA_KV_LORA:])
    cqn = _rms(c_q, qlg_ref[...]).astype(BF16)
    ckvn = _rms(c_kv, kvlg_ref[...]).astype(BF16)
    v = _dot(ckvn, wuv_ref[...])
    for c in range(tm // kb):
        vt_ref[0, c] = v[c * kb:(c + 1) * kb, :].T.astype(BF16)

    angle = pos_ref[0].astype(F32) * freq_ref[...]
    cos = jnp.cos(angle)
    sin = jnp.sin(angle)
    sin_lo = sin * sin_lo_ref[...]
    sin_hi = sin * sin_hi_ref[...]
    half = MLA_ROPE // 2

    def norm_rope(t, gain):
        tn = _rms(t, gain, MLA_QK)
        return (tn * cos + pltpu.roll(tn, LANES - half, 1) * sin_lo
                + pltpu.roll(tn, half, 1) * sin_hi)

    scale = MLA_QK ** -0.5 * LOG2E
    for pair in range(MLA_HEADS // 2):
        cols = slice(2 * pair * LANES, (2 * pair + 2) * LANES)
        q2 = _dot(cqn, wuq_ref[:, cols])
        k2 = _dot(ckvn, wuk_ref[:, cols])
        for sub in range(2):
            hd = 2 * pair + sub
            lanes = slice(sub * LANES, (sub + 1) * LANES)
            qt_ref[0, hd] = (norm_rope(q2[:, lanes], qg_ref[...]) * scale).T.astype(BF16)
            k_ref[0, hd] = norm_rope(k2[:, lanes] + k_rope, kg_ref[...]).astype(BF16)


def _mla_prep(x, positions, gain, w_in, qlg, kvlg, wuq, wuk, wuv, qg, kg, tables, *, tm, kb):
    b, s, d = x.shape
    vw = MLA_HEADS * MLA_V
    const2 = lambda bi, i: (0, 0)
    full = lambda a: pl.BlockSpec(a.shape, const2)
    vec = lambda a: a.reshape(1, -1)
    small = [vec(gain), w_in, vec(qlg), vec(kvlg), wuq, wuk, wuv, vec(qg), vec(kg), *tables]
    return pl.pallas_call(
        _mla_prep_kernel,
        grid=(b, s // tm),
        in_specs=[
            pl.BlockSpec((1, tm, d), lambda bi, i: (bi, i, 0)),
            pl.BlockSpec((1, tm, 1), lambda bi, i: (bi, i, 0)),
            *[full(a) for a in small],
        ],
        out_specs=[pl.BlockSpec((1, MLA_HEADS, LANES, tm), lambda bi, i: (bi, 0, 0, i)),
                   pl.BlockSpec((1, MLA_HEADS, tm, LANES), lambda bi, i: (bi, 0, i, 0)),
                   pl.BlockSpec((1, tm // kb, vw, kb), lambda bi, i: (bi, i, 0, 0))],
        out_shape=[jax.ShapeDtypeStruct((b, MLA_HEADS, LANES, s), BF16),
                   jax.ShapeDtypeStruct((b, MLA_HEADS, s, LANES), BF16),
                   jax.ShapeDtypeStruct((b, s // kb, vw, kb), BF16)],
        compiler_params=_params("parallel", "parallel"),
        name="mla_prep",
    )(x, positions.reshape(b, s, 1), *small)


def _mla_attn_kernel(qt_ref, k_ref, vt_ref, o_ref, acc_ref, m_ref, l_ref, s0_ref, s1_ref,
                     bm_ref, *, tq):
    i = pl.program_id(2)
    n_heads = qt_ref.shape[1]
    s_refs = (s0_ref, s1_ref)
    key = lax.broadcasted_iota(jnp.int32, (tq, tq), 0)
    query = lax.broadcasted_iota(jnp.int32, (tq, tq), 1)
    causal = key <= query

    def produce(g, hd, masked):
        start = pl.multiple_of(g * tq, tq)
        sc = _dot(k_ref[0, hd, pl.ds(start, tq), :], qt_ref[0, hd])
        if masked:
            sc = jnp.where(causal, sc, MASKED)
        s_refs[hd % 2][...] = sc
        bm_ref[hd % 2] = jnp.max(sc, axis=0, keepdims=True)

    def consume(g, hd):
        m_old = m_ref[hd]
        m_new = jnp.maximum(m_old, bm_ref[hd % 2])
        alpha = jnp.exp2(m_old - m_new)
        p = jnp.exp2(s_refs[hd % 2][...] - m_new)
        l_ref[hd] = alpha * l_ref[hd] + jnp.sum(p, axis=0, keepdims=True)
        vt = vt_ref[0, g, hd * MLA_V:(hd + 1) * MLA_V, :]
        acc_ref[hd] = alpha * acc_ref[hd] + _dot(vt, p.astype(BF16))
        m_ref[hd] = m_new

    def visit(g, g_next, masked):
        for hd in range(n_heads):
            if hd + 1 < n_heads:
                produce(g, hd + 1, masked)
            else:
                produce(g_next, 0, False)
            consume(g, hd)

    acc_ref[...] = jnp.zeros_like(acc_ref)
    l_ref[...] = jnp.zeros_like(l_ref)
    m_ref[...] = jnp.full_like(m_ref, MASKED)
    produce(i, 0, True)
    visit(i, 0, True)

    def body(it, carry):
        visit(it, jnp.minimum(it + 1, i - 1), False)
        return carry

    lax.fori_loop(0, i, body, 0)
    out_t = jnp.concatenate([acc_ref[hd] / l_ref[hd] for hd in range(n_heads)], axis=0)
    o_ref[0] = out_t.T.astype(BF16)


def _mla_attention(qt, k, vt, *, tq, heads_per_step):
    b, heads, _, s = qt.shape
    kb = vt.shape[3]
    nh = heads_per_step
    assert kb == tq and nh % 2 == 0
    vw = nh * MLA_V
    return pl.pallas_call(
        functools.partial(_mla_attn_kernel, tq=tq),
        grid=(b, heads // nh, s // tq),
        in_specs=[
            pl.BlockSpec((1, nh, LANES, tq), lambda bi, p, i: (bi, p, 0, i)),
            pl.BlockSpec((1, nh, s, LANES), lambda bi, p, i: (bi, p, 0, 0)),
            pl.BlockSpec((1, s // kb, vw, kb), lambda bi, p, i: (bi, 0, p, 0)),
        ],
        out_specs=pl.BlockSpec((1, tq, vw), lambda bi, p, i: (bi, i, p)),
        out_shape=jax.ShapeDtypeStruct((b, s, heads * MLA_V), BF16),
        scratch_shapes=[pltpu.VMEM((nh, MLA_V, tq), F32), pltpu.VMEM((nh, 1, tq), F32),
                        pltpu.VMEM((nh, 1, tq), F32), pltpu.VMEM((kb, tq), F32),
                        pltpu.VMEM((kb, tq), F32), pltpu.VMEM((2, 1, tq), F32)],
        compiler_params=_params("parallel", "parallel", "arbitrary"),
        name="mla_attn",
    )(qt, k, vt)


def _mem_kv_kernel(mem_ref, g_ref, wkv_ref, kg_ref, k_ref, v_ref):
    hm = _rms(mem_ref[0], g_ref[...]).astype(BF16)
    hd_dim = kg_ref.shape[1]
    for hd in range(MEM_HEADS):
        kcols = slice(2 * hd * hd_dim, (2 * hd + 1) * hd_dim)
        vcols = slice((2 * hd + 1) * hd_dim, (2 * hd + 2) * hd_dim)
        out = slice(hd * hd_dim, (hd + 1) * hd_dim)
        k_ref[0, :, out] = _rms(_dot(hm, wkv_ref[:, kcols]), kg_ref[...]).astype(BF16)
        v_ref[0, :, out] = _dot(hm, wkv_ref[:, vcols]).astype(BF16)


def _mem_kv(mem, gain, wkv, k_gain):
    b, m, d = mem.shape
    out = jax.ShapeDtypeStruct((b, m, d), BF16)
    spec = pl.BlockSpec((1, m, d), lambda bi: (bi, 0, 0))
    return pl.pallas_call(
        _mem_kv_kernel,
        grid=(b,),
        in_specs=[spec,
                  pl.BlockSpec((1, d), lambda bi: (0, 0)),
                  pl.BlockSpec(wkv.shape, lambda bi: (0, 0)),
                  pl.BlockSpec((1, k_gain.shape[0]), lambda bi: (0, 0))],
        out_specs=[spec, spec],
        out_shape=[out, out],
        compiler_params=_params("parallel"),
        name="mem_kv",
    )(mem, gain.reshape(1, d), wkv, k_gain.reshape(1, -1))


def _mix_xattn_kernel(*refs, n_act):
    x_ref = refs[0]
    act_refs = refs[1:1 + n_act]
    w_ref, g_ref, wq_ref, qg_ref, k_ref, v_ref, wo_ref, o_ref = refs[1 + n_act:]
    act = jnp.concatenate([a_ref[0] for a_ref in act_refs], axis=-1)
    x1 = x_ref[0] + _dot(act, w_ref[...])
    h = _rms(x1, g_ref[...]).astype(BF16)
    hd_dim = qg_ref.shape[1]
    out = x1
    for hd in range(MEM_HEADS):
        cols = slice(hd * hd_dim, (hd + 1) * hd_dim)
        qn = (_rms(_dot(h, wq_ref[:, cols]), qg_ref[...]) * (hd_dim ** -0.5)).astype(BF16)
        sc = _dot_nt(qn, k_ref[0, :, cols])
        p = jnp.exp(sc - jnp.max(sc, axis=-1, keepdims=True))
        denom = jnp.sum(p, axis=-1, keepdims=True)
        o_h = (_dot(p.astype(BF16), v_ref[0, :, cols]) / denom).astype(BF16)
        out = out + _dot(o_h, wo_ref[cols, :])
    o_ref[0] = out


def _mix_xattn(x, acts, w_mix, gain, wq, q_gain, mem_k, mem_v, wo, *, tm):
    b, s, d = x.shape
    m = mem_k.shape[1]
    const2 = lambda bi, i: (0, 0)
    row = lambda width: pl.BlockSpec((1, tm, width), lambda bi, i: (bi, i, 0))
    mem_spec = pl.BlockSpec((1, m, d), lambda bi, i: (bi, 0, 0))
    return pl.pallas_call(
        functools.partial(_mix_xattn_kernel, n_act=len(acts)),
        grid=(b, s // tm),
        in_specs=[
            row(d),
            *[row(a.shape[-1]) for a in acts],
            pl.BlockSpec(w_mix.shape, const2),
            pl.BlockSpec((1, d), const2),
            pl.BlockSpec(wq.shape, const2),
            pl.BlockSpec((1, q_gain.shape[0]), const2),
            mem_spec, mem_spec,
            pl.BlockSpec(wo.shape, const2),
        ],
        out_specs=row(d),
        out_shape=jax.ShapeDtypeStruct((b, s, d), F32),
        compiler_params=_params("parallel", "parallel"),
        name="mix_xattn",
    )(x, *acts, w_mix, gain.reshape(1, d), wq, q_gain.reshape(1, -1), mem_k, mem_v, wo)


def _pad_last(a, width):
    return jnp.pad(a, [(0, 0)] * (a.ndim - 1) + [(0, width - a.shape[-1])])


def _mla_weights(w_in, w_uq, w_ukv, q_gain, k_gain):
    lat = MLA_Q_LORA + MLA_KV_LORA
    k_rope_cols = jnp.pad(w_in[:, lat:], ((0, 0), (MLA_NOPE, LANES - MLA_QK)))
    w_in_ext = jnp.concatenate([w_in[:, :lat], k_rope_cols], axis=1)
    wuq = _pad_last(w_uq.reshape(MLA_Q_LORA, MLA_HEADS, MLA_QK), LANES)
    wukv = w_ukv.reshape(MLA_KV_LORA, MLA_HEADS, MLA_NOPE + MLA_V)
    wuk = _pad_last(wukv[..., :MLA_NOPE], LANES)
    wuv = wukv[..., MLA_NOPE:]
    flat = lambda a: a.reshape(a.shape[0], -1).astype(BF16)
    return (w_in_ext.astype(BF16), flat(wuq), flat(wuk), flat(wuv),
            _pad_last(q_gain, LANES), _pad_last(k_gain, LANES))


def _rope_tables():
    half = MLA_ROPE // 2
    inv_freq = ROPE_THETA ** (-jnp.arange(half, dtype=F32) / half)
    zeros = jnp.zeros((half,), F32)
    ones = jnp.ones((half,), F32)
    place = lambda lo, hi: jnp.pad(jnp.concatenate([lo, hi]),
                                   (MLA_NOPE, LANES - MLA_QK)).reshape(1, LANES)
    return (place(inv_freq, inv_freq), place(-ones, zeros), place(zeros, ones))


def _tile(n, pref):
    return pref if n % pref == 0 else n


def kernel(x, mem, positions, ffn_pre_norm, ffn_pre_w_gu, ffn_pre_w_down, mix_norm, sbg_w_in, sgu_ln_gain, sgu_ln_bias, sgu_w, sgu_b, sbg_w_out, mla_w_in, mla_q_lora_gain, mla_kv_lora_gain, mla_w_uq, mla_w_ukv, mla_q_gain, mla_k_gain, mla_w_out, xmem_norm, xmem_mem_norm, xmem_wq, xmem_wkv, xmem_q_gain, xmem_k_gain, xmem_wo, ffn_post_norm, ffn_post_w_gu, ffn_post_w_down):
    b, s, d = x.shape
    depth = ffn_pre_norm.shape[0]
    d_ff = ffn_pre_w_down.shape[1]
    t = b * s
    ffn_tm = _tile(t, 1024)
    ffn_tf = _tile(d_ff, 256)
    row_tm = _tile(s, 512)
    attn_tq = _tile(s, 512)
    sb_kb = _tile(attn_tq, 256)
    mla_kb = attn_tq
    bf = lambda a: a.astype(BF16)

    for layer in range(depth):
        x = _ffn(x.reshape(t, d), ffn_pre_norm[layer], bf(ffn_pre_w_gu[layer]),
                 bf(ffn_pre_w_down[layer]), tm=ffn_tm, tf=ffn_tf).reshape(b, s, d)
        if layer % 2 == 0:
            e = layer // 2
            bias_full = jnp.repeat(sgu_b[e].T, SG_GROUP_DIM, axis=1)
            qt, k, vt, o_sg = _even_prep(
                x, mix_norm[layer], bf(sbg_w_in[e]), sgu_ln_gain[e],
                sgu_ln_bias[e], sgu_w[e], bias_full, tm=row_tm, kb=sb_kb)
            acts = (_sb_attention(qt, k, vt, tq=attn_tq), o_sg)
            w_mix = bf(sbg_w_out[e])
        else:
            o = layer // 2
            w_in, wuq, wuk, wuv, qg, kg = _mla_weights(
                mla_w_in[o], mla_w_uq[o], mla_w_ukv[o], mla_q_gain[o], mla_k_gain[o])
            qt, k, vt = _mla_prep(x, positions, mix_norm[layer], w_in, mla_q_lora_gain[o],
                                  mla_kv_lora_gain[o], wuq, wuk, wuv, qg, kg, _rope_tables(),
                                  tm=row_tm, kb=mla_kb)
            acts = (_mla_attention(qt, k, vt, tq=attn_tq, heads_per_step=4),)
            w_mix = bf(mla_w_out[o])
        mem_k, mem_v = _mem_kv(mem, xmem_mem_norm[layer], bf(xmem_wkv[layer]),
                               xmem_k_gain[layer])
        x = _mix_xattn(x, acts, w_mix, xmem_norm[layer], bf(xmem_wq[layer]), xmem_q_gain[layer],
                       mem_k, mem_v, bf(xmem_wo[layer]), tm=row_tm)
        x = _ffn(x.reshape(t, d), ffn_post_norm[layer], bf(ffn_post_w_gu[layer]),
                 bf(ffn_post_w_down[layer]), tm=ffn_tm, tf=ffn_tf).reshape(b, s, d)
    return x
```

```python
import functools

import jax
import jax.numpy as jnp
from jax import lax
from jax.experimental import pallas as pl
from jax.experimental.pallas import tpu as pltpu

EPS = 1e-6
ROPE_THETA = 10000.0
LANES = 128
VMEM_LIMIT_BYTES = 56 * 1024 * 1024

SB_HEADS, SB_HEAD_DIM = 8, 64
SB_WIDTH = SB_HEADS * SB_HEAD_DIM
SG_GROUPS, SG_GROUP_DIM, SG_CHUNK = 8, 64, 128
SG_WIDTH = SG_GROUPS * SG_GROUP_DIM
MLA_HEADS, MLA_NOPE, MLA_ROPE, MLA_V = 16, 64, 32, 64
MLA_QK = MLA_NOPE + MLA_ROPE
MLA_Q_LORA, MLA_KV_LORA = 512, 256
MEM_HEADS = 4

BF16 = jnp.bfloat16
F32 = jnp.float32
LOG2E = 1.4426950408889634
MASKED = -1e30
UNDERFLOW_LOG2 = 160.0


def _params(*semantics):
    return pltpu.CompilerParams(dimension_semantics=semantics,
                                vmem_limit_bytes=VMEM_LIMIT_BYTES)


def _dot(a, b):
    return jnp.dot(a, b, preferred_element_type=F32)


def _dot_nt(a, b):
    return lax.dot_general(a, b, (((1,), (1,)), ((), ())), preferred_element_type=F32)


def _rms(x, gain, n=None):
    n = x.shape[-1] if n is None else n
    ms = jnp.sum(x * x, axis=-1, keepdims=True) * (1.0 / n)
    return x * lax.rsqrt(ms + EPS) * gain


def _ffn_kernel(x_ref, g_ref, wg_ref, wu_ref, wd_ref, o_ref, h_ref, acc_ref):
    f = pl.program_id(1)

    @pl.when(f == 0)
    def _():
        h_ref[...] = _rms(x_ref[...], g_ref[...]).astype(BF16)
        acc_ref[...] = jnp.zeros_like(acc_ref)

    h = h_ref[...]
    gate = _dot(h, wg_ref[...])
    up = _dot(h, wu_ref[...])
    act = (gate * jax.nn.sigmoid(gate) * up).astype(BF16)
    acc_ref[...] += _dot(act, wd_ref[...])

    @pl.when(f == pl.num_programs(1) - 1)
    def _():
        o_ref[...] = x_ref[...] + 0.5 * acc_ref[...]


def _ffn(x2, gain, w_gu, w_down, *, tm, tf):
    t, d = x2.shape
    d_ff = w_down.shape[0]
    nf = d_ff // tf
    return pl.pallas_call(
        _ffn_kernel,
        grid=(t // tm, nf),
        in_specs=[
            pl.BlockSpec((tm, d), lambda i, f: (i, 0)),
            pl.BlockSpec((1, d), lambda i, f: (0, 0)),
            pl.BlockSpec((d, tf), lambda i, f: (0, f)),
            pl.BlockSpec((d, tf), lambda i, f: (0, f + nf)),
            pl.BlockSpec((tf, d), lambda i, f: (f, 0)),
        ],
        out_specs=pl.BlockSpec((tm, d), lambda i, f: (i, 0)),
        out_shape=jax.ShapeDtypeStruct((t, d), F32),
        scratch_shapes=[pltpu.VMEM((tm, d), BF16), pltpu.VMEM((tm, d), F32)],
        compiler_params=_params("parallel", "arbitrary"),
        name="ffn",
    )(x2, gain.reshape(1, d), w_gu, w_gu, w_down)


def _gelu_tanh(x):
    c = 0.7978845608028654
    return 0.5 * x * (1.0 + jnp.tanh(c * (x + 0.044715 * (x * x * x))))


def _even_prep_kernel(x_ref, g_ref, win_ref, lng_ref, lnb_ref, sw_ref, sb_ref,
                      qt_ref, k_ref, vt_ref, osg_ref):
    tm = x_ref.shape[1]
    kb = vt_ref.shape[3]
    w = SB_WIDTH
    h = _rms(x_ref[0], g_ref[...]).astype(BF16)
    qt_ref[0] = (_dot(h, win_ref[:, 0:w]) * (SB_HEAD_DIM ** -0.5 * LOG2E)).T.astype(BF16)
    k_ref[0] = _dot(h, win_ref[:, w:2 * w]).astype(BF16)
    v = _dot(h, win_ref[:, 2 * w:3 * w])
    for c in range(tm // kb):
        vt_ref[0, c] = v[c * kb:(c + 1) * kb, :].T.astype(BF16)
    osg_ref = osg_ref.at[0]
    u = _gelu_tanh(_dot(h, win_ref[:, 3 * w:3 * w + SG_WIDTH]))
    g = _gelu_tanh(_dot(h, win_ref[:, 3 * w + SG_WIDTH:3 * w + 2 * SG_WIDTH]))
    mu = jnp.mean(g, axis=-1, keepdims=True)
    gc = g - mu
    var = jnp.mean(gc * gc, axis=-1, keepdims=True)
    gn = (gc * lax.rsqrt(var + EPS) * lng_ref[...] + lnb_ref[...]).astype(BF16)

    row = lax.broadcasted_iota(jnp.int32, (SG_CHUNK, SG_CHUNK), 0)
    col = lax.broadcasted_iota(jnp.int32, (SG_CHUNK, SG_CHUNK), 1)
    tri = col <= row
    first_group = lax.broadcasted_iota(jnp.int32, (SG_CHUNK, LANES), 1) < SG_GROUP_DIM
    for p in range(SG_GROUPS // 2):
        lanes = slice(p * LANES, (p + 1) * LANES)
        w0 = jnp.where(tri, sw_ref[2 * p], 0.0).astype(BF16)
        w1 = jnp.where(tri, sw_ref[2 * p + 1], 0.0).astype(BF16)
        bias = sb_ref[:, lanes]
        for c in range(tm // SG_CHUNK):
            rows = slice(c * SG_CHUNK, (c + 1) * SG_CHUNK)
            gp = gn[rows, lanes]
            mixed = jnp.where(first_group, _dot(w0, gp), _dot(w1, gp)) + bias
            osg_ref[rows, lanes] = (u[rows, lanes] * mixed).astype(BF16)


def _even_prep(x, gain, w_in, ln_g, ln_b, sgu_w, sgu_bias_full, *, tm, kb):
    b, s, d = x.shape
    n_in = w_in.shape[1]
    w = SB_WIDTH
    const2 = lambda bi, i: (0, 0)
    row_out = jax.ShapeDtypeStruct((b, s, w), BF16)
    row_spec = pl.BlockSpec((1, tm, w), lambda bi, i: (bi, i, 0))
    return pl.pallas_call(
        _even_prep_kernel,
        grid=(b, s // tm),
        in_specs=[
            pl.BlockSpec((1, tm, d), lambda bi, i: (bi, i, 0)),
            pl.BlockSpec((1, d), const2),
            pl.BlockSpec((d, n_in), const2),
            pl.BlockSpec((1, SG_WIDTH), const2),
            pl.BlockSpec((1, SG_WIDTH), const2),
            pl.BlockSpec((SG_GROUPS, SG_CHUNK, SG_CHUNK), lambda bi, i: (0, 0, 0)),
            pl.BlockSpec((SG_CHUNK, SG_WIDTH), const2),
        ],
        out_specs=[pl.BlockSpec((1, w, tm), lambda bi, i: (bi, 0, i)),
                   row_spec,
                   pl.BlockSpec((1, tm // kb, w, kb), lambda bi, i: (bi, i, 0, 0)),
                   row_spec],
        out_shape=[jax.ShapeDtypeStruct((b, w, s), BF16), row_out,
                   jax.ShapeDtypeStruct((b, s // kb, w, kb), BF16), row_out],
        compiler_params=_params("parallel", "parallel"),
        name="even_prep",
    )(x, gain.reshape(1, d), w_in, ln_g.reshape(1, -1), ln_b.reshape(1, -1),
      sgu_w, sgu_bias_full)


def _sb_attn_kernel(qt_ref, k_ref, vt_ref, o_ref, acc_ref, r_ref, z0_ref, z1_ref, zc0_ref,
                    zc1_ref, t0_ref, t1_ref, bs_ref, kn_ref, *, tq, kb):
    i = pl.program_id(2)
    n_sub = tq // kb
    hd_dim = SB_HEAD_DIM
    z_refs, zc_refs, t_refs = (z0_ref, z1_ref), (zc0_ref, zc1_ref), (t0_ref, t1_ref)
    first_head = lax.broadcasted_iota(jnp.int32, (LANES, 1), 0) < hd_dim
    qt = qt_ref[0]
    zero = jnp.zeros_like(qt)
    qt_heads = (jnp.where(first_head, qt, zero), jnp.where(first_head, zero, qt))
    key = lax.broadcasted_iota(jnp.int32, (kb, tq), 0)
    query = lax.broadcasted_iota(jnp.int32, (kb, tq), 1)
    srow = lax.broadcasted_iota(jnp.int32, (kb, kb), 0)
    scol = lax.broadcasted_iota(jnp.int32, (kb, kb), 1)
    suffix = (scol >= srow).astype(BF16)

    def score(item):
        g, hd, slot = item
        start = pl.multiple_of(g * kb, kb)
        z_refs[slot][...] = _dot(k_ref[0, pl.ds(start, kb), :], qt_heads[hd])

    def stay(item, diag_sub):
        g, hd, slot = item
        z = z_refs[slot][...]
        sp = jnp.maximum(z, 0.0) + jnp.log2(1.0 + jnp.exp2(-jnp.abs(z)))
        if diag_sub is not None:
            visible = (key + diag_sub * kb) < query
            sp = jnp.where(visible, sp, 0.0)
            z = jnp.where(visible, z, MASKED)
        zc_refs[slot][...] = z
        t_refs[slot][...] = _dot(suffix, sp.astype(BF16))
        bs_ref[slot] = jnp.sum(sp, axis=0, keepdims=True)

    def weigh(item):
        g, hd, slot = item
        wgt = jnp.exp2(zc_refs[slot][...] - t_refs[slot][...] - r_ref[hd])
        vt = vt_ref[0, g, hd * hd_dim:(hd + 1) * hd_dim, :]
        acc_ref[hd] += _dot(vt, wgt.astype(BF16))
        r_ref[hd] += bs_ref[slot]

    def group(row, prev_item, next_first_block, diagonal):
        subs = [u for u in reversed(range(n_sub)) for _ in range(2)]
        items = [(row * n_sub + u, idx % 2, idx % 2) for idx, u in enumerate(subs)]
        for idx, item in enumerate(items):
            score(items[idx + 1] if idx + 1 < len(items) else (next_first_block, 0, 0))
            weigh(items[idx - 1] if idx > 0 else prev_item)
            stay(item, subs[idx] if diagonal else None)

    @pl.when(i == 0)
    def _():
        kf = k_ref[0].astype(F32)
        dim = lax.broadcasted_iota(jnp.int32, (LANES, LANES), 0)
        head = lax.broadcasted_iota(jnp.int32, (LANES, LANES), 1)
        select = ((dim < hd_dim) == (head == 0)) & (head < 2)
        kn_ref[...] = jnp.max(_dot((kf * kf).astype(BF16), select.astype(BF16)),
                              axis=0, keepdims=True)

    lane = lax.broadcasted_iota(jnp.int32, (1, LANES), 1)
    qf = qt.astype(F32)
    exit_level = []
    for hd in range(2):
        q_sq = jnp.sum(jnp.square(qf[hd * hd_dim:(hd + 1) * hd_dim, :]), axis=0, keepdims=True)
        k_sq = jnp.max(jnp.where(lane == hd, kn_ref[...], 0.0), axis=1, keepdims=True)
        exit_level.append(1.02 * jnp.sqrt(q_sq * k_sq) + UNDERFLOW_LOG2)

    acc_ref[...] = jnp.zeros_like(acc_ref)
    r_ref[...] = jnp.zeros_like(r_ref)
    zc1_ref[...] = jnp.full_like(zc1_ref, MASKED)
    t1_ref[...] = jnp.zeros_like(t1_ref)
    bs_ref[...] = jnp.zeros_like(bs_ref)
    last_sub = n_sub - 1
    score((i * n_sub + last_sub, 0, 0))
    group(i, (i * n_sub, 1, 1), jnp.maximum(i - 1, 0) * n_sub + last_sub, True)

    def more(carry):
        row, live = carry
        return (row >= 0) & (live > 0)

    def body(carry):
        row, _ = carry
        group(row, ((row + 1) * n_sub, 1, 1), jnp.maximum(row - 1, 0) * n_sub + last_sub, False)
        dead = ((jnp.min(r_ref[0] - exit_level[0]) > 0.0)
                & (jnp.min(r_ref[1] - exit_level[1]) > 0.0))
        return row - 1, jnp.where(dead, 0, 1).astype(jnp.int32)

    row_end, _ = lax.while_loop(more, body, (i - 1, jnp.int32(1)))
    weigh(((row_end + 1) * n_sub, 1, 1))
    out_t = jnp.concatenate([acc_ref[0], acc_ref[1]], axis=0)
    o_ref[0] = out_t.T.astype(BF16)


def _sb_attention(qt, k, vt, *, tq):
    b, w, s = qt.shape
    kb = vt.shape[3]
    return pl.pallas_call(
        functools.partial(_sb_attn_kernel, tq=tq, kb=kb),
        grid=(b, w // LANES, s // tq),
        in_specs=[
            pl.BlockSpec((1, LANES, tq), lambda bi, p, i: (bi, p, i)),
            pl.BlockSpec((1, s, LANES), lambda bi, p, i: (bi, 0, p)),
            pl.BlockSpec((1, s // kb, LANES, kb), lambda bi, p, i: (bi, 0, p, 0)),
        ],
        out_specs=pl.BlockSpec((1, tq, LANES), lambda bi, p, i: (bi, i, p)),
        out_shape=jax.ShapeDtypeStruct((b, s, w), BF16),
        scratch_shapes=[pltpu.VMEM((2, SB_HEAD_DIM, tq), F32), pltpu.VMEM((2, 1, tq), F32),
                        *[pltpu.VMEM((kb, tq), F32) for _ in range(6)],
                        pltpu.VMEM((2, 1, tq), F32), pltpu.VMEM((1, LANES), F32)],
        compiler_params=_params("parallel", "parallel", "arbitrary"),
        name="sb_attn",
    )(qt, k, vt)


def _mla_prep_kernel(x_ref, pos_ref, g_ref, win_ref, qlg_ref, kvlg_ref, wuq_ref, wuk_ref,
                     wuv_ref, qg_ref, kg_ref, freq_ref, sin_lo_ref, sin_hi_ref,
                     qt_ref, k_ref, vt_ref):
    tm = x_ref.shape[1]
    kb = vt_ref.shape[3]
    h = _rms(x_ref[0], g_ref[...]).astype(BF16)
    c_q = _dot(h, win_ref[:, 0:MLA_Q_LORA])
    c_kv = _dot(h, win_ref[:, MLA_Q_LORA:MLA_Q_LORA + MLA_KV_LORA])
    k_rope = _dot(h, win_ref[:, MLA_Q_LORA + MLA_KV_LORA:])
    cqn = _rms(c_q, qlg_ref[...]).astype(BF16)
    ckvn = _rms(c_kv, kvlg_ref[...]).astype(BF16)
    v = _dot(ckvn, wuv_ref[...])
    for c in range(tm // kb):
        vt_ref[0, c] = v[c * kb:(c + 1) * kb, :].T.astype(BF16)

    angle = pos_ref[0].astype(F32) * freq_ref[...]
    cos = jnp.cos(angle)
    sin = jnp.sin(angle)
    sin_lo = sin * sin_lo_ref[...]
    sin_hi = sin * sin_hi_ref[...]
    half = MLA_ROPE // 2

    def norm_rope(t, gain):
        tn = _rms(t, gain, MLA_QK)
        return (tn * cos + pltpu.roll(tn, LANES - half, 1) * sin_lo
                + pltpu.roll(tn, half, 1) * sin_hi)

    scale = MLA_QK ** -0.5 * LOG2E
    for pair in range(MLA_HEADS // 2):
        cols = slice(2 * pair * LANES, (2 * pair + 2) * LANES)
        q2 = _dot(cqn, wuq_ref[:, cols])
        k2 = _dot(ckvn, wuk_ref[:, cols])
        for sub in range(2):
            hd = 2 * pair + sub
            lanes = slice(sub * LANES, (sub + 1) * LANES)
            qt_ref[0, hd] = (norm_rope(q2[:, lanes], qg_ref[...]) * scale).T.astype(BF16)
            k_ref[0, hd] = norm_rope(k2[:, lanes] + k_rope, kg_ref[...]).astype(BF16)


def _mla_prep(x, positions, gain, w_in, qlg, kvlg, wuq, wuk, wuv, qg, kg, tables, *, tm, kb):
    b, s, d = x.shape
    vw = MLA_HEADS * MLA_V
    const2 = lambda bi, i: (0, 0)
    full = lambda a: pl.BlockSpec(a.shape, const2)
    vec = lambda a: a.reshape(1, -1)
    small = [vec(gain), w_in, vec(qlg), vec(kvlg), wuq, wuk, wuv, vec(qg), vec(kg), *tables]
    return pl.pallas_call(
        _mla_prep_kernel,
        grid=(b, s // tm),
        in_specs=[
            pl.BlockSpec((1, tm, d), lambda bi, i: (bi, i, 0)),
            pl.BlockSpec((1, tm, 1), lambda bi, i: (bi, i, 0)),
            *[full(a) for a in small],
        ],
        out_specs=[pl.BlockSpec((1, MLA_HEADS, LANES, tm), lambda bi, i: (bi, 0, 0, i)),
                   pl.BlockSpec((1, MLA_HEADS, tm, LANES), lambda bi, i: (bi, 0, i, 0)),
                   pl.BlockSpec((1, tm // kb, vw, kb), lambda bi, i: (bi, i, 0, 0))],
        out_shape=[jax.ShapeDtypeStruct((b, MLA_HEADS, LANES, s), BF16),
                   jax.ShapeDtypeStruct((b, MLA_HEADS, s, LANES), BF16),
                   jax.ShapeDtypeStruct((b, s // kb, vw, kb), BF16)],
        compiler_params=_params("parallel", "parallel"),
        name="mla_prep",
    )(x, positions.reshape(b, s, 1), *small)


def _mla_attn_kernel(qt_ref, k_ref, vt_ref, o_ref, acc_ref, m_ref, l_ref, s0_ref, s1_ref,
                     bm_ref, *, tq):
    i = pl.program_id(2)
    n_heads = qt_ref.shape[1]
    s_refs = (s0_ref, s1_ref)
    key = lax.broadcasted_iota(jnp.int32, (tq, tq), 0)
    query = lax.broadcasted_iota(jnp.int32, (tq, tq), 1)
    causal = key <= query

    def produce(g, hd, masked):
        start = pl.multiple_of(g * tq, tq)
        sc = _dot(k_ref[0, hd, pl.ds(start, tq), :], qt_ref[0, hd])
        if masked:
            sc = jnp.where(causal, sc, MASKED)
        s_refs[hd % 2][...] = sc
        bm_ref[hd % 2] = jnp.max(sc, axis=0, keepdims=True)

    def consume(g, hd):
        m_old = m_ref[hd]
        m_new = jnp.maximum(m_old, bm_ref[hd % 2])
        alpha = jnp.exp2(m_old - m_new)
        p = jnp.exp2(s_refs[hd % 2][...] - m_new)
        l_ref[hd] = alpha * l_ref[hd] + jnp.sum(p, axis=0, keepdims=True)
        vt = vt_ref[0, g, hd * MLA_V:(hd + 1) * MLA_V, :]
        acc_ref[hd] = alpha * acc_ref[hd] + _dot(vt, p.astype(BF16))
        m_ref[hd] = m_new

    def visit(g, g_next, masked):
        for hd in range(n_heads):
            if hd + 1 < n_heads:
                produce(g, hd + 1, masked)
            else:
                produce(g_next, 0, False)
            consume(g, hd)

    acc_ref[...] = jnp.zeros_like(acc_ref)
    l_ref[...] = jnp.zeros_like(l_ref)
    m_ref[...] = jnp.full_like(m_ref, MASKED)
    produce(i, 0, True)
    visit(i, 0, True)

    def body(it, carry):
        visit(it, jnp.minimum(it + 1, i - 1), False)
        return carry

    lax.fori_loop(0, i, body, 0)
    out_t = jnp.concatenate([acc_ref[hd] / l_ref[hd] for hd in range(n_heads)], axis=0)
    o_ref[0] = out_t.T.astype(BF16)


def _mla_attention(qt, k, vt, *, tq, heads_per_step):
    b, heads, _, s = qt.shape
    kb = vt.shape[3]
    nh = heads_per_step
    assert kb == tq and nh % 2 == 0
    vw = nh * MLA_V
    return pl.pallas_call(
        functools.partial(_mla_attn_kernel, tq=tq),
        grid=(b, heads // nh, s // tq),
        in_specs=[
            pl.BlockSpec((1, nh, LANES, tq), lambda bi, p, i: (bi, p, 0, i)),
            pl.BlockSpec((1, nh, s, LANES), lambda bi, p, i: (bi, p, 0, 0)),
            pl.BlockSpec((1, s // kb, vw, kb), lambda bi, p, i: (bi, 0, p, 0)),
        ],
        out_specs=pl.BlockSpec((1, tq, vw), lambda bi, p, i: (bi, i, p)),
        out_shape=jax.ShapeDtypeStruct((b, s, heads * MLA_V), BF16),
        scratch_shapes=[pltpu.VMEM((nh, MLA_V, tq), F32), pltpu.VMEM((nh, 1, tq), F32),
                        pltpu.VMEM((nh, 1, tq), F32), pltpu.VMEM((kb, tq), F32),
                        pltpu.VMEM((kb, tq), F32), pltpu.VMEM((2, 1, tq), F32)],
        compiler_params=_params("parallel", "parallel", "arbitrary"),
        name="mla_attn",
    )(qt, k, vt)


def _mem_kv_kernel(mem_ref, g_ref, wkv_ref, kg_ref, k_ref, v_ref):
    hm = _rms(mem_ref[0], g_ref[...]).astype(BF16)
    hd_dim = kg_ref.shape[1]
    for hd in range(MEM_HEADS):
        kcols = slice(2 * hd * hd_dim, (2 * hd + 1) * hd_dim)
        vcols = slice((2 * hd + 1) * hd_dim, (2 * hd + 2) * hd_dim)
        out = slice(hd * hd_dim, (hd + 1) * hd_dim)
        k_ref[0, :, out] = _rms(_dot(hm, wkv_ref[:, kcols]), kg_ref[...]).astype(BF16)
        v_ref[0, :, out] = _dot(hm, wkv_ref[:, vcols]).astype(BF16)


def _mem_kv(mem, gain, wkv, k_gain):
    b, m, d = mem.shape
    out = jax.ShapeDtypeStruct((b, m, d), BF16)
    spec = pl.BlockSpec((1, m, d), lambda bi: (bi, 0, 0))
    return pl.pallas_call(
        _mem_kv_kernel,
        grid=(b,),
        in_specs=[spec,
                  pl.BlockSpec((1, d), lambda bi: (0, 0)),
                  pl.BlockSpec(wkv.shape, lambda bi: (0, 0)),
                  pl.BlockSpec((1, k_gain.shape[0]), lambda bi: (0, 0))],
        out_specs=[spec, spec],
        out_shape=[out, out],
        compiler_params=_params("parallel"),
        name="mem_kv",
    )(mem, gain.reshape(1, d), wkv, k_gain.reshape(1, -1))


def _mix_xattn_kernel(*refs, n_act):
    x_ref = refs[0]
    act_refs = refs[1:1 + n_act]
    w_ref, g_ref, wq_ref, qg_ref, k_ref, v_ref, wo_ref, o_ref = refs[1 + n_act:]
    act = jnp.concatenate([a_ref[0] for a_ref in act_refs], axis=-1)
    x1 = x_ref[0] + _dot(act, w_ref[...])
    h = _rms(x1, g_ref[...]).astype(BF16)
    hd_dim = qg_ref.shape[1]
    out = x1
    for hd in range(MEM_HEADS):
        cols = slice(hd * hd_dim, (hd + 1) * hd_dim)
        qn = (_rms(_dot(h, wq_ref[:, cols]), qg_ref[...]) * (hd_dim ** -0.5)).astype(BF16)
        sc = _dot_nt(qn, k_ref[0, :, cols])
        p = jnp.exp(sc - jnp.max(sc, axis=-1, keepdims=True))
        denom = jnp.sum(p, axis=-1, keepdims=True)
        o_h = (_dot(p.astype(BF16), v_ref[0, :, cols]) / denom).astype(BF16)
        out = out + _dot(o_h, wo_ref[cols, :])
    o_ref[0] = out


def _mix_xattn(x, acts, w_mix, gain, wq, q_gain, mem_k, mem_v, wo, *, tm):
    b, s, d = x.shape
    m = mem_k.shape[1]
    const2 = lambda bi, i: (0, 0)
    row = lambda width: pl.BlockSpec((1, tm, width), lambda bi, i: (bi, i, 0))
    mem_spec = pl.BlockSpec((1, m, d), lambda bi, i: (bi, 0, 0))
    return pl.pallas_call(
        functools.partial(_mix_xattn_kernel, n_act=len(acts)),
        grid=(b, s // tm),
        in_specs=[
            row(d),
            *[row(a.shape[-1]) for a in acts],
            pl.BlockSpec(w_mix.shape, const2),
            pl.BlockSpec((1, d), const2),
            pl.BlockSpec(wq.shape, const2),
            pl.BlockSpec((1, q_gain.shape[0]), const2),
            mem_spec, mem_spec,
            pl.BlockSpec(wo.shape, const2),
        ],
        out_specs=row(d),
        out_shape=jax.ShapeDtypeStruct((b, s, d), F32),
        compiler_params=_params("parallel", "parallel"),
        name="mix_xattn",
    )(x, *acts, w_mix, gain.reshape(1, d), wq, q_gain.reshape(1, -1), mem_k, mem_v, wo)


def _pad_last(a, width):
    return jnp.pad(a, [(0, 0)] * (a.ndim - 1) + [(0, width - a.shape[-1])])


def _mla_weights(w_in, w_uq, w_ukv, q_gain, k_gain):
    lat = MLA_Q_LORA + MLA_KV_LORA
    k_rope_cols = jnp.pad(w_in[:, lat:], ((0, 0), (MLA_NOPE, LANES - MLA_QK)))
    w_in_ext = jnp.concatenate([w_in[:, :lat], k_rope_cols], axis=1)
    wuq = _pad_last(w_uq.reshape(MLA_Q_LORA, MLA_HEADS, MLA_QK), LANES)
    wukv = w_ukv.reshape(MLA_KV_LORA, MLA_HEADS, MLA_NOPE + MLA_V)
    wuk = _pad_last(wukv[..., :MLA_NOPE], LANES)
    wuv = wukv[..., MLA_NOPE:]
    flat = lambda a: a.reshape(a.shape[0], -1).astype(BF16)
    return (w_in_ext.astype(BF16), flat(wuq), flat(wuk), flat(wuv),
            _pad_last(q_gain, LANES), _pad_last(k_gain, LANES))


def _rope_tables():
    half = MLA_ROPE // 2
    inv_freq = ROPE_THETA ** (-jnp.arange(half, dtype=F32) / half)
    zeros = jnp.zeros((half,), F32)
    ones = jnp.ones((half,), F32)
    place = lambda lo, hi: jnp.pad(jnp.concatenate([lo, hi]),
                                   (MLA_NOPE, LANES - MLA_QK)).reshape(1, LANES)
    return (place(inv_freq, inv_freq), place(-ones, zeros), place(zeros, ones))


def _tile(n, pref):
    return pref if n % pref == 0 else n


def kernel(x, mem, positions, ffn_pre_norm, ffn_pre_w_gu, ffn_pre_w_down, mix_norm, sbg_w_in, sgu_ln_gain, sgu_ln_bias, sgu_w, sgu_b, sbg_w_out, mla_w_in, mla_q_lora_gain, mla_kv_lora_gain, mla_w_uq, mla_w_ukv, mla_q_gain, mla_k_gain, mla_w_out, xmem_norm, xmem_mem_norm, xmem_wq, xmem_wkv, xmem_q_gain, xmem_k_gain, xmem_wo, ffn_post_norm, ffn_post_w_gu, ffn_post_w_down):
    b, s, d = x.shape
    depth = ffn_pre_norm.shape[0]
    d_ff = ffn_pre_w_down.shape[1]
    t = b * s
    ffn_tm = _tile(t, 1024)
    ffn_tf = _tile(d_ff, 256)
    row_tm = _tile(s, 512)
    attn_tq = _tile(s, 512)
    sb_kb = _tile(attn_tq, 256)
    mla_kb = attn_tq
    bf = lambda a: a.astype(BF16)

    for layer in range(depth):
        x = _ffn(x.reshape(t, d), ffn_pre_norm[layer], bf(ffn_pre_w_gu[layer]),
                 bf(ffn_pre_w_down[layer]), tm=ffn_tm, tf=ffn_tf).reshape(b, s, d)
        if layer % 2 == 0:
            e = layer // 2
            bias_full = jnp.repeat(sgu_b[e].T, SG_GROUP_DIM, axis=1)
            qt, k, vt, o_sg = _even_prep(
                x, mix_norm[layer], bf(sbg_w_in[e]), sgu_ln_gain[e],
                sgu_ln_bias[e], sgu_w[e], bias_full, tm=row_tm, kb=sb_kb)
            acts = (_sb_attention(qt, k, vt, tq=attn_tq), o_sg)
            w_mix = bf(sbg_w_out[e])
        else:
            o = layer // 2
            w_in, wuq, wuk, wuv, qg, kg = _mla_weights(
                mla_w_in[o], mla_w_uq[o], mla_w_ukv[o], mla_q_gain[o], mla_k_gain[o])
            qt, k, vt = _mla_prep(x, positions, mix_norm[layer], w_in, mla_q_lora_gain[o],
                                  mla_kv_lora_gain[o], wuq, wuk, wuv, qg, kg, _rope_tables(),
                                  tm=row_tm, kb=mla_kb)
            acts = (_mla_attention(qt, k, vt, tq=attn_tq, heads_per_step=4),)
            w_mix = bf(mla_w_out[o])
        mem_k, mem_v = _mem_kv(mem, xmem_mem_norm[layer], bf(xmem_wkv[layer]),
                               xmem_k_gain[layer])
        x = _mix_xattn(x, acts, w_mix, xmem_norm[layer], bf(xmem_wq[layer]), xmem_q_gain[layer],
                       mem_k, mem_v, bf(xmem_wo[layer]), tm=row_tm)
        x = _ffn(x.reshape(t, d), ffn_post_norm[layer], bf(ffn_post_w_gu[layer]),
                 bf(ffn_post_w_down[layer]), tm=ffn_tm, tf=ffn_tf).reshape(b, s, d)
    return x
```

```python
import functools

import jax
import jax.numpy as jnp
from jax import lax
from jax.experimental import pallas as pl
from jax.experimental.pallas import tpu as pltpu

EPS = 1e-6
ROPE_THETA = 10000.0
LANES = 128
VMEM_LIMIT_BYTES = 56 * 1024 * 1024

SB_HEADS, SB_HEAD_DIM = 8, 64
SB_WIDTH = SB_HEADS * SB_HEAD_DIM
SG_GROUPS, SG_GROUP_DIM, SG_CHUNK = 8, 64, 128
SG_WIDTH = SG_GROUPS * SG_GROUP_DIM
MLA_HEADS, MLA_NOPE, MLA_ROPE, MLA_V = 16, 64, 32, 64
MLA_QK = MLA_NOPE + MLA_ROPE
MLA_Q_LORA, MLA_KV_LORA = 512, 256
MEM_HEADS = 4

BF16 = jnp.bfloat16
F32 = jnp.float32
LOG2E = 1.4426950408889634
MASKED = -1e30
SUM_ROWS = 16
UNDERFLOW_LOG2 = 160.0


def _params(*semantics):
    return pltpu.CompilerParams(dimension_semantics=semantics,
                                vmem_limit_bytes=VMEM_LIMIT_BYTES)


def _dot(a, b):
    return jnp.dot(a, b, preferred_element_type=F32)


def _dot_nt(a, b):
    return lax.dot_general(a, b, (((1,), (1,)), ((), ())), preferred_element_type=F32)


def _rms(x, gain, n=None):
    n = x.shape[-1] if n is None else n
    ms = jnp.sum(x * x, axis=-1, keepdims=True) * (1.0 / n)
    return x * lax.rsqrt(ms + EPS) * gain


def _ffn_kernel(x_ref, g_ref, wg_ref, wu_ref, wd_ref, o_ref, h_ref, acc_ref):
    f = pl.program_id(1)

    @pl.when(f == 0)
    def _():
        h_ref[...] = _rms(x_ref[...], g_ref[...]).astype(BF16)
        acc_ref[...] = jnp.zeros_like(acc_ref)

    h = h_ref[...]
    gate = _dot(h, wg_ref[...].astype(BF16))
    up = _dot(h, wu_ref[...].astype(BF16))
    act = (gate * jax.nn.sigmoid(gate) * up).astype(BF16)
    acc_ref[...] += _dot(act, wd_ref[...].astype(BF16))

    @pl.when(f == pl.num_programs(1) - 1)
    def _():
        o_ref[...] = x_ref[...] + 0.5 * acc_ref[...]


def _ffn(x2, gain, w_gu, w_down, layer, *, tm, tf):
    t, d = x2.shape
    d_ff = w_down.shape[1]
    nf = d_ff // tf
    return pl.pallas_call(
        _ffn_kernel,
        grid=(t // tm, nf),
        in_specs=[
            pl.BlockSpec((tm, d), lambda i, f: (i, 0)),
            pl.BlockSpec((1, d), lambda i, f: (0, 0)),
            pl.BlockSpec((None, d, tf), lambda i, f: (layer, 0, f)),
            pl.BlockSpec((None, d, tf), lambda i, f: (layer, 0, f + nf)),
            pl.BlockSpec((None, tf, d), lambda i, f: (layer, f, 0)),
        ],
        out_specs=pl.BlockSpec((tm, d), lambda i, f: (i, 0)),
        out_shape=jax.ShapeDtypeStruct((t, d), F32),
        scratch_shapes=[pltpu.VMEM((tm, d), BF16), pltpu.VMEM((tm, d), F32)],
        compiler_params=_params("parallel", "arbitrary"),
        name="ffn",
    )(x2, gain.reshape(1, d), w_gu, w_gu, w_down)


def _gelu_tanh(x):
    c = 0.7978845608028654
    return 0.5 * x * (1.0 + jnp.tanh(c * (x + 0.044715 * (x * x * x))))


def _even_prep_kernel(x_ref, g_ref, win_ref, lng_ref, lnb_ref, sw_ref, sb_ref,
                      qt_ref, k_ref, vt_ref, osg_ref):
    tm = x_ref.shape[1]
    kb = vt_ref.shape[3]
    w = SB_WIDTH
    h = _rms(x_ref[0], g_ref[...]).astype(BF16)
    qt_ref[0] = (_dot(h, win_ref[:, 0:w]) * (SB_HEAD_DIM ** -0.5 * LOG2E)).T.astype(BF16)
    k_ref[0] = _dot(h, win_ref[:, w:2 * w]).astype(BF16)
    v = _dot(h, win_ref[:, 2 * w:3 * w])
    for c in range(tm // kb):
        vt_ref[0, c] = v[c * kb:(c + 1) * kb, :].T.astype(BF16)
    osg_ref = osg_ref.at[0]
    u = _gelu_tanh(_dot(h, win_ref[:, 3 * w:3 * w + SG_WIDTH]))
    g = _gelu_tanh(_dot(h, win_ref[:, 3 * w + SG_WIDTH:3 * w + 2 * SG_WIDTH]))
    mu = jnp.mean(g, axis=-1, keepdims=True)
    gc = g - mu
    var = jnp.mean(gc * gc, axis=-1, keepdims=True)
    gn = (gc * lax.rsqrt(var + EPS) * lng_ref[...] + lnb_ref[...]).astype(BF16)

    row = lax.broadcasted_iota(jnp.int32, (SG_CHUNK, SG_CHUNK), 0)
    col = lax.broadcasted_iota(jnp.int32, (SG_CHUNK, SG_CHUNK), 1)
    tri = col <= row
    first_group = lax.broadcasted_iota(jnp.int32, (SG_CHUNK, LANES), 1) < SG_GROUP_DIM
    for p in range(SG_GROUPS // 2):
        lanes = slice(p * LANES, (p + 1) * LANES)
        w0 = jnp.where(tri, sw_ref[2 * p], 0.0).astype(BF16)
        w1 = jnp.where(tri, sw_ref[2 * p + 1], 0.0).astype(BF16)
        bias = sb_ref[:, lanes]
        for c in range(tm // SG_CHUNK):
            rows = slice(c * SG_CHUNK, (c + 1) * SG_CHUNK)
            gp = gn[rows, lanes]
            mixed = jnp.where(first_group, _dot(w0, gp), _dot(w1, gp)) + bias
            osg_ref[rows, lanes] = (u[rows, lanes] * mixed).astype(BF16)


def _even_prep(x, gain, w_in, ln_g, ln_b, sgu_w, sgu_bias_full, *, tm, kb):
    b, s, d = x.shape
    n_in = w_in.shape[1]
    w = SB_WIDTH
    const2 = lambda bi, i: (0, 0)
    row_out = jax.ShapeDtypeStruct((b, s, w), BF16)
    row_spec = pl.BlockSpec((1, tm, w), lambda bi, i: (bi, i, 0))
    return pl.pallas_call(
        _even_prep_kernel,
        grid=(b, s // tm),
        in_specs=[
            pl.BlockSpec((1, tm, d), lambda bi, i: (bi, i, 0)),
            pl.BlockSpec((1, d), const2),
            pl.BlockSpec((d, n_in), const2),
            pl.BlockSpec((1, SG_WIDTH), const2),
            pl.BlockSpec((1, SG_WIDTH), const2),
            pl.BlockSpec((SG_GROUPS, SG_CHUNK, SG_CHUNK), lambda bi, i: (0, 0, 0)),
            pl.BlockSpec((SG_CHUNK, SG_WIDTH), const2),
        ],
        out_specs=[pl.BlockSpec((1, w, tm), lambda bi, i: (bi, 0, i)),
                   row_spec,
                   pl.BlockSpec((1, tm // kb, w, kb), lambda bi, i: (bi, i, 0, 0)),
                   row_spec],
        out_shape=[jax.ShapeDtypeStruct((b, w, s), BF16), row_out,
                   jax.ShapeDtypeStruct((b, s // kb, w, kb), BF16), row_out],
        compiler_params=_params("parallel", "parallel"),
        name="even_prep",
    )(x, gain.reshape(1, d), w_in, ln_g.reshape(1, -1), ln_b.reshape(1, -1),
      sgu_w, sgu_bias_full)


def _sb_attn_kernel(qt_ref, k_ref, vt_ref, o_ref, acc_ref, r_ref, z0_ref, z1_ref, zc0_ref,
                    zc1_ref, t0_ref, t1_ref, bs_ref, kn_ref, *, tq, kb):
    i = pl.program_id(2)
    n_sub = tq // kb
    hd_dim = SB_HEAD_DIM
    z_refs, zc_refs, t_refs = (z0_ref, z1_ref), (zc0_ref, zc1_ref), (t0_ref, t1_ref)
    first_head = lax.broadcasted_iota(jnp.int32, (LANES, 1), 0) < hd_dim
    qt = qt_ref[0]
    zero = jnp.zeros_like(qt)
    qt_heads = (jnp.where(first_head, qt, zero), jnp.where(first_head, zero, qt))
    key = lax.broadcasted_iota(jnp.int32, (kb, tq), 0)
    query = lax.broadcasted_iota(jnp.int32, (kb, tq), 1)
    srow = lax.broadcasted_iota(jnp.int32, (kb, kb), 0)
    scol = lax.broadcasted_iota(jnp.int32, (kb, kb), 1)
    suffix = (scol >= srow).astype(BF16)

    def score(item):
        g, hd, slot = item
        start = pl.multiple_of(g * kb, kb)
        z_refs[slot][...] = _dot(k_ref[0, pl.ds(start, kb), :], qt_heads[hd])

    def stay(item, diag_sub):
        g, hd, slot = item
        z = z_refs[slot][...]
        sp = jnp.maximum(z, 0.0) + jnp.log2(1.0 + jnp.exp2(-jnp.abs(z)))
        if diag_sub is not None:
            visible = (key + diag_sub * kb) < query
            sp = jnp.where(visible, sp, 0.0)
            z = jnp.where(visible, z, MASKED)
        zc_refs[slot][...] = z
        t_refs[slot][...] = _dot(suffix, sp.astype(BF16))
        bs_ref[slot] = jnp.sum(sp, axis=0, keepdims=True)

    def weigh(item):
        g, hd, slot = item
        wgt = jnp.exp2(zc_refs[slot][...] - t_refs[slot][...] - r_ref[hd])
        vt = vt_ref[0, g, hd * hd_dim:(hd + 1) * hd_dim, :]
        acc_ref[hd] += _dot(vt, wgt.astype(BF16))
        r_ref[hd] += bs_ref[slot]

    def group(row, prev_item, next_first_block, diagonal):
        subs = [u for u in reversed(range(n_sub)) for _ in range(2)]
        items = [(row * n_sub + u, idx % 2, idx % 2) for idx, u in enumerate(subs)]
        for idx, item in enumerate(items):
            score(items[idx + 1] if idx + 1 < len(items) else (next_first_block, 0, 0))
            weigh(items[idx - 1] if idx > 0 else prev_item)
            stay(item, subs[idx] if diagonal else None)

    @pl.when(i == 0)
    def _():
        kf = k_ref[0].astype(F32)
        dim = lax.broadcasted_iota(jnp.int32, (LANES, LANES), 0)
        head = lax.broadcasted_iota(jnp.int32, (LANES, LANES), 1)
        select = ((dim < hd_dim) == (head == 0)) & (head < 2)
        kn_ref[...] = jnp.max(_dot((kf * kf).astype(BF16), select.astype(BF16)),
                              axis=0, keepdims=True)

    lane = lax.broadcasted_iota(jnp.int32, (1, LANES), 1)
    qf = qt.astype(F32)
    exit_level = []
    for hd in range(2):
        q_sq = jnp.sum(jnp.square(qf[hd * hd_dim:(hd + 1) * hd_dim, :]), axis=0, keepdims=True)
        k_sq = jnp.max(jnp.where(lane == hd, kn_ref[...], 0.0), axis=1, keepdims=True)
        exit_level.append(1.02 * jnp.sqrt(q_sq * k_sq) + UNDERFLOW_LOG2)

    acc_ref[...] = jnp.zeros_like(acc_ref)
    r_ref[...] = jnp.zeros_like(r_ref)
    zc1_ref[...] = jnp.full_like(zc1_ref, MASKED)
    t1_ref[...] = jnp.zeros_like(t1_ref)
    bs_ref[...] = jnp.zeros_like(bs_ref)
    last_sub = n_sub - 1
    score((i * n_sub + last_sub, 0, 0))
    group(i, (i * n_sub, 1, 1), jnp.maximum(i - 1, 0) * n_sub + last_sub, True)

    def more(carry):
        row, live = carry
        return (row >= 0) & (live > 0)

    def body(carry):
        row, _ = carry
        group(row, ((row + 1) * n_sub, 1, 1), jnp.maximum(row - 1, 0) * n_sub + last_sub, False)
        dead = ((jnp.min(r_ref[0] - exit_level[0]) > 0.0)
                & (jnp.min(r_ref[1] - exit_level[1]) > 0.0))
        return row - 1, jnp.where(dead, 0, 1).astype(jnp.int32)

    row_end, _ = lax.while_loop(more, body, (i - 1, jnp.int32(1)))
    weigh(((row_end + 1) * n_sub, 1, 1))
    out_t = jnp.concatenate([acc_ref[0], acc_ref[1]], axis=0)
    o_ref[0] = out_t.T.astype(BF16)


def _sb_attention(qt, k, vt, *, tq):
    b, w, s = qt.shape
    kb = vt.shape[3]
    return pl.pallas_call(
        functools.partial(_sb_attn_kernel, tq=tq, kb=kb),
        grid=(b, w // LANES, s // tq),
        in_specs=[
            pl.BlockSpec((1, LANES, tq), lambda bi, p, i: (bi, p, i)),
            pl.BlockSpec((1, s, LANES), lambda bi, p, i: (bi, 0, p)),
            pl.BlockSpec((1, s // kb, LANES, kb), lambda bi, p, i: (bi, 0, p, 0)),
        ],
        out_specs=pl.BlockSpec((1, tq, LANES), lambda bi, p, i: (bi, i, p)),
        out_shape=jax.ShapeDtypeStruct((b, s, w), BF16),
        scratch_shapes=[pltpu.VMEM((2, SB_HEAD_DIM, tq), F32), pltpu.VMEM((2, 1, tq), F32),
                        *[pltpu.VMEM((kb, tq), F32) for _ in range(6)],
                        pltpu.VMEM((2, 1, tq), F32), pltpu.VMEM((1, LANES), F32)],
        compiler_params=_params("parallel", "parallel", "arbitrary"),
        name="sb_attn",
    )(qt, k, vt)


def _lane_tile(t, width):
    return jnp.concatenate([t] * (width // t.shape[1]), axis=1)


def _rms_rows(xt, gain):
    ms = jnp.sum(xt * xt, axis=0, keepdims=True) * (1.0 / xt.shape[0])
    return xt * lax.rsqrt(ms + EPS) * _lane_tile(gain, xt.shape[1])


def _mla_prep_kernel(x_ref, pos_ref, g_ref, win_ref, qlg_ref, kvlg_ref, wuqt_ref, wukt_ref,
                     wuvt_ref, qg_ref, kg_ref, freq_ref, qt_ref, k_ref, vt_ref):
    tm = x_ref.shape[1]
    kb = vt_ref.shape[3]
    lat = MLA_Q_LORA + MLA_KV_LORA
    half = MLA_ROPE // 2
    h = _rms(x_ref[0], g_ref[...]).astype(BF16)
    ct = _dot(h, win_ref[...]).T
    cqn = _rms_rows(ct[0:MLA_Q_LORA], qlg_ref[...]).astype(BF16)
    ckvn = _rms_rows(ct[MLA_Q_LORA:lat], kvlg_ref[...]).astype(BF16)
    k_r = ct[lat:lat + MLA_ROPE]
    vt = _dot(wuvt_ref[...], ckvn)
    for c in range(tm // kb):
        vt_ref[0, c] = vt[:, c * kb:(c + 1) * kb].astype(BF16)
    q_all = _dot(wuqt_ref[...], cqn)
    kn_all = _dot(wukt_ref[...], ckvn)

    angle = _lane_tile(freq_ref[...], tm) * pos_ref[0].astype(F32)
    cos = jnp.cos(angle)
    sin = jnp.sin(angle)

    def rope(t):
        t1, t2 = t[0:half], t[half:]
        return jnp.concatenate([t1 * cos - t2 * sin, t1 * sin + t2 * cos], axis=0)

    qg = _lane_tile(qg_ref[...], tm)
    kg = _lane_tile(kg_ref[...], tm)
    zeros = jnp.zeros((LANES - MLA_QK, tm), F32)
    inv_n = 1.0 / MLA_QK
    kr_sq = jnp.sum(k_r * k_r, axis=0, keepdims=True)
    kr_roped = rope(k_r * kg[MLA_NOPE:])
    for hd in range(MLA_HEADS):
        qh = q_all[hd * MLA_QK:(hd + 1) * MLA_QK]
        r = lax.rsqrt(jnp.sum(qh * qh, axis=0, keepdims=True) * inv_n + EPS)
        qn = qh * r * qg
        qt_ref[0, hd] = jnp.concatenate([qn[0:MLA_NOPE], rope(qn[MLA_NOPE:]), zeros],
                                        axis=0).astype(BF16)
        kn = kn_all[hd * MLA_NOPE:(hd + 1) * MLA_NOPE]
        r = lax.rsqrt((jnp.sum(kn * kn, axis=0, keepdims=True) + kr_sq) * inv_n + EPS)
        kt = jnp.concatenate([kn * r * kg[0:MLA_NOPE], kr_roped * r, zeros], axis=0)
        k_ref[0, hd] = kt.T.astype(BF16)


def _mla_prep(x, positions, gain, w_in, qlg, kvlg, wuqt, wukt, wuvt, qg, kg, freq, *, tm, kb):
    b, s, d = x.shape
    vw = MLA_HEADS * MLA_V
    const2 = lambda bi, i: (0, 0)
    full = lambda a: pl.BlockSpec(a.shape, const2)
    small = [gain.reshape(1, -1), w_in, qlg, kvlg, wuqt, wukt, wuvt, qg, kg, freq]
    return pl.pallas_call(
        _mla_prep_kernel,
        grid=(b, s // tm),
        in_specs=[
            pl.BlockSpec((1, tm, d), lambda bi, i: (bi, i, 0)),
            pl.BlockSpec((1, 1, tm), lambda bi, i: (bi, 0, i)),
            *[full(a) for a in small],
        ],
        out_specs=[pl.BlockSpec((1, MLA_HEADS, LANES, tm), lambda bi, i: (bi, 0, 0, i)),
                   pl.BlockSpec((1, MLA_HEADS, tm, LANES), lambda bi, i: (bi, 0, i, 0)),
                   pl.BlockSpec((1, tm // kb, vw, kb), lambda bi, i: (bi, i, 0, 0))],
        out_shape=[jax.ShapeDtypeStruct((b, MLA_HEADS, LANES, s), BF16),
                   jax.ShapeDtypeStruct((b, MLA_HEADS, s, LANES), BF16),
                   jax.ShapeDtypeStruct((b, s // kb, vw, kb), BF16)],
        compiler_params=_params("parallel", "parallel"),
        name="mla_prep",
    )(x, positions.reshape(b, 1, s), *small)


def _mla_attn_kernel(qt_ref, k_ref, vt_ref, o_ref, acc_ref, m_ref, s0_ref, s1_ref, bm_ref,
                     *, tq):
    i = pl.program_id(2)
    n_heads = qt_ref.shape[1]
    s_refs = (s0_ref, s1_ref)
    key = lax.broadcasted_iota(jnp.int32, (tq, tq), 0)
    query = lax.broadcasted_iota(jnp.int32, (tq, tq), 1)
    causal = key <= query

    def produce(g, hd, masked):
        start = pl.multiple_of(g * tq, tq)
        sc = _dot(k_ref[0, hd, pl.ds(start, tq), :], qt_ref[0, hd])
        if masked:
            sc = jnp.where(causal, sc, MASKED)
        s_refs[hd % 2][...] = sc
        bm_ref[hd % 2] = jnp.max(sc, axis=0, keepdims=True)

    ones_rows = jnp.ones((SUM_ROWS, tq), BF16)

    def consume(g, hd):
        m_old = m_ref[hd]
        m_new = jnp.maximum(m_old, bm_ref[hd % 2])
        alpha = jnp.exp2(m_old - m_new)
        p = jnp.exp2(s_refs[hd % 2][...] - m_new)
        vt = jnp.concatenate([vt_ref[0, g, hd * MLA_V:(hd + 1) * MLA_V, :], ones_rows], axis=0)
        acc_ref[hd] = alpha * acc_ref[hd] + _dot(vt, p.astype(BF16))
        m_ref[hd] = m_new

    def visit(g, g_next, masked):
        for hd in range(n_heads):
            if hd + 1 < n_heads:
                produce(g, hd + 1, masked)
            else:
                produce(g_next, 0, False)
            consume(g, hd)

    acc_ref[...] = jnp.zeros_like(acc_ref)
    m_ref[...] = jnp.full_like(m_ref, MASKED)
    produce(i, 0, True)
    visit(i, 0, True)

    def body(it, carry):
        visit(it, jnp.minimum(it + 1, i - 1), False)
        return carry

    lax.fori_loop(0, i, body, 0)
    out_t = jnp.concatenate([acc_ref[hd, 0:MLA_V, :] / acc_ref[hd, MLA_V:MLA_V + 1, :]
                             for hd in range(n_heads)], axis=0)
    o_ref[0] = out_t.T.astype(BF16)


def _mla_attention(qt, k, vt, *, tq, heads_per_step):
    b, heads, _, s = qt.shape
    kb = vt.shape[3]
    nh = heads_per_step
    assert kb == tq and nh % 2 == 0
    vw = nh * MLA_V
    return pl.pallas_call(
        functools.partial(_mla_attn_kernel, tq=tq),
        grid=(b, heads // nh, s // tq),
        in_specs=[
            pl.BlockSpec((1, nh, LANES, tq), lambda bi, p, i: (bi, p, 0, i)),
            pl.BlockSpec((1, nh, s, LANES), lambda bi, p, i: (bi, p, 0, 0)),
            pl.BlockSpec((1, s // kb, vw, kb), lambda bi, p, i: (bi, 0, p, 0)),
        ],
        out_specs=pl.BlockSpec((1, tq, vw), lambda bi, p, i: (bi, i, p)),
        out_shape=jax.ShapeDtypeStruct((b, s, heads * MLA_V), BF16),
        scratch_shapes=[pltpu.VMEM((nh, MLA_V + SUM_ROWS, tq), F32),
                        pltpu.VMEM((nh, 1, tq), F32), pltpu.VMEM((kb, tq), F32),
                        pltpu.VMEM((kb, tq), F32), pltpu.VMEM((2, 1, tq), F32)],
        compiler_params=_params("parallel", "parallel", "arbitrary"),
        name="mla_attn",
    )(qt, k, vt)


def _mem_kv_kernel(mem_ref, g_ref, wkv_ref, kg_ref, k_ref, v_ref):
    hm = _rms(mem_ref[0], g_ref[...]).astype(BF16)
    hd_dim = kg_ref.shape[1]
    for hd in range(MEM_HEADS):
        kcols = slice(2 * hd * hd_dim, (2 * hd + 1) * hd_dim)
        vcols = slice((2 * hd + 1) * hd_dim, (2 * hd + 2) * hd_dim)
        out = slice(hd * hd_dim, (hd + 1) * hd_dim)
        k_ref[0, :, out] = _rms(_dot(hm, wkv_ref[:, kcols]), kg_ref[...]).astype(BF16)
        v_ref[0, :, out] = _dot(hm, wkv_ref[:, vcols]).astype(BF16)


def _mem_kv(mem, gain, wkv, k_gain):
    b, m, d = mem.shape
    out = jax.ShapeDtypeStruct((b, m, d), BF16)
    spec = pl.BlockSpec((1, m, d), lambda bi: (bi, 0, 0))
    return pl.pallas_call(
        _mem_kv_kernel,
        grid=(b,),
        in_specs=[spec,
                  pl.BlockSpec((1, d), lambda bi: (0, 0)),
                  pl.BlockSpec(wkv.shape, lambda bi: (0, 0)),
                  pl.BlockSpec((1, k_gain.shape[0]), lambda bi: (0, 0))],
        out_specs=[spec, spec],
        out_shape=[out, out],
        compiler_params=_params("parallel"),
        name="mem_kv",
    )(mem, gain.reshape(1, d), wkv, k_gain.reshape(1, -1))


def _mix_xattn_kernel(*refs, n_act):
    x_ref = refs[0]
    act_refs = refs[1:1 + n_act]
    w_ref, g_ref, wq_ref, qg_ref, k_ref, v_ref, wo_ref, o_ref = refs[1 + n_act:]
    act = jnp.concatenate([a_ref[0] for a_ref in act_refs], axis=-1)
    x1 = x_ref[0] + _dot(act, w_ref[...])
    h = _rms(x1, g_ref[...]).astype(BF16)
    hd_dim = qg_ref.shape[1]
    out = x1
    for hd in range(MEM_HEADS):
        cols = slice(hd * hd_dim, (hd + 1) * hd_dim)
        qn = (_rms(_dot(h, wq_ref[:, cols]), qg_ref[...]) * (hd_dim ** -0.5)).astype(BF16)
        sc = _dot_nt(qn, k_ref[0, :, cols])
        p = jnp.exp(sc - jnp.max(sc, axis=-1, keepdims=True))
        denom = jnp.sum(p, axis=-1, keepdims=True)
        o_h = (_dot(p.astype(BF16), v_ref[0, :, cols]) / denom).astype(BF16)
        out = out + _dot(o_h, wo_ref[cols, :])
    o_ref[0] = out


def _mix_xattn(x, acts, w_mix, gain, wq, q_gain, mem_k, mem_v, wo, *, tm):
    b, s, d = x.shape
    m = mem_k.shape[1]
    const2 = lambda bi, i: (0, 0)
    row = lambda width: pl.BlockSpec((1, tm, width), lambda bi, i: (bi, i, 0))
    mem_spec = pl.BlockSpec((1, m, d), lambda bi, i: (bi, 0, 0))
    return pl.pallas_call(
        functools.partial(_mix_xattn_kernel, n_act=len(acts)),
        grid=(b, s // tm),
        in_specs=[
            row(d),
            *[row(a.shape[-1]) for a in acts],
            pl.BlockSpec(w_mix.shape, const2),
            pl.BlockSpec((1, d), const2),
            pl.BlockSpec(wq.shape, const2),
            pl.BlockSpec((1, q_gain.shape[0]), const2),
            mem_spec, mem_spec,
            pl.BlockSpec(wo.shape, const2),
        ],
        out_specs=row(d),
        out_shape=jax.ShapeDtypeStruct((b, s, d), F32),
        compiler_params=_params("parallel", "parallel"),
        name="mix_xattn",
    )(x, *acts, w_mix, gain.reshape(1, d), wq, q_gain.reshape(1, -1), mem_k, mem_v, wo)


def _lane_bcast(vec):
    return jnp.broadcast_to(vec[:, None], (vec.shape[0], LANES))


def _mla_weights(w_in, w_uq, w_ukv, q_lora_gain, kv_lora_gain, q_gain, k_gain):
    lat = MLA_Q_LORA + MLA_KV_LORA
    w_in_ext = jnp.pad(w_in, ((0, 0), (0, lat + LANES - w_in.shape[1])))
    wukv = w_ukv.reshape(MLA_KV_LORA, MLA_HEADS, MLA_NOPE + MLA_V)
    wukt = wukv[..., :MLA_NOPE].reshape(MLA_KV_LORA, -1).T
    wuvt = wukv[..., MLA_NOPE:].reshape(MLA_KV_LORA, -1).T
    half = MLA_ROPE // 2
    inv_freq = ROPE_THETA ** (-jnp.arange(half, dtype=F32) / half)
    bf = lambda a: a.astype(BF16)
    return (bf(w_in_ext), _lane_bcast(q_lora_gain), _lane_bcast(kv_lora_gain),
            bf(w_uq.T), bf(wukt), bf(wuvt),
            _lane_bcast(q_gain * (MLA_QK ** -0.5 * LOG2E)), _lane_bcast(k_gain),
            _lane_bcast(inv_freq))


def _tile(n, pref):
    return pref if n % pref == 0 else n


def kernel(x, mem, positions, ffn_pre_norm, ffn_pre_w_gu, ffn_pre_w_down, mix_norm, sbg_w_in, sgu_ln_gain, sgu_ln_bias, sgu_w, sgu_b, sbg_w_out, mla_w_in, mla_q_lora_gain, mla_kv_lora_gain, mla_w_uq, mla_w_ukv, mla_q_gain, mla_k_gain, mla_w_out, xmem_norm, xmem_mem_norm, xmem_wq, xmem_wkv, xmem_q_gain, xmem_k_gain, xmem_wo, ffn_post_norm, ffn_post_w_gu, ffn_post_w_down):
    b, s, d = x.shape
    depth = ffn_pre_norm.shape[0]
    d_ff = ffn_pre_w_down.shape[1]
    t = b * s
    ffn_tm = _tile(t, 1024)
    ffn_tf = _tile(d_ff, 256)
    row_tm = _tile(s, 512)
    attn_tq = _tile(s, 512)
    sb_kb = _tile(attn_tq, 256)
    mla_kb = attn_tq
    bf = lambda a: a.astype(BF16)

    for layer in range(depth):
        x = _ffn(x.reshape(t, d), ffn_pre_norm[layer], ffn_pre_w_gu, ffn_pre_w_down, layer,
                 tm=ffn_tm, tf=ffn_tf).reshape(b, s, d)
        if layer % 2 == 0:
            e = layer // 2
            bias_full = jnp.repeat(sgu_b[e].T, SG_GROUP_DIM, axis=1)
            qt, k, vt, o_sg = _even_prep(
                x, mix_norm[layer], bf(sbg_w_in[e]), sgu_ln_gain[e],
                sgu_ln_bias[e], sgu_w[e], bias_full, tm=row_tm, kb=sb_kb)
            acts = (_sb_attention(qt, k, vt, tq=attn_tq), o_sg)
            w_mix = bf(sbg_w_out[e])
        else:
            o = layer // 2
            mla_consts = _mla_weights(
                mla_w_in[o], mla_w_uq[o], mla_w_ukv[o], mla_q_lora_gain[o],
                mla_kv_lora_gain[o], mla_q_gain[o], mla_k_gain[o])
            qt, k, vt = _mla_prep(x, positions, mix_norm[layer], *mla_consts,
                                  tm=row_tm, kb=mla_kb)
            acts = (_mla_attention(qt, k, vt, tq=attn_tq, heads_per_step=4),)
            w_mix = bf(mla_w_out[o])
        mem_k, mem_v = _mem_kv(mem, xmem_mem_norm[layer], bf(xmem_wkv[layer]),
                               xmem_k_gain[layer])
        x = _mix_xattn(x, acts, w_mix, xmem_norm[layer], bf(xmem_wq[layer]), xmem_q_gain[layer],
                       mem_k, mem_v, bf(xmem_wo[layer]), tm=row_tm)
        x = _ffn(x.reshape(t, d), ffn_post_norm[layer], ffn_post_w_gu, ffn_post_w_down, layer,
                 tm=ffn_tm, tf=ffn_tf).reshape(b, s, d)
    return x
```

```python
import functools

import jax
import jax.numpy as jnp
from jax import lax
from jax.experimental import pallas as pl
from jax.experimental.pallas import tpu as pltpu

EPS = 1e-6
ROPE_THETA = 10000.0
LANES = 128
VMEM_LIMIT_BYTES = 56 * 1024 * 1024

SB_HEADS, SB_HEAD_DIM = 8, 64
SB_WIDTH = SB_HEADS * SB_HEAD_DIM
SG_GROUPS, SG_GROUP_DIM, SG_CHUNK = 8, 64, 128
SG_WIDTH = SG_GROUPS * SG_GROUP_DIM
MLA_HEADS, MLA_NOPE, MLA_ROPE, MLA_V = 16, 64, 32, 64
MLA_QK = MLA_NOPE + MLA_ROPE
MLA_Q_LORA, MLA_KV_LORA = 512, 256
MEM_HEADS = 4

BF16 = jnp.bfloat16
F32 = jnp.float32
LOG2E = 1.4426950408889634
MASKED = -1e30
SUM_ROWS = 16
UNDERFLOW_LOG2 = 160.0


def _params(*semantics):
    return pltpu.CompilerParams(dimension_semantics=semantics,
                                vmem_limit_bytes=VMEM_LIMIT_BYTES)


def _dot(a, b):
    return jnp.dot(a, b, preferred_element_type=F32)


def _dot_nt(a, b):
    return lax.dot_general(a, b, (((1,), (1,)), ((), ())), preferred_element_type=F32)


def _rms(x, gain, n=None):
    n = x.shape[-1] if n is None else n
    ms = jnp.sum(x * x, axis=-1, keepdims=True) * (1.0 / n)
    return x * lax.rsqrt(ms + EPS) * gain


def _ffn_kernel(x_ref, g_ref, wgu_ref, wd_ref, o_ref, *, tf):
    d_ff = wd_ref.shape[0]
    x = x_ref[...]
    h = _rms(x, g_ref[...]).astype(BF16)
    acc = None
    for c in range(d_ff // tf):
        cols = slice(c * tf, (c + 1) * tf)
        gate = _dot(h, wgu_ref[:, cols])
        up = _dot(h, wgu_ref[:, d_ff + c * tf:d_ff + (c + 1) * tf])
        act = (gate * jax.nn.sigmoid(gate) * up).astype(BF16)
        part = _dot(act, wd_ref[cols, :])
        acc = part if acc is None else acc + part
    o_ref[...] = x + 0.5 * acc


def _ffn(x2, gain, w_gu, w_down, layer, *, tm, tf):
    t, d = x2.shape
    d_ff = w_down.shape[1]
    resident = pl.Buffered(1)
    return pl.pallas_call(
        functools.partial(_ffn_kernel, tf=tf),
        grid=(t // tm,),
        in_specs=[
            pl.BlockSpec((tm, d), lambda i: (i, 0)),
            pl.BlockSpec((1, d), lambda i: (0, 0)),
            pl.BlockSpec((None, d, 2 * d_ff), lambda i: (layer, 0, 0), pipeline_mode=resident),
            pl.BlockSpec((None, d_ff, d), lambda i: (layer, 0, 0), pipeline_mode=resident),
        ],
        out_specs=pl.BlockSpec((tm, d), lambda i: (i, 0)),
        out_shape=jax.ShapeDtypeStruct((t, d), F32),
        compiler_params=_params("parallel"),
        name="ffn",
    )(x2, gain.reshape(1, d), w_gu, w_down)


def _gelu_tanh(x):
    c = 0.7978845608028654
    return 0.5 * x * (1.0 + jnp.tanh(c * (x + 0.044715 * (x * x * x))))


def _even_prep_kernel(x_ref, g_ref, win_ref, lng_ref, lnb_ref, sw_ref, sb_ref,
                      qt_ref, k_ref, vt_ref, osg_ref):
    tm = x_ref.shape[1]
    kb = vt_ref.shape[3]
    w = SB_WIDTH
    h = _rms(x_ref[0], g_ref[...]).astype(BF16)
    qt_ref[0] = (_dot(h, win_ref[:, 0:w]) * (SB_HEAD_DIM ** -0.5 * LOG2E)).T.astype(BF16)
    k_ref[0] = _dot(h, win_ref[:, w:2 * w]).astype(BF16)
    v = _dot(h, win_ref[:, 2 * w:3 * w])
    for c in range(tm // kb):
        vt_ref[0, c] = v[c * kb:(c + 1) * kb, :].T.astype(BF16)
    osg_ref = osg_ref.at[0]
    u = _gelu_tanh(_dot(h, win_ref[:, 3 * w:3 * w + SG_WIDTH]))
    g = _gelu_tanh(_dot(h, win_ref[:, 3 * w + SG_WIDTH:3 * w + 2 * SG_WIDTH]))
    mu = jnp.mean(g, axis=-1, keepdims=True)
    gc = g - mu
    var = jnp.mean(gc * gc, axis=-1, keepdims=True)
    gn = (gc * lax.rsqrt(var + EPS) * lng_ref[...] + lnb_ref[...]).astype(BF16)

    row = lax.broadcasted_iota(jnp.int32, (SG_CHUNK, SG_CHUNK), 0)
    col = lax.broadcasted_iota(jnp.int32, (SG_CHUNK, SG_CHUNK), 1)
    tri = col <= row
    first_group = lax.broadcasted_iota(jnp.int32, (SG_CHUNK, LANES), 1) < SG_GROUP_DIM
    for p in range(SG_GROUPS // 2):
        lanes = slice(p * LANES, (p + 1) * LANES)
        w0 = jnp.where(tri, sw_ref[2 * p], 0.0).astype(BF16)
        w1 = jnp.where(tri, sw_ref[2 * p + 1], 0.0).astype(BF16)
        bias = sb_ref[:, lanes]
        for c in range(tm // SG_CHUNK):
            rows = slice(c * SG_CHUNK, (c + 1) * SG_CHUNK)
            gp = gn[rows, lanes]
            mixed = jnp.where(first_group, _dot(w0, gp), _dot(w1, gp)) + bias
            osg_ref[rows, lanes] = (u[rows, lanes] * mixed).astype(BF16)


def _even_prep(x, gain, w_in, ln_g, ln_b, sgu_w, sgu_bias_full, *, tm, kb):
    b, s, d = x.shape
    n_in = w_in.shape[1]
    w = SB_WIDTH
    const2 = lambda bi, i: (0, 0)
    row_out = jax.ShapeDtypeStruct((b, s, w), BF16)
    row_spec = pl.BlockSpec((1, tm, w), lambda bi, i: (bi, i, 0))
    return pl.pallas_call(
        _even_prep_kernel,
        grid=(b, s // tm),
        in_specs=[
            pl.BlockSpec((1, tm, d), lambda bi, i: (bi, i, 0)),
            pl.BlockSpec((1, d), const2),
            pl.BlockSpec((d, n_in), const2),
            pl.BlockSpec((1, SG_WIDTH), const2),
            pl.BlockSpec((1, SG_WIDTH), const2),
            pl.BlockSpec((SG_GROUPS, SG_CHUNK, SG_CHUNK), lambda bi, i: (0, 0, 0)),
            pl.BlockSpec((SG_CHUNK, SG_WIDTH), const2),
        ],
        out_specs=[pl.BlockSpec((1, w, tm), lambda bi, i: (bi, 0, i)),
                   row_spec,
                   pl.BlockSpec((1, tm // kb, w, kb), lambda bi, i: (bi, i, 0, 0)),
                   row_spec],
        out_shape=[jax.ShapeDtypeStruct((b, w, s), BF16), row_out,
                   jax.ShapeDtypeStruct((b, s // kb, w, kb), BF16), row_out],
        compiler_params=_params("parallel", "parallel"),
        name="even_prep",
    )(x, gain.reshape(1, d), w_in, ln_g.reshape(1, -1), ln_b.reshape(1, -1),
      sgu_w, sgu_bias_full)


def _sb_attn_kernel(qt_ref, k_ref, vt_ref, o_ref, acc_ref, r_ref, z0_ref, z1_ref, zc0_ref,
                    zc1_ref, t0_ref, t1_ref, bs_ref, kn_ref, *, tq, kb):
    i = pl.program_id(2)
    n_sub = tq // kb
    hd_dim = SB_HEAD_DIM
    z_refs, zc_refs, t_refs = (z0_ref, z1_ref), (zc0_ref, zc1_ref), (t0_ref, t1_ref)
    first_head = lax.broadcasted_iota(jnp.int32, (LANES, 1), 0) < hd_dim
    qt = qt_ref[0]
    zero = jnp.zeros_like(qt)
    qt_heads = (jnp.where(first_head, qt, zero), jnp.where(first_head, zero, qt))
    key = lax.broadcasted_iota(jnp.int32, (kb, tq), 0)
    query = lax.broadcasted_iota(jnp.int32, (kb, tq), 1)
    srow = lax.broadcasted_iota(jnp.int32, (kb, kb), 0)
    scol = lax.broadcasted_iota(jnp.int32, (kb, kb), 1)
    suffix = (scol >= srow).astype(BF16)

    def score(item):
        g, hd, slot = item
        start = pl.multiple_of(g * kb, kb)
        z_refs[slot][...] = _dot(k_ref[0, pl.ds(start, kb), :], qt_heads[hd])

    def stay(item, diag_sub):
        g, hd, slot = item
        z = z_refs[slot][...]
        sp = jnp.maximum(z, 0.0) + jnp.log2(1.0 + jnp.exp2(-jnp.abs(z)))
        if diag_sub is not None:
            visible = (key + diag_sub * kb) < query
            sp = jnp.where(visible, sp, 0.0)
            z = jnp.where(visible, z, MASKED)
        zc_refs[slot][...] = z
        t_refs[slot][...] = _dot(suffix, sp.astype(BF16))
        bs_ref[slot] = jnp.sum(sp, axis=0, keepdims=True)

    def weigh(item):
        g, hd, slot = item
        wgt = jnp.exp2(zc_refs[slot][...] - t_refs[slot][...] - r_ref[hd])
        vt = vt_ref[0, g, hd * hd_dim:(hd + 1) * hd_dim, :]
        acc_ref[hd] += _dot(vt, wgt.astype(BF16))
        r_ref[hd] += bs_ref[slot]

    def group(row, prev_item, next_first_block, diagonal):
        subs = [u for u in reversed(range(n_sub)) for _ in range(2)]
        items = [(row * n_sub + u, idx % 2, idx % 2) for idx, u in enumerate(subs)]
        for idx, item in enumerate(items):
            score(items[idx + 1] if idx + 1 < len(items) else (next_first_block, 0, 0))
            weigh(items[idx - 1] if idx > 0 else prev_item)
            stay(item, subs[idx] if diagonal else None)

    @pl.when(i == 0)
    def _():
        kf = k_ref[0].astype(F32)
        dim = lax.broadcasted_iota(jnp.int32, (LANES, LANES), 0)
        head = lax.broadcasted_iota(jnp.int32, (LANES, LANES), 1)
        select = ((dim < hd_dim) == (head == 0)) & (head < 2)
        kn_ref[...] = jnp.max(_dot((kf * kf).astype(BF16), select.astype(BF16)),
                              axis=0, keepdims=True)

    lane = lax.broadcasted_iota(jnp.int32, (1, LANES), 1)
    qf = qt.astype(F32)
    exit_level = []
    for hd in range(2):
        q_sq = jnp.sum(jnp.square(qf[hd * hd_dim:(hd + 1) * hd_dim, :]), axis=0, keepdims=True)
        k_sq = jnp.max(jnp.where(lane == hd, kn_ref[...], 0.0), axis=1, keepdims=True)
        exit_level.append(1.02 * jnp.sqrt(q_sq * k_sq) + UNDERFLOW_LOG2)

    acc_ref[...] = jnp.zeros_like(acc_ref)
    r_ref[...] = jnp.zeros_like(r_ref)
    zc1_ref[...] = jnp.full_like(zc1_ref, MASKED)
    t1_ref[...] = jnp.zeros_like(t1_ref)
    bs_ref[...] = jnp.zeros_like(bs_ref)
    last_sub = n_sub - 1
    score((i * n_sub + last_sub, 0, 0))
    group(i, (i * n_sub, 1, 1), jnp.maximum(i - 1, 0) * n_sub + last_sub, True)

    def more(carry):
        row, live = carry
        return (row >= 0) & (live > 0)

    def body(carry):
        row, _ = carry
        group(row, ((row + 1) * n_sub, 1, 1), jnp.maximum(row - 1, 0) * n_sub + last_sub, False)
        dead = ((jnp.min(r_ref[0] - exit_level[0]) > 0.0)
                & (jnp.min(r_ref[1] - exit_level[1]) > 0.0))
        return row - 1, jnp.where(dead, 0, 1).astype(jnp.int32)

    row_end, _ = lax.while_loop(more, body, (i - 1, jnp.int32(1)))
    weigh(((row_end + 1) * n_sub, 1, 1))
    out_t = jnp.concatenate([acc_ref[0], acc_ref[1]], axis=0)
    o_ref[0] = out_t.T.astype(BF16)


def _sb_attention(qt, k, vt, *, tq):
    b, w, s = qt.shape
    kb = vt.shape[3]
    return pl.pallas_call(
        functools.partial(_sb_attn_kernel, tq=tq, kb=kb),
        grid=(b, w // LANES, s // tq),
        in_specs=[
            pl.BlockSpec((1, LANES, tq), lambda bi, p, i: (bi, p, i)),
            pl.BlockSpec((1, s, LANES), lambda bi, p, i: (bi, 0, p)),
            pl.BlockSpec((1, s // kb, LANES, kb), lambda bi, p, i: (bi, 0, p, 0)),
        ],
        out_specs=pl.BlockSpec((1, tq, LANES), lambda bi, p, i: (bi, i, p)),
        out_shape=jax.ShapeDtypeStruct((b, s, w), BF16),
        scratch_shapes=[pltpu.VMEM((2, SB_HEAD_DIM, tq), F32), pltpu.VMEM((2, 1, tq), F32),
                        *[pltpu.VMEM((kb, tq), F32) for _ in range(6)],
                        pltpu.VMEM((2, 1, tq), F32), pltpu.VMEM((1, LANES), F32)],
        compiler_params=_params("parallel", "parallel", "arbitrary"),
        name="sb_attn",
    )(qt, k, vt)


def _lane_tile(t, width):
    return jnp.concatenate([t] * (width // t.shape[1]), axis=1)


def _rms_rows(xt, gain):
    ms = jnp.sum(xt * xt, axis=0, keepdims=True) * (1.0 / xt.shape[0])
    return xt * lax.rsqrt(ms + EPS) * _lane_tile(gain, xt.shape[1])


def _mla_prep_kernel(x_ref, pos_ref, g_ref, win_ref, qlg_ref, kvlg_ref, wuqt_ref, wukt_ref,
                     wuvt_ref, qg_ref, kg_ref, freq_ref, qt_ref, k_ref, vt_ref):
    tm = x_ref.shape[1]
    kb = vt_ref.shape[3]
    lat = MLA_Q_LORA + MLA_KV_LORA
    half = MLA_ROPE // 2
    h = _rms(x_ref[0], g_ref[...]).astype(BF16)
    ct = _dot(h, win_ref[...]).T
    cqn = _rms_rows(ct[0:MLA_Q_LORA], qlg_ref[...]).astype(BF16)
    ckvn = _rms_rows(ct[MLA_Q_LORA:lat], kvlg_ref[...]).astype(BF16)
    k_r = ct[lat:lat + MLA_ROPE]
    vt = _dot(wuvt_ref[...], ckvn)
    for c in range(tm // kb):
        vt_ref[0, c] = vt[:, c * kb:(c + 1) * kb].astype(BF16)
    q_all = _dot(wuqt_ref[...], cqn)
    kn_all = _dot(wukt_ref[...], ckvn)

    angle = _lane_tile(freq_ref[...], tm) * pos_ref[0].astype(F32)
    cos = jnp.cos(angle)
    sin = jnp.sin(angle)

    def rope(t):
        t1, t2 = t[0:half], t[half:]
        return jnp.concatenate([t1 * cos - t2 * sin, t1 * sin + t2 * cos], axis=0)

    qg = _lane_tile(qg_ref[...], tm)
    kg = _lane_tile(kg_ref[...], tm)
    zeros = jnp.zeros((LANES - MLA_QK, tm), F32)
    inv_n = 1.0 / MLA_QK
    kr_sq = jnp.sum(k_r * k_r, axis=0, keepdims=True)
    kr_roped = rope(k_r * kg[MLA_NOPE:])
    for hd in range(MLA_HEADS):
        qh = q_all[hd * MLA_QK:(hd + 1) * MLA_QK]
        r = lax.rsqrt(jnp.sum(qh * qh, axis=0, keepdims=True) * inv_n + EPS)
        qn = qh * r * qg
        qt_ref[0, hd] = jnp.concatenate([qn[0:MLA_NOPE], rope(qn[MLA_NOPE:]), zeros],
                                        axis=0).astype(BF16)
        kn = kn_all[hd * MLA_NOPE:(hd + 1) * MLA_NOPE]
        r = lax.rsqrt((jnp.sum(kn * kn, axis=0, keepdims=True) + kr_sq) * inv_n + EPS)
        kt = jnp.concatenate([kn * r * kg[0:MLA_NOPE], kr_roped * r, zeros], axis=0)
        k_ref[0, hd] = kt.T.astype(BF16)


def _mla_prep(x, positions, gain, w_in, qlg, kvlg, wuqt, wukt, wuvt, qg, kg, freq, *, tm, kb):
    b, s, d = x.shape
    vw = MLA_HEADS * MLA_V
    const2 = lambda bi, i: (0, 0)
    full = lambda a: pl.BlockSpec(a.shape, const2)
    small = [gain.reshape(1, -1), w_in, qlg, kvlg, wuqt, wukt, wuvt, qg, kg, freq]
    return pl.pallas_call(
        _mla_prep_kernel,
        grid=(b, s // tm),
        in_specs=[
            pl.BlockSpec((1, tm, d), lambda bi, i: (bi, i, 0)),
            pl.BlockSpec((1, 1, tm), lambda bi, i: (bi, 0, i)),
            *[full(a) for a in small],
        ],
        out_specs=[pl.BlockSpec((1, MLA_HEADS, LANES, tm), lambda bi, i: (bi, 0, 0, i)),
                   pl.BlockSpec((1, MLA_HEADS, tm, LANES), lambda bi, i: (bi, 0, i, 0)),
                   pl.BlockSpec((1, tm // kb, vw, kb), lambda bi, i: (bi, i, 0, 0))],
        out_shape=[jax.ShapeDtypeStruct((b, MLA_HEADS, LANES, s), BF16),
                   jax.ShapeDtypeStruct((b, MLA_HEADS, s, LANES), BF16),
                   jax.ShapeDtypeStruct((b, s // kb, vw, kb), BF16)],
        compiler_params=_params("parallel", "parallel"),
        name="mla_prep",
    )(x, positions.reshape(b, 1, s), *small)


def _mla_attn_kernel(qt_ref, k_ref, vt_ref, o_ref, acc_ref, m_ref, s0_ref, s1_ref, bm_ref,
                     *, tq):
    i = pl.program_id(2)
    n_heads = qt_ref.shape[1]
    s_refs = (s0_ref, s1_ref)
    key = lax.broadcasted_iota(jnp.int32, (tq, tq), 0)
    query = lax.broadcasted_iota(jnp.int32, (tq, tq), 1)
    causal = key <= query

    def produce(g, hd, masked):
        start = pl.multiple_of(g * tq, tq)
        sc = _dot(k_ref[0, hd, pl.ds(start, tq), :], qt_ref[0, hd])
        if masked:
            sc = jnp.where(causal, sc, MASKED)
        s_refs[hd % 2][...] = sc
        bm_ref[hd % 2] = jnp.max(sc, axis=0, keepdims=True)

    ones_rows = jnp.ones((SUM_ROWS, tq), BF16)

    def consume(g, hd):
        m_old = m_ref[hd]
        m_new = jnp.maximum(m_old, bm_ref[hd % 2])
        alpha = jnp.exp2(m_old - m_new)
        p = jnp.exp2(s_refs[hd % 2][...] - m_new)
        vt = jnp.concatenate([vt_ref[0, g, hd * MLA_V:(hd + 1) * MLA_V, :], ones_rows], axis=0)
        acc_ref[hd] = alpha * acc_ref[hd] + _dot(vt, p.astype(BF16))
        m_ref[hd] = m_new

    def visit(g, g_next, masked):
        for hd in range(n_heads):
            if hd + 1 < n_heads:
                produce(g, hd + 1, masked)
            else:
                produce(g_next, 0, False)
            consume(g, hd)

    acc_ref[...] = jnp.zeros_like(acc_ref)
    m_ref[...] = jnp.full_like(m_ref, MASKED)
    produce(i, 0, True)
    visit(i, 0, True)

    def body(it, carry):
        visit(it, jnp.minimum(it + 1, i - 1), False)
        return carry

    lax.fori_loop(0, i, body, 0)
    out_t = jnp.concatenate([acc_ref[hd, 0:MLA_V, :] / acc_ref[hd, MLA_V:MLA_V + 1, :]
                             for hd in range(n_heads)], axis=0)
    o_ref[0] = out_t.T.astype(BF16)


def _mla_attention(qt, k, vt, *, tq, heads_per_step):
    b, heads, _, s = qt.shape
    kb = vt.shape[3]
    nh = heads_per_step
    assert kb == tq and nh % 2 == 0
    vw = nh * MLA_V
    return pl.pallas_call(
        functools.partial(_mla_attn_kernel, tq=tq),
        grid=(b, heads // nh, s // tq),
        in_specs=[
            pl.BlockSpec((1, nh, LANES, tq), lambda bi, p, i: (bi, p, 0, i)),
            pl.BlockSpec((1, nh, s, LANES), lambda bi, p, i: (bi, p, 0, 0)),
            pl.BlockSpec((1, s // kb, vw, kb), lambda bi, p, i: (bi, 0, p, 0)),
        ],
        out_specs=pl.BlockSpec((1, tq, vw), lambda bi, p, i: (bi, i, p)),
        out_shape=jax.ShapeDtypeStruct((b, s, heads * MLA_V), BF16),
        scratch_shapes=[pltpu.VMEM((nh, MLA_V + SUM_ROWS, tq), F32),
                        pltpu.VMEM((nh, 1, tq), F32), pltpu.VMEM((kb, tq), F32),
                        pltpu.VMEM((kb, tq), F32), pltpu.VMEM((2, 1, tq), F32)],
        compiler_params=_params("parallel", "parallel", "arbitrary"),
        name="mla_attn",
    )(qt, k, vt)


def _mem_kv_kernel(mem_ref, g_ref, wkv_ref, kg_ref, k_ref, v_ref):
    hm = _rms(mem_ref[0], g_ref[...]).astype(BF16)
    hd_dim = kg_ref.shape[1]
    for hd in range(MEM_HEADS):
        kcols = slice(2 * hd * hd_dim, (2 * hd + 1) * hd_dim)
        vcols = slice((2 * hd + 1) * hd_dim, (2 * hd + 2) * hd_dim)
        out = slice(hd * hd_dim, (hd + 1) * hd_dim)
        k_ref[0, :, out] = _rms(_dot(hm, wkv_ref[:, kcols]), kg_ref[...]).astype(BF16)
        v_ref[0, :, out] = _dot(hm, wkv_ref[:, vcols]).astype(BF16)


def _mem_kv(mem, gain, wkv, k_gain):
    b, m, d = mem.shape
    out = jax.ShapeDtypeStruct((b, m, d), BF16)
    spec = pl.BlockSpec((1, m, d), lambda bi: (bi, 0, 0))
    return pl.pallas_call(
        _mem_kv_kernel,
        grid=(b,),
        in_specs=[spec,
                  pl.BlockSpec((1, d), lambda bi: (0, 0)),
                  pl.BlockSpec(wkv.shape, lambda bi: (0, 0)),
                  pl.BlockSpec((1, k_gain.shape[0]), lambda bi: (0, 0))],
        out_specs=[spec, spec],
        out_shape=[out, out],
        compiler_params=_params("parallel"),
        name="mem_kv",
    )(mem, gain.reshape(1, d), wkv, k_gain.reshape(1, -1))


def _mix_xattn_kernel(*refs, n_act):
    x_ref = refs[0]
    act_refs = refs[1:1 + n_act]
    w_ref, g_ref, wq_ref, qg_ref, k_ref, v_ref, wo_ref, o_ref = refs[1 + n_act:]
    act = jnp.concatenate([a_ref[0] for a_ref in act_refs], axis=-1)
    x1 = x_ref[0] + _dot(act, w_ref[...])
    h = _rms(x1, g_ref[...]).astype(BF16)
    hd_dim = qg_ref.shape[1]
    out = x1
    for hd in range(MEM_HEADS):
        cols = slice(hd * hd_dim, (hd + 1) * hd_dim)
        qn = (_rms(_dot(h, wq_ref[:, cols]), qg_ref[...]) * (hd_dim ** -0.5)).astype(BF16)
        sc = _dot_nt(qn, k_ref[0, :, cols])
        p = jnp.exp(sc - jnp.max(sc, axis=-1, keepdims=True))
        denom = jnp.sum(p, axis=-1, keepdims=True)
        o_h = (_dot(p.astype(BF16), v_ref[0, :, cols]) / denom).astype(BF16)
        out = out + _dot(o_h, wo_ref[cols, :])
    o_ref[0] = out


def _mix_xattn(x, acts, w_mix, gain, wq, q_gain, mem_k, mem_v, wo, *, tm):
    b, s, d = x.shape
    m = mem_k.shape[1]
    const2 = lambda bi, i: (0, 0)
    row = lambda width: pl.BlockSpec((1, tm, width), lambda bi, i: (bi, i, 0))
    mem_spec = pl.BlockSpec((1, m, d), lambda bi, i: (bi, 0, 0))
    return pl.pallas_call(
        functools.partial(_mix_xattn_kernel, n_act=len(acts)),
        grid=(b, s // tm),
        in_specs=[
            row(d),
            *[row(a.shape[-1]) for a in acts],
            pl.BlockSpec(w_mix.shape, const2),
            pl.BlockSpec((1, d), const2),
            pl.BlockSpec(wq.shape, const2),
            pl.BlockSpec((1, q_gain.shape[0]), const2),
            mem_spec, mem_spec,
            pl.BlockSpec(wo.shape, const2),
        ],
        out_specs=row(d),
        out_shape=jax.ShapeDtypeStruct((b, s, d), F32),
        compiler_params=_params("parallel", "parallel"),
        name="mix_xattn",
    )(x, *acts, w_mix, gain.reshape(1, d), wq, q_gain.reshape(1, -1), mem_k, mem_v, wo)


def _lane_bcast(vec):
    return jnp.broadcast_to(vec[:, None], (vec.shape[0], LANES))


def _mla_weights(w_in, w_uq, w_ukv, q_lora_gain, kv_lora_gain, q_gain, k_gain):
    lat = MLA_Q_LORA + MLA_KV_LORA
    w_in_ext = jnp.pad(w_in, ((0, 0), (0, lat + LANES - w_in.shape[1])))
    wukv = w_ukv.reshape(MLA_KV_LORA, MLA_HEADS, MLA_NOPE + MLA_V)
    wukt = wukv[..., :MLA_NOPE].reshape(MLA_KV_LORA, -1).T
    wuvt = wukv[..., MLA_NOPE:].reshape(MLA_KV_LORA, -1).T
    half = MLA_ROPE // 2
    inv_freq = ROPE_THETA ** (-jnp.arange(half, dtype=F32) / half)
    bf = lambda a: a.astype(BF16)
    return (bf(w_in_ext), _lane_bcast(q_lora_gain), _lane_bcast(kv_lora_gain),
            bf(w_uq.T), bf(wukt), bf(wuvt),
            _lane_bcast(q_gain * (MLA_QK ** -0.5 * LOG2E)), _lane_bcast(k_gain),
            _lane_bcast(inv_freq))


def _tile(n, pref):
    return pref if n % pref == 0 else n


def kernel(x, mem, positions, ffn_pre_norm, ffn_pre_w_gu, ffn_pre_w_down, mix_norm, sbg_w_in, sgu_ln_gain, sgu_ln_bias, sgu_w, sgu_b, sbg_w_out, mla_w_in, mla_q_lora_gain, mla_kv_lora_gain, mla_w_uq, mla_w_ukv, mla_q_gain, mla_k_gain, mla_w_out, xmem_norm, xmem_mem_norm, xmem_wq, xmem_wkv, xmem_q_gain, xmem_k_gain, xmem_wo, ffn_post_norm, ffn_post_w_gu, ffn_post_w_down):
    b, s, d = x.shape
    depth = ffn_pre_norm.shape[0]
    d_ff = ffn_pre_w_down.shape[1]
    t = b * s
    ffn_tm = _tile(t, 512)
    ffn_tf = _tile(d_ff, 256)
    pre_w_gu, pre_w_down = ffn_pre_w_gu.astype(BF16), ffn_pre_w_down.astype(BF16)
    post_w_gu, post_w_down = ffn_post_w_gu.astype(BF16), ffn_post_w_down.astype(BF16)
    row_tm = _tile(s, 512)
    mla_tq = _tile(s, 512)
    sb_tq = _tile(s, 256)
    sb_kb = sb_tq
    bf = lambda a: a.astype(BF16)

    for layer in range(depth):
        x = _ffn(x.reshape(t, d), ffn_pre_norm[layer], pre_w_gu, pre_w_down, layer,
                 tm=ffn_tm, tf=ffn_tf).reshape(b, s, d)
        if layer % 2 == 0:
            e = layer // 2
            bias_full = jnp.repeat(sgu_b[e].T, SG_GROUP_DIM, axis=1)
            qt, k, vt, o_sg = _even_prep(
                x, mix_norm[layer], bf(sbg_w_in[e]), sgu_ln_gain[e],
                sgu_ln_bias[e], sgu_w[e], bias_full, tm=row_tm, kb=sb_kb)
            acts = (_sb_attention(qt, k, vt, tq=sb_tq), o_sg)
            w_mix = bf(sbg_w_out[e])
        else:
            o = layer // 2
            mla_consts = _mla_weights(
                mla_w_in[o], mla_w_uq[o], mla_w_ukv[o], mla_q_lora_gain[o],
                mla_kv_lora_gain[o], mla_q_gain[o], mla_k_gain[o])
            qt, k, vt = _mla_prep(x, positions, mix_norm[layer], *mla_consts,
                                  tm=row_tm, kb=mla_tq)
            acts = (_mla_attention(qt, k, vt, tq=mla_tq, heads_per_step=4),)
            w_mix = bf(mla_w_out[o])
        mem_k, mem_v = _mem_kv(mem, xmem_mem_norm[layer], bf(xmem_wkv[layer]),
                               xmem_k_gain[layer])
        x = _mix_xattn(x, acts, w_mix, xmem_norm[layer], bf(xmem_wq[layer]), xmem_q_gain[layer],
                       mem_k, mem_v, bf(xmem_wo[layer]), tm=row_tm)
        x = _ffn(x.reshape(t, d), ffn_post_norm[layer], post_w_gu, post_w_down, layer,
                 tm=ffn_tm, tf=ffn_tf).reshape(b, s, d)
    return x
```

```python
import functools

import jax
import jax.numpy as jnp
from jax import lax
from jax.experimental import pallas as pl
from jax.experimental.pallas import tpu as pltpu

EPS = 1e-6
ROPE_THETA = 10000.0
LANES = 128
VMEM_LIMIT_BYTES = 56 * 1024 * 1024

SB_HEADS, SB_HEAD_DIM = 8, 64
SB_WIDTH = SB_HEADS * SB_HEAD_DIM
SG_GROUPS, SG_GROUP_DIM, SG_CHUNK = 8, 64, 128
SG_WIDTH = SG_GROUPS * SG_GROUP_DIM
MLA_HEADS, MLA_NOPE, MLA_ROPE, MLA_V = 16, 64, 32, 64
MLA_QK = MLA_NOPE + MLA_ROPE
MLA_Q_LORA, MLA_KV_LORA = 512, 256
MEM_HEADS = 4

BF16 = jnp.bfloat16
F32 = jnp.float32
LOG2E = 1.4426950408889634
MASKED = -1e30
SUM_ROWS = 16
UNDERFLOW_LOG2 = 160.0


def _params(*semantics):
    return pltpu.CompilerParams(dimension_semantics=semantics,
                                vmem_limit_bytes=VMEM_LIMIT_BYTES)


def _dot(a, b):
    return jnp.dot(a, b, preferred_element_type=F32)


def _dot_nt(a, b):
    return lax.dot_general(a, b, (((1,), (1,)), ((), ())), preferred_element_type=F32)


def _rms(x, gain, n=None):
    n = x.shape[-1] if n is None else n
    ms = jnp.sum(x * x, axis=-1, keepdims=True) * (1.0 / n)
    return x * lax.rsqrt(ms + EPS) * gain


def _ffn_kernel(x_ref, g_ref, wgu_ref, wd_ref, o_ref, *, tf):
    d_ff = wd_ref.shape[0]
    x = x_ref[...]
    h = _rms(x, g_ref[...]).astype(BF16)
    acc = None
    for c in range(d_ff // tf):
        cols = slice(c * tf, (c + 1) * tf)
        gate = _dot(h, wgu_ref[:, cols])
        up = _dot(h, wgu_ref[:, d_ff + c * tf:d_ff + (c + 1) * tf])
        act = (gate * jax.nn.sigmoid(gate) * up).astype(BF16)
        part = _dot(act, wd_ref[cols, :])
        acc = part if acc is None else acc + part
    o_ref[...] = x + 0.5 * acc


def _ffn(x2, gain, w_gu, w_down, layer, *, tm, tf):
    t, d = x2.shape
    d_ff = w_down.shape[1]
    resident = pl.Buffered(1)
    return pl.pallas_call(
        functools.partial(_ffn_kernel, tf=tf),
        grid=(t // tm,),
        in_specs=[
            pl.BlockSpec((tm, d), lambda i: (i, 0)),
            pl.BlockSpec((1, d), lambda i: (0, 0)),
            pl.BlockSpec((None, d, 2 * d_ff), lambda i: (layer, 0, 0), pipeline_mode=resident),
            pl.BlockSpec((None, d_ff, d), lambda i: (layer, 0, 0), pipeline_mode=resident),
        ],
        out_specs=pl.BlockSpec((tm, d), lambda i: (i, 0)),
        out_shape=jax.ShapeDtypeStruct((t, d), F32),
        compiler_params=_params("parallel"),
        name="ffn",
    )(x2, gain.reshape(1, d), w_gu, w_down)


def _gelu_tanh(x):
    c = 0.7978845608028654
    return 0.5 * x * (1.0 + jnp.tanh(c * (x + 0.044715 * (x * x * x))))


def _even_prep_kernel(x_ref, g_ref, win_ref, lng_ref, lnb_ref, sw_ref, sb_ref,
                      qt_ref, k_ref, vt_ref, osg_ref):
    tm = x_ref.shape[1]
    kb = vt_ref.shape[3]
    w = SB_WIDTH
    h = _rms(x_ref[0], g_ref[...]).astype(BF16)
    qt_ref[0] = (_dot(h, win_ref[:, 0:w]) * (SB_HEAD_DIM ** -0.5 * LOG2E)).T.astype(BF16)
    k_ref[0] = _dot(h, win_ref[:, w:2 * w]).astype(BF16)
    v = _dot(h, win_ref[:, 2 * w:3 * w])
    for c in range(tm // kb):
        vt_ref[0, c] = v[c * kb:(c + 1) * kb, :].T.astype(BF16)
    osg_ref = osg_ref.at[0]
    u = _gelu_tanh(_dot(h, win_ref[:, 3 * w:3 * w + SG_WIDTH]))
    g = _gelu_tanh(_dot(h, win_ref[:, 3 * w + SG_WIDTH:3 * w + 2 * SG_WIDTH]))
    mu = jnp.mean(g, axis=-1, keepdims=True)
    gc = g - mu
    var = jnp.mean(gc * gc, axis=-1, keepdims=True)
    gn = (gc * lax.rsqrt(var + EPS) * lng_ref[...] + lnb_ref[...]).astype(BF16)

    row = lax.broadcasted_iota(jnp.int32, (SG_CHUNK, SG_CHUNK), 0)
    col = lax.broadcasted_iota(jnp.int32, (SG_CHUNK, SG_CHUNK), 1)
    tri = col <= row
    first_group = lax.broadcasted_iota(jnp.int32, (SG_CHUNK, LANES), 1) < SG_GROUP_DIM
    for p in range(SG_GROUPS // 2):
        lanes = slice(p * LANES, (p + 1) * LANES)
        w0 = jnp.where(tri, sw_ref[2 * p], 0.0).astype(BF16)
        w1 = jnp.where(tri, sw_ref[2 * p + 1], 0.0).astype(BF16)
        bias = sb_ref[:, lanes]
        for c in range(tm // SG_CHUNK):
            rows = slice(c * SG_CHUNK, (c + 1) * SG_CHUNK)
            gp = gn[rows, lanes]
            mixed = jnp.where(first_group, _dot(w0, gp), _dot(w1, gp)) + bias
            osg_ref[rows, lanes] = (u[rows, lanes] * mixed).astype(BF16)


def _even_prep(x, gain, w_in, ln_g, ln_b, sgu_w, sgu_bias_full, *, tm, kb):
    b, s, d = x.shape
    n_in = w_in.shape[1]
    w = SB_WIDTH
    const2 = lambda bi, i: (0, 0)
    row_out = jax.ShapeDtypeStruct((b, s, w), BF16)
    row_spec = pl.BlockSpec((1, tm, w), lambda bi, i: (bi, i, 0))
    return pl.pallas_call(
        _even_prep_kernel,
        grid=(b, s // tm),
        in_specs=[
            pl.BlockSpec((1, tm, d), lambda bi, i: (bi, i, 0)),
            pl.BlockSpec((1, d), const2),
            pl.BlockSpec((d, n_in), const2),
            pl.BlockSpec((1, SG_WIDTH), const2),
            pl.BlockSpec((1, SG_WIDTH), const2),
            pl.BlockSpec((SG_GROUPS, SG_CHUNK, SG_CHUNK), lambda bi, i: (0, 0, 0)),
            pl.BlockSpec((SG_CHUNK, SG_WIDTH), const2),
        ],
        out_specs=[pl.BlockSpec((1, w, tm), lambda bi, i: (bi, 0, i)),
                   row_spec,
                   pl.BlockSpec((1, tm // kb, w, kb), lambda bi, i: (bi, i, 0, 0)),
                   row_spec],
        out_shape=[jax.ShapeDtypeStruct((b, w, s), BF16), row_out,
                   jax.ShapeDtypeStruct((b, s // kb, w, kb), BF16), row_out],
        compiler_params=_params("parallel", "parallel"),
        name="even_prep",
    )(x, gain.reshape(1, d), w_in, ln_g.reshape(1, -1), ln_b.reshape(1, -1),
      sgu_w, sgu_bias_full)


def _sb_attn_kernel(qt_ref, k_ref, vt_ref, o_ref, acc_ref, r_ref, z0_ref, z1_ref, zc0_ref,
                    zc1_ref, t0_ref, t1_ref, bs_ref, kn_ref, *, tq, kb):
    i = pl.program_id(2)
    n_sub = tq // kb
    hd_dim = SB_HEAD_DIM
    z_refs, zc_refs, t_refs = (z0_ref, z1_ref), (zc0_ref, zc1_ref), (t0_ref, t1_ref)
    first_head = lax.broadcasted_iota(jnp.int32, (LANES, 1), 0) < hd_dim
    qt = qt_ref[0]
    zero = jnp.zeros_like(qt)
    qt_heads = (jnp.where(first_head, qt, zero), jnp.where(first_head, zero, qt))
    key = lax.broadcasted_iota(jnp.int32, (kb, tq), 0)
    query = lax.broadcasted_iota(jnp.int32, (kb, tq), 1)
    srow = lax.broadcasted_iota(jnp.int32, (kb, kb), 0)
    scol = lax.broadcasted_iota(jnp.int32, (kb, kb), 1)
    suffix = (scol >= srow).astype(BF16)

    def score(item):
        g, hd, slot = item
        start = pl.multiple_of(g * kb, kb)
        z_refs[slot][...] = _dot(k_ref[0, pl.ds(start, kb), :], qt_heads[hd])

    def stay(item, diag_sub):
        g, hd, slot = item
        z = z_refs[slot][...]
        sp = jnp.maximum(z, 0.0) + jnp.log2(1.0 + jnp.exp2(-jnp.abs(z)))
        if diag_sub is not None:
            visible = (key + diag_sub * kb) < query
            sp = jnp.where(visible, sp, 0.0)
            z = jnp.where(visible, z, MASKED)
        zc_refs[slot][...] = z
        t_refs[slot][...] = _dot(suffix, sp.astype(BF16))
        bs_ref[slot] = jnp.sum(sp, axis=0, keepdims=True)

    def weigh(item):
        g, hd, slot = item
        wgt = jnp.exp2(zc_refs[slot][...] - t_refs[slot][...] - r_ref[hd])
        vt = vt_ref[0, g, hd * hd_dim:(hd + 1) * hd_dim, :]
        acc_ref[hd] += _dot(vt, wgt.astype(BF16))
        r_ref[hd] += bs_ref[slot]

    def group(row, prev_item, next_first_block, diagonal):
        subs = [u for u in reversed(range(n_sub)) for _ in range(2)]
        items = [(row * n_sub + u, idx % 2, idx % 2) for idx, u in enumerate(subs)]
        for idx, item in enumerate(items):
            score(items[idx + 1] if idx + 1 < len(items) else (next_first_block, 0, 0))
            weigh(items[idx - 1] if idx > 0 else prev_item)
            stay(item, subs[idx] if diagonal else None)

    @pl.when(i == 0)
    def _():
        kf = k_ref[0].astype(F32)
        dim = lax.broadcasted_iota(jnp.int32, (LANES, LANES), 0)
        head = lax.broadcasted_iota(jnp.int32, (LANES, LANES), 1)
        select = ((dim < hd_dim) == (head == 0)) & (head < 2)
        kn_ref[...] = jnp.max(_dot((kf * kf).astype(BF16), select.astype(BF16)),
                              axis=0, keepdims=True)

    lane = lax.broadcasted_iota(jnp.int32, (1, LANES), 1)
    qf = qt.astype(F32)
    exit_level = []
    for hd in range(2):
        q_sq = jnp.sum(jnp.square(qf[hd * hd_dim:(hd + 1) * hd_dim, :]), axis=0, keepdims=True)
        k_sq = jnp.max(jnp.where(lane == hd, kn_ref[...], 0.0), axis=1, keepdims=True)
        exit_level.append(1.02 * jnp.sqrt(q_sq * k_sq) + UNDERFLOW_LOG2)

    acc_ref[...] = jnp.zeros_like(acc_ref)
    r_ref[...] = jnp.zeros_like(r_ref)
    zc1_ref[...] = jnp.full_like(zc1_ref, MASKED)
    t1_ref[...] = jnp.zeros_like(t1_ref)
    bs_ref[...] = jnp.zeros_like(bs_ref)
    last_sub = n_sub - 1
    score((i * n_sub + last_sub, 0, 0))
    group(i, (i * n_sub, 1, 1), jnp.maximum(i - 1, 0) * n_sub + last_sub, True)

    def more(carry):
        row, live = carry
        return (row >= 0) & (live > 0)

    def body(carry):
        row, _ = carry
        group(row, ((row + 1) * n_sub, 1, 1), jnp.maximum(row - 1, 0) * n_sub + last_sub, False)
        dead = ((jnp.min(r_ref[0] - exit_level[0]) > 0.0)
                & (jnp.min(r_ref[1] - exit_level[1]) > 0.0))
        return row - 1, jnp.where(dead, 0, 1).astype(jnp.int32)

    row_end, _ = lax.while_loop(more, body, (i - 1, jnp.int32(1)))
    weigh(((row_end + 1) * n_sub, 1, 1))
    out_t = jnp.concatenate([acc_ref[0], acc_ref[1]], axis=0)
    o_ref[0] = out_t.T.astype(BF16)


def _sb_attention(qt, k, vt, *, tq):
    b, w, s = qt.shape
    kb = vt.shape[3]
    return pl.pallas_call(
        functools.partial(_sb_attn_kernel, tq=tq, kb=kb),
        grid=(b, w // LANES, s // tq),
        in_specs=[
            pl.BlockSpec((1, LANES, tq), lambda bi, p, i: (bi, p, i)),
            pl.BlockSpec((1, s, LANES), lambda bi, p, i: (bi, 0, p)),
            pl.BlockSpec((1, s // kb, LANES, kb), lambda bi, p, i: (bi, 0, p, 0)),
        ],
        out_specs=pl.BlockSpec((1, tq, LANES), lambda bi, p, i: (bi, i, p)),
        out_shape=jax.ShapeDtypeStruct((b, s, w), BF16),
        scratch_shapes=[pltpu.VMEM((2, SB_HEAD_DIM, tq), F32), pltpu.VMEM((2, 1, tq), F32),
                        *[pltpu.VMEM((kb, tq), F32) for _ in range(6)],
                        pltpu.VMEM((2, 1, tq), F32), pltpu.VMEM((1, LANES), F32)],
        compiler_params=_params("parallel", "parallel", "arbitrary"),
        name="sb_attn",
    )(qt, k, vt)


def _lane_tile(t, width):
    return jnp.concatenate([t] * (width // t.shape[1]), axis=1)


def _rms_rows(xt, gain):
    ms = jnp.sum(xt * xt, axis=0, keepdims=True) * (1.0 / xt.shape[0])
    return xt * lax.rsqrt(ms + EPS) * _lane_tile(gain, xt.shape[1])


def _mla_prep_kernel(x_ref, pos_ref, g_ref, win_ref, qlg_ref, kvlg_ref, wuqt_ref, wukt_ref,
                     wuvt_ref, qg_ref, kg_ref, freq_ref, qt_ref, k_ref, vt_ref):
    tm = x_ref.shape[1]
    kb = vt_ref.shape[3]
    lat = MLA_Q_LORA + MLA_KV_LORA
    half = MLA_ROPE // 2
    h = _rms(x_ref[0], g_ref[...]).astype(BF16)
    ct = _dot(h, win_ref[...]).T
    cqn = _rms_rows(ct[0:MLA_Q_LORA], qlg_ref[...]).astype(BF16)
    ckvn = _rms_rows(ct[MLA_Q_LORA:lat], kvlg_ref[...]).astype(BF16)
    k_r = ct[lat:lat + MLA_ROPE]
    vt = _dot(wuvt_ref[...], ckvn)
    for c in range(tm // kb):
        vt_ref[0, c] = vt[:, c * kb:(c + 1) * kb].astype(BF16)
    q_all = _dot(wuqt_ref[...], cqn)
    kn_all = _dot(wukt_ref[...], ckvn)

    angle = _lane_tile(freq_ref[...], tm) * pos_ref[0].astype(F32)
    cos = jnp.cos(angle)
    sin = jnp.sin(angle)

    def rope(t):
        t1, t2 = t[0:half], t[half:]
        return jnp.concatenate([t1 * cos - t2 * sin, t1 * sin + t2 * cos], axis=0)

    qg = _lane_tile(qg_ref[...], tm)
    kg = _lane_tile(kg_ref[...], tm)
    zeros = jnp.zeros((LANES - MLA_QK, tm), F32)
    inv_n = 1.0 / MLA_QK
    kr_sq = jnp.sum(k_r * k_r, axis=0, keepdims=True)
    kr_roped = rope(k_r * kg[MLA_NOPE:])
    for hd in range(MLA_HEADS):
        qh = q_all[hd * MLA_QK:(hd + 1) * MLA_QK]
        r = lax.rsqrt(jnp.sum(qh * qh, axis=0, keepdims=True) * inv_n + EPS)
        qn = qh * r * qg
        qt_ref[0, hd] = jnp.concatenate([qn[0:MLA_NOPE], rope(qn[MLA_NOPE:]), zeros],
                                        axis=0).astype(BF16)
        kn = kn_all[hd * MLA_NOPE:(hd + 1) * MLA_NOPE]
        r = lax.rsqrt((jnp.sum(kn * kn, axis=0, keepdims=True) + kr_sq) * inv_n + EPS)
        kt = jnp.concatenate([kn * r * kg[0:MLA_NOPE], kr_roped * r, zeros], axis=0)
        k_ref[0, hd] = kt.T.astype(BF16)


def _mla_prep(x, positions, gain, w_in, qlg, kvlg, wuqt, wukt, wuvt, qg, kg, freq, *, tm, kb):
    b, s, d = x.shape
    vw = MLA_HEADS * MLA_V
    const2 = lambda bi, i: (0, 0)
    full = lambda a: pl.BlockSpec(a.shape, const2)
    small = [gain.reshape(1, -1), w_in, qlg, kvlg, wuqt, wukt, wuvt, qg, kg, freq]
    return pl.pallas_call(
        _mla_prep_kernel,
        grid=(b, s // tm),
        in_specs=[
            pl.BlockSpec((1, tm, d), lambda bi, i: (bi, i, 0)),
            pl.BlockSpec((1, 1, tm), lambda bi, i: (bi, 0, i)),
            *[full(a) for a in small],
        ],
        out_specs=[pl.BlockSpec((1, MLA_HEADS, LANES, tm), lambda bi, i: (bi, 0, 0, i)),
                   pl.BlockSpec((1, MLA_HEADS, tm, LANES), lambda bi, i: (bi, 0, i, 0)),
                   pl.BlockSpec((1, tm // kb, vw, kb), lambda bi, i: (bi, i, 0, 0))],
        out_shape=[jax.ShapeDtypeStruct((b, MLA_HEADS, LANES, s), BF16),
                   jax.ShapeDtypeStruct((b, MLA_HEADS, s, LANES), BF16),
                   jax.ShapeDtypeStruct((b, s // kb, vw, kb), BF16)],
        compiler_params=_params("parallel", "parallel"),
        name="mla_prep",
    )(x, positions.reshape(b, 1, s), *small)


def _mla_attn_kernel(qt_ref, k_ref, vt_ref, o_ref, acc_ref, m_ref, s0_ref, s1_ref, bm_ref,
                     *, tq):
    i = pl.program_id(2)
    n_heads = qt_ref.shape[1]
    s_refs = (s0_ref, s1_ref)
    key = lax.broadcasted_iota(jnp.int32, (tq, tq), 0)
    query = lax.broadcasted_iota(jnp.int32, (tq, tq), 1)
    causal = key <= query

    def produce(g, hd, masked):
        start = pl.multiple_of(g * tq, tq)
        sc = _dot(k_ref[0, hd, pl.ds(start, tq), :], qt_ref[0, hd])
        if masked:
            sc = jnp.where(causal, sc, MASKED)
        s_refs[hd % 2][...] = sc
        bm_ref[hd % 2] = jnp.max(sc, axis=0, keepdims=True)

    ones_rows = jnp.ones((SUM_ROWS, tq), BF16)

    def consume(g, hd):
        m_old = m_ref[hd]
        m_new = jnp.maximum(m_old, bm_ref[hd % 2])
        alpha = jnp.exp2(m_old - m_new)
        p = jnp.exp2(s_refs[hd % 2][...] - m_new)
        vt = jnp.concatenate([vt_ref[0, g, hd * MLA_V:(hd + 1) * MLA_V, :], ones_rows], axis=0)
        acc_ref[hd] = alpha * acc_ref[hd] + _dot(vt, p.astype(BF16))
        m_ref[hd] = m_new

    def visit(g, g_next, masked):
        for hd in range(n_heads):
            if hd + 1 < n_heads:
                produce(g, hd + 1, masked)
            else:
                produce(g_next, 0, False)
            consume(g, hd)

    acc_ref[...] = jnp.zeros_like(acc_ref)
    m_ref[...] = jnp.full_like(m_ref, MASKED)
    produce(i, 0, True)
    visit(i, 0, True)

    def body(it, carry):
        visit(it, jnp.minimum(it + 1, i - 1), False)
        return carry

    lax.fori_loop(0, i, body, 0)
    out_t = jnp.concatenate([acc_ref[hd, 0:MLA_V, :] / acc_ref[hd, MLA_V:MLA_V + 1, :]
                             for hd in range(n_heads)], axis=0)
    o_ref[0] = out_t.T.astype(BF16)


def _mla_attention(qt, k, vt, *, tq, heads_per_step):
    b, heads, _, s = qt.shape
    kb = vt.shape[3]
    nh = heads_per_step
    assert kb == tq and nh % 2 == 0
    vw = nh * MLA_V
    return pl.pallas_call(
        functools.partial(_mla_attn_kernel, tq=tq),
        grid=(b, heads // nh, s // tq),
        in_specs=[
            pl.BlockSpec((1, nh, LANES, tq), lambda bi, p, i: (bi, p, 0, i)),
            pl.BlockSpec((1, nh, s, LANES), lambda bi, p, i: (bi, p, 0, 0)),
            pl.BlockSpec((1, s // kb, vw, kb), lambda bi, p, i: (bi, 0, p, 0)),
        ],
        out_specs=pl.BlockSpec((1, tq, vw), lambda bi, p, i: (bi, i, p)),
        out_shape=jax.ShapeDtypeStruct((b, s, heads * MLA_V), BF16),
        scratch_shapes=[pltpu.VMEM((nh, MLA_V + SUM_ROWS, tq), F32),
                        pltpu.VMEM((nh, 1, tq), F32), pltpu.VMEM((kb, tq), F32),
                        pltpu.VMEM((kb, tq), F32), pltpu.VMEM((2, 1, tq), F32)],
        compiler_params=_params("parallel", "parallel", "arbitrary"),
        name="mla_attn",
    )(qt, k, vt)


def _mem_kv_kernel(mem_ref, g_ref, wkv_ref, kg_ref, k_ref, v_ref):
    hm = _rms(mem_ref[0], g_ref[...]).astype(BF16)
    hd_dim = kg_ref.shape[1]
    for hd in range(MEM_HEADS):
        kcols = slice(2 * hd * hd_dim, (2 * hd + 1) * hd_dim)
        vcols = slice((2 * hd + 1) * hd_dim, (2 * hd + 2) * hd_dim)
        out = slice(hd * hd_dim, (hd + 1) * hd_dim)
        k_ref[0, :, out] = _rms(_dot(hm, wkv_ref[:, kcols]), kg_ref[...]).astype(BF16)
        v_ref[0, :, out] = _dot(hm, wkv_ref[:, vcols]).astype(BF16)


def _mem_kv(mem, gain, wkv, k_gain):
    b, m, d = mem.shape
    out = jax.ShapeDtypeStruct((b, m, d), BF16)
    spec = pl.BlockSpec((1, m, d), lambda bi: (bi, 0, 0))
    return pl.pallas_call(
        _mem_kv_kernel,
        grid=(b,),
        in_specs=[spec,
                  pl.BlockSpec((1, d), lambda bi: (0, 0)),
                  pl.BlockSpec(wkv.shape, lambda bi: (0, 0)),
                  pl.BlockSpec((1, k_gain.shape[0]), lambda bi: (0, 0))],
        out_specs=[spec, spec],
        out_shape=[out, out],
        compiler_params=_params("parallel"),
        name="mem_kv",
    )(mem, gain.reshape(1, d), wkv, k_gain.reshape(1, -1))


def _mix_xattn_kernel(*refs, n_act):
    x_ref = refs[0]
    act_refs = refs[1:1 + n_act]
    w_ref, g_ref, wq_ref, qg_ref, k_ref, v_ref, wo_ref, o_ref = refs[1 + n_act:]
    act = jnp.concatenate([a_ref[0] for a_ref in act_refs], axis=-1)
    x1 = x_ref[0] + _dot(act, w_ref[...])
    h = _rms(x1, g_ref[...]).astype(BF16)
    hd_dim = qg_ref.shape[1]
    cols = [slice(hd * hd_dim, (hd + 1) * hd_dim) for hd in range(MEM_HEADS)]
    q = [_dot(h, wq_ref[:, c]) for c in cols]
    qn = [(_rms(q_h, qg_ref[...]) * (hd_dim ** -0.5)).astype(BF16) for q_h in q]
    sc = [_dot_nt(qn_h, k_ref[0, :, c]) for qn_h, c in zip(qn, cols)]
    p = [jnp.exp(s_h - jnp.max(s_h, axis=-1, keepdims=True)) for s_h in sc]
    o = [(_dot(p_h.astype(BF16), v_ref[0, :, c]) / jnp.sum(p_h, axis=-1, keepdims=True)
          ).astype(BF16) for p_h, c in zip(p, cols)]
    o_ref[0] = x1 + _dot(jnp.concatenate(o, axis=-1), wo_ref[...])


def _mix_xattn(x, acts, w_mix, gain, wq, q_gain, mem_k, mem_v, wo, *, tm):
    b, s, d = x.shape
    m = mem_k.shape[1]
    const2 = lambda bi, i: (0, 0)
    row = lambda width: pl.BlockSpec((1, tm, width), lambda bi, i: (bi, i, 0))
    mem_spec = pl.BlockSpec((1, m, d), lambda bi, i: (bi, 0, 0))
    return pl.pallas_call(
        functools.partial(_mix_xattn_kernel, n_act=len(acts)),
        grid=(b, s // tm),
        in_specs=[
            row(d),
            *[row(a.shape[-1]) for a in acts],
            pl.BlockSpec(w_mix.shape, const2),
            pl.BlockSpec((1, d), const2),
            pl.BlockSpec(wq.shape, const2),
            pl.BlockSpec((1, q_gain.shape[0]), const2),
            mem_spec, mem_spec,
            pl.BlockSpec(wo.shape, const2),
        ],
        out_specs=row(d),
        out_shape=jax.ShapeDtypeStruct((b, s, d), F32),
        compiler_params=_params("parallel", "parallel"),
        name="mix_xattn",
    )(x, *acts, w_mix, gain.reshape(1, d), wq, q_gain.reshape(1, -1), mem_k, mem_v, wo)


def _lane_bcast(vec):
    return jnp.broadcast_to(vec[:, None], (vec.shape[0], LANES))


def _mla_weights(w_in, w_uq, w_ukv, q_lora_gain, kv_lora_gain, q_gain, k_gain):
    lat = MLA_Q_LORA + MLA_KV_LORA
    w_in_ext = jnp.pad(w_in, ((0, 0), (0, lat + LANES - w_in.shape[1])))
    wukv = w_ukv.reshape(MLA_KV_LORA, MLA_HEADS, MLA_NOPE + MLA_V)
    wukt = wukv[..., :MLA_NOPE].reshape(MLA_KV_LORA, -1).T
    wuvt = wukv[..., MLA_NOPE:].reshape(MLA_KV_LORA, -1).T
    half = MLA_ROPE // 2
    inv_freq = ROPE_THETA ** (-jnp.arange(half, dtype=F32) / half)
    bf = lambda a: a.astype(BF16)
    return (bf(w_in_ext), _lane_bcast(q_lora_gain), _lane_bcast(kv_lora_gain),
            bf(w_uq.T), bf(wukt), bf(wuvt),
            _lane_bcast(q_gain * (MLA_QK ** -0.5 * LOG2E)), _lane_bcast(k_gain),
            _lane_bcast(inv_freq))


def _tile(n, pref):
    return pref if n % pref == 0 else n


def kernel(x, mem, positions, ffn_pre_norm, ffn_pre_w_gu, ffn_pre_w_down, mix_norm, sbg_w_in, sgu_ln_gain, sgu_ln_bias, sgu_w, sgu_b, sbg_w_out, mla_w_in, mla_q_lora_gain, mla_kv_lora_gain, mla_w_uq, mla_w_ukv, mla_q_gain, mla_k_gain, mla_w_out, xmem_norm, xmem_mem_norm, xmem_wq, xmem_wkv, xmem_q_gain, xmem_k_gain, xmem_wo, ffn_post_norm, ffn_post_w_gu, ffn_post_w_down):
    b, s, d = x.shape
    depth = ffn_pre_norm.shape[0]
    d_ff = ffn_pre_w_down.shape[1]
    t = b * s
    ffn_tm = _tile(t, 512)
    ffn_tf = _tile(d_ff, 256)
    pre_w_gu, pre_w_down = ffn_pre_w_gu.astype(BF16), ffn_pre_w_down.astype(BF16)
    post_w_gu, post_w_down = ffn_post_w_gu.astype(BF16), ffn_post_w_down.astype(BF16)
    row_tm = _tile(s, 512)
    mla_tq = _tile(s, 512)
    sb_tq = _tile(s, 512)
    sb_kb = _tile(sb_tq, 256)
    bf = lambda a: a.astype(BF16)

    for layer in range(depth):
        x = _ffn(x.reshape(t, d), ffn_pre_norm[layer], pre_w_gu, pre_w_down, layer,
                 tm=ffn_tm, tf=ffn_tf).reshape(b, s, d)
        if layer % 2 == 0:
            e = layer // 2
            bias_full = jnp.repeat(sgu_b[e].T, SG_GROUP_DIM, axis=1)
            qt, k, vt, o_sg = _even_prep(
                x, mix_norm[layer], bf(sbg_w_in[e]), sgu_ln_gain[e],
                sgu_ln_bias[e], sgu_w[e], bias_full, tm=row_tm, kb=sb_kb)
            acts = (_sb_attention(qt, k, vt, tq=sb_tq), o_sg)
            w_mix = bf(sbg_w_out[e])
        else:
            o = layer // 2
            mla_consts = _mla_weights(
                mla_w_in[o], mla_w_uq[o], mla_w_ukv[o], mla_q_lora_gain[o],
                mla_kv_lora_gain[o], mla_q_gain[o], mla_k_gain[o])
            qt, k, vt = _mla_prep(x, positions, mix_norm[layer], *mla_consts,
                                  tm=row_tm, kb=mla_tq)
            acts = (_mla_attention(qt, k, vt, tq=mla_tq, heads_per_step=4),)
            w_mix = bf(mla_w_out[o])
        mem_k, mem_v = _mem_kv(mem, xmem_mem_norm[layer], bf(xmem_wkv[layer]),
                               xmem_k_gain[layer])
        x = _mix_xattn(x, acts, w_mix, xmem_norm[layer], bf(xmem_wq[layer]), xmem_q_gain[layer],
                       mem_k, mem_v, bf(xmem_wo[layer]), tm=row_tm)
        x = _ffn(x.reshape(t, d), ffn_post_norm[layer], post_w_gu, post_w_down, layer,
                 tm=ffn_tm, tf=ffn_tf).reshape(b, s, d)
    return x
```

```python
import functools

import jax
import jax.numpy as jnp
from jax import lax
from jax.experimental import pallas as pl
from jax.experimental.pallas import tpu as pltpu

EPS = 1e-6
ROPE_THETA = 10000.0
LANES = 128
VMEM_LIMIT_BYTES = 56 * 1024 * 1024

SB_HEADS, SB_HEAD_DIM = 8, 64
SB_WIDTH = SB_HEADS * SB_HEAD_DIM
SG_GROUPS, SG_GROUP_DIM, SG_CHUNK = 8, 64, 128
SG_WIDTH = SG_GROUPS * SG_GROUP_DIM
MLA_HEADS, MLA_NOPE, MLA_ROPE, MLA_V = 16, 64, 32, 64
MLA_QK = MLA_NOPE + MLA_ROPE
MLA_Q_LORA, MLA_KV_LORA = 512, 256
MEM_HEADS = 4

BF16 = jnp.bfloat16
F32 = jnp.float32
LOG2E = 1.4426950408889634
MASKED = -1e30
SUM_ROWS = 16
UNDERFLOW_LOG2 = 160.0


def _params(*semantics):
    return pltpu.CompilerParams(dimension_semantics=semantics,
                                vmem_limit_bytes=VMEM_LIMIT_BYTES)


def _dot(a, b):
    return jnp.dot(a, b, preferred_element_type=F32)


def _dot_nt(a, b):
    return lax.dot_general(a, b, (((1,), (1,)), ((), ())), preferred_element_type=F32)


def _rms(x, gain, n=None):
    n = x.shape[-1] if n is None else n
    ms = jnp.sum(x * x, axis=-1, keepdims=True) * (1.0 / n)
    return x * lax.rsqrt(ms + EPS) * gain


def _ffn_kernel(x_ref, g_ref, wgu_ref, wd_ref, o_ref, *, tf):
    d_ff = wd_ref.shape[0]
    x = x_ref[...]
    h = _rms(x, g_ref[...]).astype(BF16)
    acc = None
    for c in range(d_ff // tf):
        cols = slice(c * tf, (c + 1) * tf)
        gate = _dot(h, wgu_ref[:, cols])
        up = _dot(h, wgu_ref[:, d_ff + c * tf:d_ff + (c + 1) * tf])
        act = (gate * jax.nn.sigmoid(gate) * up).astype(BF16)
        part = _dot(act, wd_ref[cols, :])
        acc = part if acc is None else acc + part
    o_ref[...] = x + 0.5 * acc


def _ffn(x2, gain, w_gu, w_down, layer, *, tm, tf):
    t, d = x2.shape
    d_ff = w_down.shape[1]
    resident = pl.Buffered(1)
    return pl.pallas_call(
        functools.partial(_ffn_kernel, tf=tf),
        grid=(t // tm,),
        in_specs=[
            pl.BlockSpec((tm, d), lambda i: (i, 0)),
            pl.BlockSpec((1, d), lambda i: (0, 0)),
            pl.BlockSpec((None, d, 2 * d_ff), lambda i: (layer, 0, 0), pipeline_mode=resident),
            pl.BlockSpec((None, d_ff, d), lambda i: (layer, 0, 0), pipeline_mode=resident),
        ],
        out_specs=pl.BlockSpec((tm, d), lambda i: (i, 0)),
        out_shape=jax.ShapeDtypeStruct((t, d), F32),
        compiler_params=_params("parallel"),
        name="ffn",
    )(x2, gain.reshape(1, d), w_gu, w_down)


def _gelu_tanh(x):
    c = 0.7978845608028654
    return 0.5 * x * (1.0 + jnp.tanh(c * (x + 0.044715 * (x * x * x))))


def _even_prep_kernel(x_ref, g_ref, win_ref, lng_ref, lnb_ref, sw_ref, sb_ref,
                      qt_ref, k_ref, vt_ref, osg_ref):
    tm = x_ref.shape[1]
    kb = vt_ref.shape[3]
    w = SB_WIDTH
    h = _rms(x_ref[0], g_ref[...]).astype(BF16)
    qt_ref[0] = (_dot(h, win_ref[:, 0:w]) * (SB_HEAD_DIM ** -0.5 * LOG2E)).T.astype(BF16)
    k_ref[0] = _dot(h, win_ref[:, w:2 * w]).astype(BF16)
    v = _dot(h, win_ref[:, 2 * w:3 * w])
    for c in range(tm // kb):
        vt_ref[0, c] = v[c * kb:(c + 1) * kb, :].T.astype(BF16)
    osg_ref = osg_ref.at[0]
    u = _gelu_tanh(_dot(h, win_ref[:, 3 * w:3 * w + SG_WIDTH]))
    g = _gelu_tanh(_dot(h, win_ref[:, 3 * w + SG_WIDTH:3 * w + 2 * SG_WIDTH]))
    mu = jnp.mean(g, axis=-1, keepdims=True)
    gc = g - mu
    var = jnp.mean(gc * gc, axis=-1, keepdims=True)
    gn = (gc * lax.rsqrt(var + EPS) * lng_ref[...] + lnb_ref[...]).astype(BF16)

    row = lax.broadcasted_iota(jnp.int32, (SG_CHUNK, SG_CHUNK), 0)
    col = lax.broadcasted_iota(jnp.int32, (SG_CHUNK, SG_CHUNK), 1)
    tri = col <= row
    first_group = lax.broadcasted_iota(jnp.int32, (SG_CHUNK, LANES), 1) < SG_GROUP_DIM
    for p in range(SG_GROUPS // 2):
        lanes = slice(p * LANES, (p + 1) * LANES)
        w0 = jnp.where(tri, sw_ref[2 * p], 0.0).astype(BF16)
        w1 = jnp.where(tri, sw_ref[2 * p + 1], 0.0).astype(BF16)
        bias = sb_ref[:, lanes]
        for c in range(tm // SG_CHUNK):
            rows = slice(c * SG_CHUNK, (c + 1) * SG_CHUNK)
            gp = gn[rows, lanes]
            mixed = jnp.where(first_group, _dot(w0, gp), _dot(w1, gp)) + bias
            osg_ref[rows, lanes] = (u[rows, lanes] * mixed).astype(BF16)


def _even_prep(x, gain, w_in, ln_g, ln_b, sgu_w, sgu_bias_full, *, tm, kb):
    b, s, d = x.shape
    n_in = w_in.shape[1]
    w = SB_WIDTH
    const2 = lambda bi, i: (0, 0)
    row_out = jax.ShapeDtypeStruct((b, s, w), BF16)
    row_spec = pl.BlockSpec((1, tm, w), lambda bi, i: (bi, i, 0))
    return pl.pallas_call(
        _even_prep_kernel,
        grid=(b, s // tm),
        in_specs=[
            pl.BlockSpec((1, tm, d), lambda bi, i: (bi, i, 0)),
            pl.BlockSpec((1, d), const2),
            pl.BlockSpec((d, n_in), const2),
            pl.BlockSpec((1, SG_WIDTH), const2),
            pl.BlockSpec((1, SG_WIDTH), const2),
            pl.BlockSpec((SG_GROUPS, SG_CHUNK, SG_CHUNK), lambda bi, i: (0, 0, 0)),
            pl.BlockSpec((SG_CHUNK, SG_WIDTH), const2),
        ],
        out_specs=[pl.BlockSpec((1, w, tm), lambda bi, i: (bi, 0, i)),
                   row_spec,
                   pl.BlockSpec((1, tm // kb, w, kb), lambda bi, i: (bi, i, 0, 0)),
                   row_spec],
        out_shape=[jax.ShapeDtypeStruct((b, w, s), BF16), row_out,
                   jax.ShapeDtypeStruct((b, s // kb, w, kb), BF16), row_out],
        compiler_params=_params("parallel", "parallel"),
        name="even_prep",
    )(x, gain.reshape(1, d), w_in, ln_g.reshape(1, -1), ln_b.reshape(1, -1),
      sgu_w, sgu_bias_full)


def _sb_attn_kernel(qt_ref, k_ref, vt_ref, o_ref, acc_ref, r_ref, z0_ref, z1_ref, zc0_ref,
                    zc1_ref, t0_ref, t1_ref, bs_ref, kn_ref, *, tq, kb):
    i = pl.program_id(2)
    n_sub = tq // kb
    hd_dim = SB_HEAD_DIM
    z_refs, zc_refs, t_refs = (z0_ref, z1_ref), (zc0_ref, zc1_ref), (t0_ref, t1_ref)
    first_head = lax.broadcasted_iota(jnp.int32, (LANES, 1), 0) < hd_dim
    qt = qt_ref[0]
    zero = jnp.zeros_like(qt)
    qt_heads = (jnp.where(first_head, qt, zero), jnp.where(first_head, zero, qt))
    key = lax.broadcasted_iota(jnp.int32, (kb, tq), 0)
    query = lax.broadcasted_iota(jnp.int32, (kb, tq), 1)
    srow = lax.broadcasted_iota(jnp.int32, (kb, kb), 0)
    scol = lax.broadcasted_iota(jnp.int32, (kb, kb), 1)
    suffix = (scol >= srow).astype(BF16)

    def score(item):
        g, hd, slot = item
        start = pl.multiple_of(g * kb, kb)
        z_refs[slot][...] = _dot(k_ref[0, pl.ds(start, kb), :], qt_heads[hd])

    def stay(item, diag_sub):
        g, hd, slot = item
        z = z_refs[slot][...]
        sp = jnp.maximum(z, 0.0) + jnp.log2(1.0 + jnp.exp2(-jnp.abs(z)))
        if diag_sub is not None:
            visible = (key + diag_sub * kb) < query
            sp = jnp.where(visible, sp, 0.0)
            z = jnp.where(visible, z, MASKED)
        zc_refs[slot][...] = z
        t_refs[slot][...] = _dot(suffix, sp.astype(BF16))
        bs_ref[slot] = jnp.sum(sp, axis=0, keepdims=True)

    def weigh(item):
        g, hd, slot = item
        wgt = jnp.exp2(zc_refs[slot][...] - t_refs[slot][...] - r_ref[hd])
        vt = vt_ref[0, g, hd * hd_dim:(hd + 1) * hd_dim, :]
        acc_ref[hd] += _dot(vt, wgt.astype(BF16))
        r_ref[hd] += bs_ref[slot]

    def block(g, has_prev, g_next, diag_sub):
        score((g, 1, 1))
        if has_prev:
            weigh((g + 1, 1, 1))
        stay((g, 0, 0), diag_sub)
        score((g_next, 0, 0))
        weigh((g, 0, 0))
        stay((g, 1, 1), diag_sub)

    @pl.when(i == 0)
    def _():
        kf = k_ref[0].astype(F32)
        dim = lax.broadcasted_iota(jnp.int32, (LANES, LANES), 0)
        head = lax.broadcasted_iota(jnp.int32, (LANES, LANES), 1)
        select = ((dim < hd_dim) == (head == 0)) & (head < 2)
        kn_ref[...] = jnp.max(_dot((kf * kf).astype(BF16), select.astype(BF16)),
                              axis=0, keepdims=True)

    lane = lax.broadcasted_iota(jnp.int32, (1, LANES), 1)
    qf = qt.astype(F32)
    exit_level = []
    for hd in range(2):
        q_sq = jnp.sum(jnp.square(qf[hd * hd_dim:(hd + 1) * hd_dim, :]), axis=0, keepdims=True)
        k_sq = jnp.max(jnp.where(lane == hd, kn_ref[...], 0.0), axis=1, keepdims=True)
        exit_level.append(1.02 * jnp.sqrt(q_sq * k_sq) + UNDERFLOW_LOG2)

    acc_ref[...] = jnp.zeros_like(acc_ref)
    r_ref[...] = jnp.zeros_like(r_ref)
    first = i * n_sub
    score((first + n_sub - 1, 0, 0))
    for d in reversed(range(n_sub)):
        block(first + d, d < n_sub - 1, jnp.maximum(first + d - 1, 0), d)

    def more(carry):
        g, live = carry
        return (g >= 0) & (live > 0)

    def body(carry):
        g, _ = carry
        block(g, True, jnp.maximum(g - 1, 0), None)
        dead = ((jnp.min(r_ref[0] - exit_level[0]) > 0.0)
                & (jnp.min(r_ref[1] - exit_level[1]) > 0.0))
        return g - 1, jnp.where(dead, 0, 1).astype(jnp.int32)

    g_end, _ = lax.while_loop(more, body, (first - 1, jnp.int32(1)))
    weigh((g_end + 1, 1, 1))
    out_t = jnp.concatenate([acc_ref[0], acc_ref[1]], axis=0)
    o_ref[0] = out_t.T.astype(BF16)


def _sb_attention(qt, k, vt, *, tq):
    b, w, s = qt.shape
    kb = vt.shape[3]
    return pl.pallas_call(
        functools.partial(_sb_attn_kernel, tq=tq, kb=kb),
        grid=(b, w // LANES, s // tq),
        in_specs=[
            pl.BlockSpec((1, LANES, tq), lambda bi, p, i: (bi, p, i)),
            pl.BlockSpec((1, s, LANES), lambda bi, p, i: (bi, 0, p)),
            pl.BlockSpec((1, s // kb, LANES, kb), lambda bi, p, i: (bi, 0, p, 0)),
        ],
        out_specs=pl.BlockSpec((1, tq, LANES), lambda bi, p, i: (bi, i, p)),
        out_shape=jax.ShapeDtypeStruct((b, s, w), BF16),
        scratch_shapes=[pltpu.VMEM((2, SB_HEAD_DIM, tq), F32), pltpu.VMEM((2, 1, tq), F32),
                        *[pltpu.VMEM((kb, tq), F32) for _ in range(6)],
                        pltpu.VMEM((2, 1, tq), F32), pltpu.VMEM((1, LANES), F32)],
        compiler_params=_params("parallel", "parallel", "arbitrary"),
        name="sb_attn",
    )(qt, k, vt)


def _lane_tile(t, width):
    return jnp.concatenate([t] * (width // t.shape[1]), axis=1)


def _rms_rows(xt, gain):
    ms = jnp.sum(xt * xt, axis=0, keepdims=True) * (1.0 / xt.shape[0])
    return xt * lax.rsqrt(ms + EPS) * _lane_tile(gain, xt.shape[1])


def _mla_prep_kernel(x_ref, pos_ref, g_ref, win_ref, qlg_ref, kvlg_ref, wuqt_ref, wukt_ref,
                     wuvt_ref, qg_ref, kg_ref, freq_ref, qt_ref, k_ref, vt_ref):
    tm = x_ref.shape[1]
    kb = vt_ref.shape[3]
    lat = MLA_Q_LORA + MLA_KV_LORA
    half = MLA_ROPE // 2
    h = _rms(x_ref[0], g_ref[...]).astype(BF16)
    ct = _dot(h, win_ref[...]).T
    cqn = _rms_rows(ct[0:MLA_Q_LORA], qlg_ref[...]).astype(BF16)
    ckvn = _rms_rows(ct[MLA_Q_LORA:lat], kvlg_ref[...]).astype(BF16)
    k_r = ct[lat:lat + MLA_ROPE]
    vt = _dot(wuvt_ref[...], ckvn)
    for c in range(tm // kb):
        vt_ref[0, c] = vt[:, c * kb:(c + 1) * kb].astype(BF16)
    q_all = _dot(wuqt_ref[...], cqn)
    kn_all = _dot(wukt_ref[...], ckvn)

    angle = _lane_tile(freq_ref[...], tm) * pos_ref[0].astype(F32)
    cos = jnp.cos(angle)
    sin = jnp.sin(angle)

    def rope(t):
        t1, t2 = t[0:half], t[half:]
        return jnp.concatenate([t1 * cos - t2 * sin, t1 * sin + t2 * cos], axis=0)

    qg = _lane_tile(qg_ref[...], tm)
    kg = _lane_tile(kg_ref[...], tm)
    zeros = jnp.zeros((LANES - MLA_QK, tm), F32)
    inv_n = 1.0 / MLA_QK
    kr_sq = jnp.sum(k_r * k_r, axis=0, keepdims=True)
    kr_roped = rope(k_r * kg[MLA_NOPE:])
    for hd in range(MLA_HEADS):
        qh = q_all[hd * MLA_QK:(hd + 1) * MLA_QK]
        r = lax.rsqrt(jnp.sum(qh * qh, axis=0, keepdims=True) * inv_n + EPS)
        qn = qh * r * qg
        qt_ref[0, hd] = jnp.concatenate([qn[0:MLA_NOPE], rope(qn[MLA_NOPE:]), zeros],
                                        axis=0).astype(BF16)
        kn = kn_all[hd * MLA_NOPE:(hd + 1) * MLA_NOPE]
        r = lax.rsqrt((jnp.sum(kn * kn, axis=0, keepdims=True) + kr_sq) * inv_n + EPS)
        kt = jnp.concatenate([kn * r * kg[0:MLA_NOPE], kr_roped * r, zeros], axis=0)
        k_ref[0, hd] = kt.T.astype(BF16)


def _mla_prep(x, positions, gain, w_in, qlg, kvlg, wuqt, wukt, wuvt, qg, kg, freq, *, tm, kb):
    b, s, d = x.shape
    vw = MLA_HEADS * MLA_V
    const2 = lambda bi, i: (0, 0)
    full = lambda a: pl.BlockSpec(a.shape, const2)
    small = [gain.reshape(1, -1), w_in, qlg, kvlg, wuqt, wukt, wuvt, qg, kg, freq]
    return pl.pallas_call(
        _mla_prep_kernel,
        grid=(b, s // tm),
        in_specs=[
            pl.BlockSpec((1, tm, d), lambda bi, i: (bi, i, 0)),
            pl.BlockSpec((1, 1, tm), lambda bi, i: (bi, 0, i)),
            *[full(a) for a in small],
        ],
        out_specs=[pl.BlockSpec((1, MLA_HEADS, LANES, tm), lambda bi, i: (bi, 0, 0, i)),
                   pl.BlockSpec((1, MLA_HEADS, tm, LANES), lambda bi, i: (bi, 0, i, 0)),
                   pl.BlockSpec((1, tm // kb, vw, kb), lambda bi, i: (bi, i, 0, 0))],
        out_shape=[jax.ShapeDtypeStruct((b, MLA_HEADS, LANES, s), BF16),
                   jax.ShapeDtypeStruct((b, MLA_HEADS, s, LANES), BF16),
                   jax.ShapeDtypeStruct((b, s // kb, vw, kb), BF16)],
        compiler_params=_params("parallel", "parallel"),
        name="mla_prep",
    )(x, positions.reshape(b, 1, s), *small)


def _mla_attn_kernel(qt_ref, k_ref, vt_ref, o_ref, acc_ref, m_ref, s0_ref, s1_ref, bm_ref,
                     *, tq):
    i = pl.program_id(2)
    n_heads = qt_ref.shape[1]
    s_refs = (s0_ref, s1_ref)
    key = lax.broadcasted_iota(jnp.int32, (tq, tq), 0)
    query = lax.broadcasted_iota(jnp.int32, (tq, tq), 1)
    causal = key <= query

    def produce(g, hd, masked):
        start = pl.multiple_of(g * tq, tq)
        sc = _dot(k_ref[0, hd, pl.ds(start, tq), :], qt_ref[0, hd])
        if masked:
            sc = jnp.where(causal, sc, MASKED)
        s_refs[hd % 2][...] = sc
        bm_ref[hd % 2] = jnp.max(sc, axis=0, keepdims=True)

    ones_rows = jnp.ones((SUM_ROWS, tq), BF16)

    def consume(g, hd):
        m_old = m_ref[hd]
        m_new = jnp.maximum(m_old, bm_ref[hd % 2])
        alpha = jnp.exp2(m_old - m_new)
        p = jnp.exp2(s_refs[hd % 2][...] - m_new)
        vt = jnp.concatenate([vt_ref[0, g, hd * MLA_V:(hd + 1) * MLA_V, :], ones_rows], axis=0)
        acc_ref[hd] = alpha * acc_ref[hd] + _dot(vt, p.astype(BF16))
        m_ref[hd] = m_new

    def visit(g, g_next, masked):
        for hd in range(n_heads):
            if hd + 1 < n_heads:
                produce(g, hd + 1, masked)
            else:
                produce(g_next, 0, False)
            consume(g, hd)

    acc_ref[...] = jnp.zeros_like(acc_ref)
    m_ref[...] = jnp.full_like(m_ref, MASKED)
    produce(i, 0, True)
    visit(i, 0, True)

    def body(it, carry):
        visit(2 * it, 2 * it + 1, False)
        visit(2 * it + 1, jnp.minimum(2 * it + 2, i - 1), False)
        return carry

    lax.fori_loop(0, i // 2, body, 0)

    @pl.when(i % 2 == 1)
    def _():
        visit(i - 1, i - 1, False)

    out_t = jnp.concatenate([acc_ref[hd, 0:MLA_V, :] / acc_ref[hd, MLA_V:MLA_V + 1, :]
                             for hd in range(n_heads)], axis=0)
    o_ref[0] = out_t.T.astype(BF16)


def _mla_attention(qt, k, vt, *, tq, heads_per_step):
    b, heads, _, s = qt.shape
    kb = vt.shape[3]
    nh = heads_per_step
    assert kb == tq and nh % 2 == 0
    vw = nh * MLA_V
    return pl.pallas_call(
        functools.partial(_mla_attn_kernel, tq=tq),
        grid=(b, heads // nh, s // tq),
        in_specs=[
            pl.BlockSpec((1, nh, LANES, tq), lambda bi, p, i: (bi, p, 0, i)),
            pl.BlockSpec((1, nh, s, LANES), lambda bi, p, i: (bi, p, 0, 0)),
            pl.BlockSpec((1, s // kb, vw, kb), lambda bi, p, i: (bi, 0, p, 0)),
        ],
        out_specs=pl.BlockSpec((1, tq, vw), lambda bi, p, i: (bi, i, p)),
        out_shape=jax.ShapeDtypeStruct((b, s, heads * MLA_V), BF16),
        scratch_shapes=[pltpu.VMEM((nh, MLA_V + SUM_ROWS, tq), F32),
                        pltpu.VMEM((nh, 1, tq), F32), pltpu.VMEM((kb, tq), F32),
                        pltpu.VMEM((kb, tq), F32), pltpu.VMEM((2, 1, tq), F32)],
        compiler_params=_params("parallel", "parallel", "arbitrary"),
        name="mla_attn",
    )(qt, k, vt)


def _mem_kv_kernel(mem_ref, g_ref, wkv_ref, kg_ref, k_ref, v_ref):
    hm = _rms(mem_ref[0], g_ref[...]).astype(BF16)
    hd_dim = kg_ref.shape[1]
    for hd in range(MEM_HEADS):
        kcols = slice(2 * hd * hd_dim, (2 * hd + 1) * hd_dim)
        vcols = slice((2 * hd + 1) * hd_dim, (2 * hd + 2) * hd_dim)
        out = slice(hd * hd_dim, (hd + 1) * hd_dim)
        k_ref[0, :, out] = _rms(_dot(hm, wkv_ref[:, kcols]), kg_ref[...]).astype(BF16)
        v_ref[0, :, out] = _dot(hm, wkv_ref[:, vcols]).astype(BF16)


def _mem_kv(mem, gain, wkv, k_gain):
    b, m, d = mem.shape
    out = jax.ShapeDtypeStruct((b, m, d), BF16)
    spec = pl.BlockSpec((1, m, d), lambda bi: (bi, 0, 0))
    return pl.pallas_call(
        _mem_kv_kernel,
        grid=(b,),
        in_specs=[spec,
                  pl.BlockSpec((1, d), lambda bi: (0, 0)),
                  pl.BlockSpec(wkv.shape, lambda bi: (0, 0)),
                  pl.BlockSpec((1, k_gain.shape[0]), lambda bi: (0, 0))],
        out_specs=[spec, spec],
        out_shape=[out, out],
        compiler_params=_params("parallel"),
        name="mem_kv",
    )(mem, gain.reshape(1, d), wkv, k_gain.reshape(1, -1))


def _mix_xattn_kernel(*refs, n_act):
    x_ref = refs[0]
    act_refs = refs[1:1 + n_act]
    w_ref, g_ref, wq_ref, qg_ref, k_ref, v_ref, wo_ref, o_ref = refs[1 + n_act:]
    act = jnp.concatenate([a_ref[0] for a_ref in act_refs], axis=-1)
    x1 = x_ref[0] + _dot(act, w_ref[...])
    h = _rms(x1, g_ref[...]).astype(BF16)
    hd_dim = qg_ref.shape[1]
    cols = [slice(hd * hd_dim, (hd + 1) * hd_dim) for hd in range(MEM_HEADS)]
    q = [_dot(h, wq_ref[:, c]) for c in cols]
    qn = [(_rms(q_h, qg_ref[...]) * (hd_dim ** -0.5)).astype(BF16) for q_h in q]
    sc = [_dot_nt(qn_h, k_ref[0, :, c]) for qn_h, c in zip(qn, cols)]
    p = [jnp.exp(s_h - jnp.max(s_h, axis=-1, keepdims=True)) for s_h in sc]
    o = [(_dot(p_h.astype(BF16), v_ref[0, :, c]) / jnp.sum(p_h, axis=-1, keepdims=True)
          ).astype(BF16) for p_h, c in zip(p, cols)]
    o_ref[0] = x1 + _dot(jnp.concatenate(o, axis=-1), wo_ref[...])


def _mix_xattn(x, acts, w_mix, gain, wq, q_gain, mem_k, mem_v, wo, *, tm):
    b, s, d = x.shape
    m = mem_k.shape[1]
    const2 = lambda bi, i: (0, 0)
    row = lambda width: pl.BlockSpec((1, tm, width), lambda bi, i: (bi, i, 0))
    mem_spec = pl.BlockSpec((1, m, d), lambda bi, i: (bi, 0, 0))
    return pl.pallas_call(
        functools.partial(_mix_xattn_kernel, n_act=len(acts)),
        grid=(b, s // tm),
        in_specs=[
            row(d),
            *[row(a.shape[-1]) for a in acts],
            pl.BlockSpec(w_mix.shape, const2),
            pl.BlockSpec((1, d), const2),
            pl.BlockSpec(wq.shape, const2),
            pl.BlockSpec((1, q_gain.shape[0]), const2),
            mem_spec, mem_spec,
            pl.BlockSpec(wo.shape, const2),
        ],
        out_specs=row(d),
        out_shape=jax.ShapeDtypeStruct((b, s, d), F32),
        compiler_params=_params("parallel", "parallel"),
        name="mix_xattn",
    )(x, *acts, w_mix, gain.reshape(1, d), wq, q_gain.reshape(1, -1), mem_k, mem_v, wo)


def _lane_bcast(vec):
    return jnp.broadcast_to(vec[:, None], (vec.shape[0], LANES))


def _mla_weights(w_in, w_uq, w_ukv, q_lora_gain, kv_lora_gain, q_gain, k_gain):
    lat = MLA_Q_LORA + MLA_KV_LORA
    w_in_ext = jnp.pad(w_in, ((0, 0), (0, lat + LANES - w_in.shape[1])))
    wukv = w_ukv.reshape(MLA_KV_LORA, MLA_HEADS, MLA_NOPE + MLA_V)
    wukt = wukv[..., :MLA_NOPE].reshape(MLA_KV_LORA, -1).T
    wuvt = wukv[..., MLA_NOPE:].reshape(MLA_KV_LORA, -1).T
    half = MLA_ROPE // 2
    inv_freq = ROPE_THETA ** (-jnp.arange(half, dtype=F32) / half)
    bf = lambda a: a.astype(BF16)
    return (bf(w_in_ext), _lane_bcast(q_lora_gain), _lane_bcast(kv_lora_gain),
            bf(w_uq.T), bf(wukt), bf(wuvt),
            _lane_bcast(q_gain * (MLA_QK ** -0.5 * LOG2E)), _lane_bcast(k_gain),
            _lane_bcast(inv_freq))


def _tile(n, pref):
    return pref if n % pref == 0 else n


def kernel(x, mem, positions, ffn_pre_norm, ffn_pre_w_gu, ffn_pre_w_down, mix_norm, sbg_w_in, sgu_ln_gain, sgu_ln_bias, sgu_w, sgu_b, sbg_w_out, mla_w_in, mla_q_lora_gain, mla_kv_lora_gain, mla_w_uq, mla_w_ukv, mla_q_gain, mla_k_gain, mla_w_out, xmem_norm, xmem_mem_norm, xmem_wq, xmem_wkv, xmem_q_gain, xmem_k_gain, xmem_wo, ffn_post_norm, ffn_post_w_gu, ffn_post_w_down):
    b, s, d = x.shape
    depth = ffn_pre_norm.shape[0]
    d_ff = ffn_pre_w_down.shape[1]
    t = b * s
    ffn_tm = _tile(t, 512)
    ffn_tf = _tile(d_ff, 256)
    pre_w_gu, pre_w_down = ffn_pre_w_gu.astype(BF16), ffn_pre_w_down.astype(BF16)
    post_w_gu, post_w_down = ffn_post_w_gu.astype(BF16), ffn_post_w_down.astype(BF16)
    row_tm = _tile(s, 512)
    mla_tq = _tile(s, 512)
    sb_tq = _tile(s, 512)
    sb_kb = _tile(sb_tq, 256)
    bf = lambda a: a.astype(BF16)

    for layer in range(depth):
        x = _ffn(x.reshape(t, d), ffn_pre_norm[layer], pre_w_gu, pre_w_down, layer,
                 tm=ffn_tm, tf=ffn_tf).reshape(b, s, d)
        if layer % 2 == 0:
            e = layer // 2
            bias_full = jnp.repeat(sgu_b[e].T, SG_GROUP_DIM, axis=1)
            qt, k, vt, o_sg = _even_prep(
                x, mix_norm[layer], bf(sbg_w_in[e]), sgu_ln_gain[e],
                sgu_ln_bias[e], sgu_w[e], bias_full, tm=row_tm, kb=sb_kb)
            acts = (_sb_attention(qt, k, vt, tq=sb_tq), o_sg)
            w_mix = bf(sbg_w_out[e])
        else:
            o = layer // 2
            mla_consts = _mla_weights(
                mla_w_in[o], mla_w_uq[o], mla_w_ukv[o], mla_q_lora_gain[o],
                mla_kv_lora_gain[o], mla_q_gain[o], mla_k_gain[o])
            qt, k, vt = _mla_prep(x, positions, mix_norm[layer], *mla_consts,
                                  tm=row_tm, kb=mla_tq)
            acts = (_mla_attention(qt, k, vt, tq=mla_tq, heads_per_step=4),)
            w_mix = bf(mla_w_out[o])
        mem_k, mem_v = _mem_kv(mem, xmem_mem_norm[layer], bf(xmem_wkv[layer]),
                               xmem_k_gain[layer])
        x = _mix_xattn(x, acts, w_mix, xmem_norm[layer], bf(xmem_wq[layer]), xmem_q_gain[layer],
                       mem_k, mem_v, bf(xmem_wo[layer]), tm=row_tm)
        x = _ffn(x.reshape(t, d), ffn_post_norm[layer], post_w_gu, post_w_down, layer,
                 tm=ffn_tm, tf=ffn_tf).reshape(b, s, d)
    return x
```

```python
import functools

import jax
import jax.numpy as jnp
from jax import lax
from jax.experimental import pallas as pl
from jax.experimental.pallas import tpu as pltpu

EPS = 1e-6
ROPE_THETA = 10000.0
LANES = 128
VMEM_LIMIT_BYTES = 56 * 1024 * 1024

SB_HEADS, SB_HEAD_DIM = 8, 64
SB_WIDTH = SB_HEADS * SB_HEAD_DIM
SG_GROUPS, SG_GROUP_DIM, SG_CHUNK = 8, 64, 128
SG_WIDTH = SG_GROUPS * SG_GROUP_DIM
MLA_HEADS, MLA_NOPE, MLA_ROPE, MLA_V = 16, 64, 32, 64
MLA_QK = MLA_NOPE + MLA_ROPE
MLA_Q_LORA, MLA_KV_LORA = 512, 256
MEM_HEADS = 4

BF16 = jnp.bfloat16
F32 = jnp.float32
LOG2E = 1.4426950408889634
MASKED = -1e30
SUM_ROWS = 16
UNDERFLOW_LOG2 = 160.0


def _params(*semantics):
    return pltpu.CompilerParams(dimension_semantics=semantics,
                                vmem_limit_bytes=VMEM_LIMIT_BYTES)


def _dot(a, b):
    return jnp.dot(a, b, preferred_element_type=F32)


def _dot_nt(a, b):
    return lax.dot_general(a, b, (((1,), (1,)), ((), ())), preferred_element_type=F32)


def _rms(x, gain, n=None):
    n = x.shape[-1] if n is None else n
    ms = jnp.sum(x * x, axis=-1, keepdims=True) * (1.0 / n)
    return x * lax.rsqrt(ms + EPS) * gain


def _ffn_kernel(x_ref, g_ref, wgu_ref, wd_ref, o_ref, *, tf):
    d_ff = wd_ref.shape[0]
    x = x_ref[...]
    h = _rms(x, g_ref[...]).astype(BF16)
    acc = None
    for c in range(d_ff // tf):
        cols = slice(c * tf, (c + 1) * tf)
        gate = _dot(h, wgu_ref[:, cols].astype(BF16))
        up = _dot(h, wgu_ref[:, d_ff + c * tf:d_ff + (c + 1) * tf].astype(BF16))
        act = (gate * jax.nn.sigmoid(gate) * up).astype(BF16)
        part = _dot(act, wd_ref[cols, :].astype(BF16))
        acc = part if acc is None else acc + part
    o_ref[...] = x + 0.5 * acc


def _ffn(x2, gain, w_gu, w_down, layer, *, tm, tf):
    t, d = x2.shape
    d_ff = w_down.shape[1]
    resident = pl.Buffered(1)
    return pl.pallas_call(
        functools.partial(_ffn_kernel, tf=tf),
        grid=(t // tm,),
        in_specs=[
            pl.BlockSpec((tm, d), lambda i: (i, 0)),
            pl.BlockSpec((1, d), lambda i: (0, 0)),
            pl.BlockSpec((None, d, 2 * d_ff), lambda i: (layer, 0, 0), pipeline_mode=resident),
            pl.BlockSpec((None, d_ff, d), lambda i: (layer, 0, 0), pipeline_mode=resident),
        ],
        out_specs=pl.BlockSpec((tm, d), lambda i: (i, 0)),
        out_shape=jax.ShapeDtypeStruct((t, d), F32),
        compiler_params=_params("parallel"),
        name="ffn",
    )(x2, gain.reshape(1, d), w_gu, w_down)


def _gelu_tanh(x):
    c = 0.7978845608028654
    return 0.5 * x * (1.0 + jnp.tanh(c * (x + 0.044715 * (x * x * x))))


def _even_prep_kernel(x_ref, g_ref, win_ref, lng_ref, lnb_ref, sw_ref, sb_ref,
                      qt_ref, k_ref, vt_ref, osg_ref):
    tm = x_ref.shape[1]
    kb = vt_ref.shape[3]
    w = SB_WIDTH
    h = _rms(x_ref[0], g_ref[...]).astype(BF16)
    osg_ref = osg_ref.at[0]
    u_raw = _dot(h, win_ref[:, 3 * w:3 * w + SG_WIDTH])
    g_raw = _dot(h, win_ref[:, 3 * w + SG_WIDTH:3 * w + 2 * SG_WIDTH])
    qt_ref[0] = (_dot(h, win_ref[:, 0:w]) * (SB_HEAD_DIM ** -0.5 * LOG2E)).T.astype(BF16)
    u = _gelu_tanh(u_raw)
    k_ref[0] = _dot(h, win_ref[:, w:2 * w]).astype(BF16)
    g = _gelu_tanh(g_raw)
    mu = jnp.mean(g, axis=-1, keepdims=True)
    gc = g - mu
    var = jnp.mean(gc * gc, axis=-1, keepdims=True)
    gn = (gc * lax.rsqrt(var + EPS) * lng_ref[...] + lnb_ref[...]).astype(BF16)
    v = _dot(h, win_ref[:, 2 * w:3 * w])
    for c in range(tm // kb):
        vt_ref[0, c] = v[c * kb:(c + 1) * kb, :].T.astype(BF16)

    row = lax.broadcasted_iota(jnp.int32, (SG_CHUNK, SG_CHUNK), 0)
    col = lax.broadcasted_iota(jnp.int32, (SG_CHUNK, SG_CHUNK), 1)
    tri = col <= row
    first_group = lax.broadcasted_iota(jnp.int32, (SG_CHUNK, LANES), 1) < SG_GROUP_DIM
    for p in range(SG_GROUPS // 2):
        lanes = slice(p * LANES, (p + 1) * LANES)
        w0 = jnp.where(tri, sw_ref[2 * p], 0.0).astype(BF16)
        w1 = jnp.where(tri, sw_ref[2 * p + 1], 0.0).astype(BF16)
        bias = sb_ref[:, lanes]
        for c in range(tm // SG_CHUNK):
            rows = slice(c * SG_CHUNK, (c + 1) * SG_CHUNK)
            gp = gn[rows, lanes]
            mixed = jnp.where(first_group, _dot(w0, gp), _dot(w1, gp)) + bias
            osg_ref[rows, lanes] = (u[rows, lanes] * mixed).astype(BF16)


def _even_prep(x, gain, w_in, ln_g, ln_b, sgu_w, sgu_bias_full, *, tm, kb):
    b, s, d = x.shape
    n_in = w_in.shape[1]
    w = SB_WIDTH
    const2 = lambda bi, i: (0, 0)
    row_out = jax.ShapeDtypeStruct((b, s, w), BF16)
    row_spec = pl.BlockSpec((1, tm, w), lambda bi, i: (bi, i, 0))
    return pl.pallas_call(
        _even_prep_kernel,
        grid=(b, s // tm),
        in_specs=[
            pl.BlockSpec((1, tm, d), lambda bi, i: (bi, i, 0)),
            pl.BlockSpec((1, d), const2),
            pl.BlockSpec((d, n_in), const2),
            pl.BlockSpec((1, SG_WIDTH), const2),
            pl.BlockSpec((1, SG_WIDTH), const2),
            pl.BlockSpec((SG_GROUPS, SG_CHUNK, SG_CHUNK), lambda bi, i: (0, 0, 0)),
            pl.BlockSpec((SG_CHUNK, SG_WIDTH), const2),
        ],
        out_specs=[pl.BlockSpec((1, w, tm), lambda bi, i: (bi, 0, i)),
                   row_spec,
                   pl.BlockSpec((1, tm // kb, w, kb), lambda bi, i: (bi, i, 0, 0)),
                   row_spec],
        out_shape=[jax.ShapeDtypeStruct((b, w, s), BF16), row_out,
                   jax.ShapeDtypeStruct((b, s // kb, w, kb), BF16), row_out],
        compiler_params=_params("parallel", "parallel"),
        name="even_prep",
    )(x, gain.reshape(1, d), w_in, ln_g.reshape(1, -1), ln_b.reshape(1, -1),
      sgu_w, sgu_bias_full)


def _sb_attn_kernel(qt_ref, k_ref, vt_ref, o_ref, acc_ref, r_ref, z0_ref, z1_ref, zc0_ref,
                    zc1_ref, t0_ref, t1_ref, bs_ref, kn_ref, *, tq, kb):
    i = pl.program_id(2)
    n_sub = tq // kb
    hd_dim = SB_HEAD_DIM
    z_refs, zc_refs, t_refs = (z0_ref, z1_ref), (zc0_ref, zc1_ref), (t0_ref, t1_ref)
    first_head = lax.broadcasted_iota(jnp.int32, (LANES, 1), 0) < hd_dim
    qt = qt_ref[0]
    zero = jnp.zeros_like(qt)
    qt_heads = (jnp.where(first_head, qt, zero), jnp.where(first_head, zero, qt))
    key = lax.broadcasted_iota(jnp.int32, (kb, tq), 0)
    query = lax.broadcasted_iota(jnp.int32, (kb, tq), 1)
    srow = lax.broadcasted_iota(jnp.int32, (kb, kb), 0)
    scol = lax.broadcasted_iota(jnp.int32, (kb, kb), 1)
    suffix = (scol >= srow).astype(BF16)

    def score(item):
        g, hd, slot = item
        start = pl.multiple_of(g * kb, kb)
        z_refs[slot][...] = _dot(k_ref[0, pl.ds(start, kb), :], qt_heads[hd])

    def stay(item, diag_sub):
        g, hd, slot = item
        z = z_refs[slot][...]
        sp = jnp.maximum(z, 0.0) + jnp.log2(1.0 + jnp.exp2(-jnp.abs(z)))
        if diag_sub is not None:
            visible = (key + diag_sub * kb) < query
            sp = jnp.where(visible, sp, 0.0)
            z = jnp.where(visible, z, MASKED)
        zc_refs[slot][...] = z
        tail = _dot(suffix, sp.astype(BF16))
        t_refs[slot][...] = tail
        bs_ref[slot] = tail[0:1, :]

    def weigh(item):
        g, hd, slot = item
        wgt = jnp.exp2(zc_refs[slot][...] - t_refs[slot][...] - r_ref[hd])
        vt = vt_ref[0, g, hd * hd_dim:(hd + 1) * hd_dim, :]
        acc_ref[hd] += _dot(vt, wgt.astype(BF16))
        r_ref[hd] += bs_ref[slot]

    def block(g, has_prev, g_next, diag_sub):
        score((g, 1, 1))
        if has_prev:
            weigh((g + 1, 1, 1))
        stay((g, 0, 0), diag_sub)
        score((g_next, 0, 0))
        weigh((g, 0, 0))
        stay((g, 1, 1), diag_sub)

    @pl.when(i == 0)
    def _():
        kf = k_ref[0].astype(F32)
        dim = lax.broadcasted_iota(jnp.int32, (LANES, LANES), 0)
        head = lax.broadcasted_iota(jnp.int32, (LANES, LANES), 1)
        select = ((dim < hd_dim) == (head == 0)) & (head < 2)
        kn_ref[...] = jnp.max(_dot((kf * kf).astype(BF16), select.astype(BF16)),
                              axis=0, keepdims=True)

    lane = lax.broadcasted_iota(jnp.int32, (1, LANES), 1)
    qf = qt.astype(F32)
    exit_level = []
    for hd in range(2):
        q_sq = jnp.sum(jnp.square(qf[hd * hd_dim:(hd + 1) * hd_dim, :]), axis=0, keepdims=True)
        k_sq = jnp.max(jnp.where(lane == hd, kn_ref[...], 0.0), axis=1, keepdims=True)
        exit_level.append(1.02 * jnp.sqrt(q_sq * k_sq) + UNDERFLOW_LOG2)

    acc_ref[...] = jnp.zeros_like(acc_ref)
    r_ref[...] = jnp.zeros_like(r_ref)
    first = i * n_sub
    score((first + n_sub - 1, 0, 0))
    for d in reversed(range(n_sub)):
        block(first + d, d < n_sub - 1, jnp.maximum(first + d - 1, 0), d)

    def more(carry):
        g, live = carry
        return (g >= 0) & (live > 0)

    def body(carry):
        g, _ = carry
        block(g, True, jnp.maximum(g - 1, 0), None)
        dead = ((jnp.min(r_ref[0] - exit_level[0]) > 0.0)
                & (jnp.min(r_ref[1] - exit_level[1]) > 0.0))
        return g - 1, jnp.where(dead, 0, 1).astype(jnp.int32)

    g_end, _ = lax.while_loop(more, body, (first - 1, jnp.int32(1)))
    weigh((g_end + 1, 1, 1))
    out_t = jnp.concatenate([acc_ref[0], acc_ref[1]], axis=0)
    o_ref[0] = out_t.T.astype(BF16)


def _sb_attention(qt, k, vt, *, tq):
    b, w, s = qt.shape
    kb = vt.shape[3]
    return pl.pallas_call(
        functools.partial(_sb_attn_kernel, tq=tq, kb=kb),
        grid=(b, w // LANES, s // tq),
        in_specs=[
            pl.BlockSpec((1, LANES, tq), lambda bi, p, i: (bi, p, i)),
            pl.BlockSpec((1, s, LANES), lambda bi, p, i: (bi, 0, p)),
            pl.BlockSpec((1, s // kb, LANES, kb), lambda bi, p, i: (bi, 0, p, 0)),
        ],
        out_specs=pl.BlockSpec((1, tq, LANES), lambda bi, p, i: (bi, i, p)),
        out_shape=jax.ShapeDtypeStruct((b, s, w), BF16),
        scratch_shapes=[pltpu.VMEM((2, SB_HEAD_DIM, tq), F32), pltpu.VMEM((2, 1, tq), F32),
                        *[pltpu.VMEM((kb, tq), F32) for _ in range(6)],
                        pltpu.VMEM((2, 1, tq), F32), pltpu.VMEM((1, LANES), F32)],
        compiler_params=_params("parallel", "parallel", "arbitrary"),
        name="sb_attn",
    )(qt, k, vt)


def _lane_tile(t, width):
    return jnp.concatenate([t] * (width // t.shape[1]), axis=1)


def _rms_rows(xt, gain):
    ms = jnp.sum(xt * xt, axis=0, keepdims=True) * (1.0 / xt.shape[0])
    return xt * lax.rsqrt(ms + EPS) * _lane_tile(gain, xt.shape[1])


def _mla_prep_kernel(x_ref, pos_ref, g_ref, win_ref, qlg_ref, kvlg_ref, wuqt_ref, wukt_ref,
                     wuvt_ref, qg_ref, kg_ref, freq_ref, qt_ref, k_ref, vt_ref):
    tm = x_ref.shape[1]
    kb = vt_ref.shape[3]
    lat = MLA_Q_LORA + MLA_KV_LORA
    half = MLA_ROPE // 2
    h = _rms(x_ref[0], g_ref[...]).astype(BF16)
    ct = _dot(h, win_ref[...]).T
    cqn = _rms_rows(ct[0:MLA_Q_LORA], qlg_ref[...]).astype(BF16)
    ckvn = _rms_rows(ct[MLA_Q_LORA:lat], kvlg_ref[...]).astype(BF16)
    k_r = ct[lat:lat + MLA_ROPE]

    group = 4

    def up_project(gi):
        rows = slice(gi * group * MLA_QK, (gi + 1) * group * MLA_QK)
        krows = slice(gi * group * MLA_NOPE, (gi + 1) * group * MLA_NOPE)
        return _dot(wuqt_ref[rows, :], cqn), _dot(wukt_ref[krows, :], ckvn)

    angle = _lane_tile(freq_ref[...], tm) * pos_ref[0].astype(F32)
    cos = jnp.cos(angle)
    sin = jnp.sin(angle)

    def rope(t):
        t1, t2 = t[0:half], t[half:]
        return jnp.concatenate([t1 * cos - t2 * sin, t1 * sin + t2 * cos], axis=0)

    qg = _lane_tile(qg_ref[...], tm)
    kg = _lane_tile(kg_ref[...], tm)
    zeros = jnp.zeros((LANES - MLA_QK, tm), F32)
    inv_n = 1.0 / MLA_QK
    kr_sq = jnp.sum(k_r * k_r, axis=0, keepdims=True)
    kr_roped = rope(k_r * kg[MLA_NOPE:])
    n_groups = MLA_HEADS // group
    pending = up_project(0)
    for gi in range(n_groups):
        q_grp, kn_grp = pending
        if gi + 1 < n_groups:
            pending = up_project(gi + 1)
        else:
            vt = _dot(wuvt_ref[...], ckvn)
            for c in range(tm // kb):
                vt_ref[0, c] = vt[:, c * kb:(c + 1) * kb].astype(BF16)
        for sub in range(group):
            hd = gi * group + sub
            qh = q_grp[sub * MLA_QK:(sub + 1) * MLA_QK]
            r = lax.rsqrt(jnp.sum(qh * qh, axis=0, keepdims=True) * inv_n + EPS)
            qn = qh * r * qg
            qt_ref[0, hd] = jnp.concatenate([qn[0:MLA_NOPE], rope(qn[MLA_NOPE:]), zeros],
                                            axis=0).astype(BF16)
            kn = kn_grp[sub * MLA_NOPE:(sub + 1) * MLA_NOPE]
            r = lax.rsqrt((jnp.sum(kn * kn, axis=0, keepdims=True) + kr_sq) * inv_n + EPS)
            kt = jnp.concatenate([kn * r * kg[0:MLA_NOPE], kr_roped * r, zeros], axis=0)
            k_ref[0, hd] = kt.T.astype(BF16)


def _mla_prep(x, positions, gain, w_in, qlg, kvlg, wuqt, wukt, wuvt, qg, kg, freq, *, tm, kb):
    b, s, d = x.shape
    vw = MLA_HEADS * MLA_V
    const2 = lambda bi, i: (0, 0)
    full = lambda a: pl.BlockSpec(a.shape, const2)
    small = [gain.reshape(1, -1), w_in, qlg, kvlg, wuqt, wukt, wuvt, qg, kg, freq]
    return pl.pallas_call(
        _mla_prep_kernel,
        grid=(b, s // tm),
        in_specs=[
            pl.BlockSpec((1, tm, d), lambda bi, i: (bi, i, 0)),
            pl.BlockSpec((1, 1, tm), lambda bi, i: (bi, 0, i)),
            *[full(a) for a in small],
        ],
        out_specs=[pl.BlockSpec((1, MLA_HEADS, LANES, tm), lambda bi, i: (bi, 0, 0, i)),
                   pl.BlockSpec((1, MLA_HEADS, tm, LANES), lambda bi, i: (bi, 0, i, 0)),
                   pl.BlockSpec((1, tm // kb, vw, kb), lambda bi, i: (bi, i, 0, 0))],
        out_shape=[jax.ShapeDtypeStruct((b, MLA_HEADS, LANES, s), BF16),
                   jax.ShapeDtypeStruct((b, MLA_HEADS, s, LANES), BF16),
                   jax.ShapeDtypeStruct((b, s // kb, vw, kb), BF16)],
        compiler_params=_params("parallel", "parallel"),
        name="mla_prep",
    )(x, positions.reshape(b, 1, s), *small)


def _mla_attn_kernel(qt_ref, k_ref, vt_ref, o_ref, acc_ref, m_ref, s0_ref, s1_ref, bm_ref,
                     *, tq):
    i = pl.program_id(2)
    n_heads = qt_ref.shape[1]
    s_refs = (s0_ref, s1_ref)
    key = lax.broadcasted_iota(jnp.int32, (tq, tq), 0)
    query = lax.broadcasted_iota(jnp.int32, (tq, tq), 1)
    causal = key <= query

    def produce(g, hd, masked):
        start = pl.multiple_of(g * tq, tq)
        sc = _dot(k_ref[0, hd, pl.ds(start, tq), :], qt_ref[0, hd])
        if masked:
            sc = jnp.where(causal, sc, MASKED)
        s_refs[hd % 2][...] = sc
        bm_ref[hd % 2] = jnp.max(sc, axis=0, keepdims=True)

    ones_rows = jnp.ones((SUM_ROWS, tq), BF16)

    def consume(g, hd):
        m_old = m_ref[hd]
        m_new = jnp.maximum(m_old, bm_ref[hd % 2])
        alpha = jnp.exp2(m_old - m_new)
        p = jnp.exp2(s_refs[hd % 2][...] - m_new)
        vt = jnp.concatenate([vt_ref[0, g, hd * MLA_V:(hd + 1) * MLA_V, :], ones_rows], axis=0)
        acc_ref[hd] = alpha * acc_ref[hd] + _dot(vt, p.astype(BF16))
        m_ref[hd] = m_new

    def visit(g, g_next, masked):
        for hd in range(n_heads):
            if hd + 1 < n_heads:
                produce(g, hd + 1, masked)
            else:
                produce(g_next, 0, False)
            consume(g, hd)

    acc_ref[...] = jnp.zeros_like(acc_ref)
    m_ref[...] = jnp.full_like(m_ref, MASKED)
    produce(i, 0, True)
    visit(i, 0, True)

    def body(it, carry):
        visit(2 * it, 2 * it + 1, False)
        visit(2 * it + 1, jnp.minimum(2 * it + 2, i - 1), False)
        return carry

    lax.fori_loop(0, i // 2, body, 0)

    @pl.when(i % 2 == 1)
    def _():
        visit(i - 1, i - 1, False)

    out_t = jnp.concatenate([acc_ref[hd, 0:MLA_V, :] / acc_ref[hd, MLA_V:MLA_V + 1, :]
                             for hd in range(n_heads)], axis=0)
    o_ref[0] = out_t.T.astype(BF16)


def _mla_attention(qt, k, vt, *, tq, heads_per_step):
    b, heads, _, s = qt.shape
    kb = vt.shape[3]
    nh = heads_per_step
    assert kb == tq and nh % 2 == 0
    vw = nh * MLA_V
    return pl.pallas_call(
        functools.partial(_mla_attn_kernel, tq=tq),
        grid=(b, heads // nh, s // tq),
        in_specs=[
            pl.BlockSpec((1, nh, LANES, tq), lambda bi, p, i: (bi, p, 0, i)),
            pl.BlockSpec((1, nh, s, LANES), lambda bi, p, i: (bi, p, 0, 0)),
            pl.BlockSpec((1, s // kb, vw, kb), lambda bi, p, i: (bi, 0, p, 0)),
        ],
        out_specs=pl.BlockSpec((1, tq, vw), lambda bi, p, i: (bi, i, p)),
        out_shape=jax.ShapeDtypeStruct((b, s, heads * MLA_V), BF16),
        scratch_shapes=[pltpu.VMEM((nh, MLA_V + SUM_ROWS, tq), F32),
                        pltpu.VMEM((nh, 1, tq), F32), pltpu.VMEM((kb, tq), F32),
                        pltpu.VMEM((kb, tq), F32), pltpu.VMEM((2, 1, tq), F32)],
        compiler_params=_params("parallel", "parallel", "arbitrary"),
        name="mla_attn",
    )(qt, k, vt)


def _mem_kv_kernel(mem_ref, g_ref, wkv_ref, kg_ref, k_ref, v_ref):
    hm = _rms(mem_ref[0], g_ref[...]).astype(BF16)
    hd_dim = kg_ref.shape[1]
    for hd in range(MEM_HEADS):
        kcols = slice(2 * hd * hd_dim, (2 * hd + 1) * hd_dim)
        vcols = slice((2 * hd + 1) * hd_dim, (2 * hd + 2) * hd_dim)
        out = slice(hd * hd_dim, (hd + 1) * hd_dim)
        k_ref[0, :, out] = _rms(_dot(hm, wkv_ref[:, kcols]), kg_ref[...]).astype(BF16)
        v_ref[0, :, out] = _dot(hm, wkv_ref[:, vcols]).astype(BF16)


def _mem_kv(mem, gain, wkv, k_gain):
    b, m, d = mem.shape
    out = jax.ShapeDtypeStruct((b, m, d), BF16)
    spec = pl.BlockSpec((1, m, d), lambda bi: (bi, 0, 0))
    return pl.pallas_call(
        _mem_kv_kernel,
        grid=(b,),
        in_specs=[spec,
                  pl.BlockSpec((1, d), lambda bi: (0, 0)),
                  pl.BlockSpec(wkv.shape, lambda bi: (0, 0)),
                  pl.BlockSpec((1, k_gain.shape[0]), lambda bi: (0, 0))],
        out_specs=[spec, spec],
        out_shape=[out, out],
        compiler_params=_params("parallel"),
        name="mem_kv",
    )(mem, gain.reshape(1, d), wkv, k_gain.reshape(1, -1))


def _mix_xattn_kernel(*refs, n_act):
    x_ref = refs[0]
    act_refs = refs[1:1 + n_act]
    w_ref, g_ref, wq_ref, qg_ref, k_ref, v_ref, wo_ref, o_ref = refs[1 + n_act:]
    act = jnp.concatenate([a_ref[0] for a_ref in act_refs], axis=-1)
    x1 = x_ref[0] + _dot(act, w_ref[...])
    h = _rms(x1, g_ref[...]).astype(BF16)
    hd_dim = qg_ref.shape[1]
    cols = [slice(hd * hd_dim, (hd + 1) * hd_dim) for hd in range(MEM_HEADS)]
    q = [_dot(h, wq_ref[:, c]) for c in cols]
    qn = [(_rms(q_h, qg_ref[...]) * (hd_dim ** -0.5)).astype(BF16) for q_h in q]
    sc = [_dot_nt(qn_h, k_ref[0, :, c]) for qn_h, c in zip(qn, cols)]
    p = [jnp.exp(s_h - jnp.max(s_h, axis=-1, keepdims=True)) for s_h in sc]
    o = [(_dot(p_h.astype(BF16), v_ref[0, :, c]) / jnp.sum(p_h, axis=-1, keepdims=True)
          ).astype(BF16) for p_h, c in zip(p, cols)]
    o_ref[0] = x1 + _dot(jnp.concatenate(o, axis=-1), wo_ref[...])


def _mix_xattn(x, acts, w_mix, gain, wq, q_gain, mem_k, mem_v, wo, *, tm):
    b, s, d = x.shape
    m = mem_k.shape[1]
    const2 = lambda bi, i: (0, 0)
    row = lambda width: pl.BlockSpec((1, tm, width), lambda bi, i: (bi, i, 0))
    mem_spec = pl.BlockSpec((1, m, d), lambda bi, i: (bi, 0, 0))
    return pl.pallas_call(
        functools.partial(_mix_xattn_kernel, n_act=len(acts)),
        grid=(b, s // tm),
        in_specs=[
            row(d),
            *[row(a.shape[-1]) for a in acts],
            pl.BlockSpec(w_mix.shape, const2),
            pl.BlockSpec((1, d), const2),
            pl.BlockSpec(wq.shape, const2),
            pl.BlockSpec((1, q_gain.shape[0]), const2),
            mem_spec, mem_spec,
            pl.BlockSpec(wo.shape, const2),
        ],
        out_specs=row(d),
        out_shape=jax.ShapeDtypeStruct((b, s, d), F32),
        compiler_params=_params("parallel", "parallel"),
        name="mix_xattn",
    )(x, *acts, w_mix, gain.reshape(1, d), wq, q_gain.reshape(1, -1), mem_k, mem_v, wo)


def _lane_bcast(vec):
    return jnp.broadcast_to(vec[:, None], (vec.shape[0], LANES))


def _mla_weights(w_in, w_uq, w_ukv, q_lora_gain, kv_lora_gain, q_gain, k_gain):
    lat = MLA_Q_LORA + MLA_KV_LORA
    w_in_ext = jnp.pad(w_in, ((0, 0), (0, lat + LANES - w_in.shape[1])))
    wukv = w_ukv.reshape(MLA_KV_LORA, MLA_HEADS, MLA_NOPE + MLA_V)
    wukt = wukv[..., :MLA_NOPE].reshape(MLA_KV_LORA, -1).T
    wuvt = wukv[..., MLA_NOPE:].reshape(MLA_KV_LORA, -1).T
    half = MLA_ROPE // 2
    inv_freq = ROPE_THETA ** (-jnp.arange(half, dtype=F32) / half)
    bf = lambda a: a.astype(BF16)
    return (bf(w_in_ext), _lane_bcast(q_lora_gain), _lane_bcast(kv_lora_gain),
            bf(w_uq.T), bf(wukt), bf(wuvt),
            _lane_bcast(q_gain * (MLA_QK ** -0.5 * LOG2E)), _lane_bcast(k_gain),
            _lane_bcast(inv_freq))


def _tile(n, pref):
    return pref if n % pref == 0 else n


def kernel(x, mem, positions, ffn_pre_norm, ffn_pre_w_gu, ffn_pre_w_down, mix_norm, sbg_w_in, sgu_ln_gain, sgu_ln_bias, sgu_w, sgu_b, sbg_w_out, mla_w_in, mla_q_lora_gain, mla_kv_lora_gain, mla_w_uq, mla_w_ukv, mla_q_gain, mla_k_gain, mla_w_out, xmem_norm, xmem_mem_norm, xmem_wq, xmem_wkv, xmem_q_gain, xmem_k_gain, xmem_wo, ffn_post_norm, ffn_post_w_gu, ffn_post_w_down):
    b, s, d = x.shape
    depth = ffn_pre_norm.shape[0]
    d_ff = ffn_pre_w_down.shape[1]
    t = b * s
    ffn_tm = _tile(t, 512)
    ffn_tf = _tile(d_ff, 256)
    row_tm = _tile(s, 512)
    mla_tq = _tile(s, 512)
    sb_tq = _tile(s, 512)
    sb_kb = _tile(sb_tq, 256)
    bf = lambda a: a.astype(BF16)

    for layer in range(depth):
        x = _ffn(x.reshape(t, d), ffn_pre_norm[layer], ffn_pre_w_gu, ffn_pre_w_down, layer,
                 tm=ffn_tm, tf=ffn_tf).reshape(b, s, d)
        if layer % 2 == 0:
            e = layer // 2
            bias_full = jnp.repeat(sgu_b[e].T, SG_GROUP_DIM, axis=1)
            qt, k, vt, o_sg = _even_prep(
                x, mix_norm[layer], bf(sbg_w_in[e]), sgu_ln_gain[e],
                sgu_ln_bias[e], sgu_w[e], bias_full, tm=row_tm, kb=sb_kb)
            acts = (_sb_attention(qt, k, vt, tq=sb_tq), o_sg)
            w_mix = bf(sbg_w_out[e])
        else:
            o = layer // 2
            mla_consts = _mla_weights(
                mla_w_in[o], mla_w_uq[o], mla_w_ukv[o], mla_q_lora_gain[o],
                mla_kv_lora_gain[o], mla_q_gain[o], mla_k_gain[o])
            qt, k, vt = _mla_prep(x, positions, mix_norm[layer], *mla_consts,
                                  tm=row_tm, kb=mla_tq)
            acts = (_mla_attention(qt, k, vt, tq=mla_tq, heads_per_step=4),)
            w_mix = bf(mla_w_out[o])
        mem_k, mem_v = _mem_kv(mem, xmem_mem_norm[layer], bf(xmem_wkv[layer]),
                               xmem_k_gain[layer])
        x = _mix_xattn(x, acts, w_mix, xmem_norm[layer], bf(xmem_wq[layer]), xmem_q_gain[layer],
                       mem_k, mem_v, bf(xmem_wo[layer]), tm=row_tm)
        x = _ffn(x.reshape(t, d), ffn_post_norm[layer], ffn_post_w_gu, ffn_post_w_down, layer,
                 tm=ffn_tm, tf=ffn_tf).reshape(b, s, d)
    return x
```

```python
import functools

import jax
import jax.numpy as jnp
from jax import lax
from jax.experimental import pallas as pl
from jax.experimental.pallas import tpu as pltpu

EPS = 1e-6
ROPE_THETA = 10000.0
LANES = 128
VMEM_LIMIT_BYTES = 56 * 1024 * 1024

SB_HEADS, SB_HEAD_DIM = 8, 64
SB_WIDTH = SB_HEADS * SB_HEAD_DIM
SG_GROUPS, SG_GROUP_DIM, SG_CHUNK = 8, 64, 128
SG_WIDTH = SG_GROUPS * SG_GROUP_DIM
MLA_HEADS, MLA_NOPE, MLA_ROPE, MLA_V = 16, 64, 32, 64
MLA_QK = MLA_NOPE + MLA_ROPE
MLA_Q_LORA, MLA_KV_LORA = 512, 256
MEM_HEADS = 4

BF16 = jnp.bfloat16
F32 = jnp.float32
LOG2E = 1.4426950408889634
MASKED = -1e30
SUM_ROWS = 16
FIXED_SHIFT_MAX = 60.0
UNDERFLOW_LOG2 = 160.0


def _params(*semantics):
    return pltpu.CompilerParams(dimension_semantics=semantics,
                                vmem_limit_bytes=VMEM_LIMIT_BYTES)


def _dot(a, b):
    return jnp.dot(a, b, preferred_element_type=F32)


def _dot_nt(a, b):
    return lax.dot_general(a, b, (((1,), (1,)), ((), ())), preferred_element_type=F32)


def _rms(x, gain, n=None):
    n = x.shape[-1] if n is None else n
    ms = jnp.sum(x * x, axis=-1, keepdims=True) * (1.0 / n)
    return x * lax.rsqrt(ms + EPS) * gain


def _ffn_kernel(x_ref, g_ref, wgu_ref, wd_ref, o_ref, *, tf):
    d_ff = wd_ref.shape[0]
    x = x_ref[...]
    h = _rms(x, g_ref[...]).astype(BF16)
    acc = None
    for c in range(d_ff // tf):
        cols = slice(c * tf, (c + 1) * tf)
        gate = _dot(h, wgu_ref[:, cols].astype(BF16))
        up = _dot(h, wgu_ref[:, d_ff + c * tf:d_ff + (c + 1) * tf].astype(BF16))
        act = (gate * jax.nn.sigmoid(gate) * up).astype(BF16)
        part = _dot(act, wd_ref[cols, :].astype(BF16))
        acc = part if acc is None else acc + part
    o_ref[...] = x + 0.5 * acc


def _ffn(x2, gain, w_gu, w_down, layer, *, tm, tf):
    t, d = x2.shape
    d_ff = w_down.shape[1]
    resident = pl.Buffered(1)
    return pl.pallas_call(
        functools.partial(_ffn_kernel, tf=tf),
        grid=(t // tm,),
        in_specs=[
            pl.BlockSpec((tm, d), lambda i: (i, 0)),
            pl.BlockSpec((1, d), lambda i: (0, 0)),
            pl.BlockSpec((None, d, 2 * d_ff), lambda i: (layer, 0, 0), pipeline_mode=resident),
            pl.BlockSpec((None, d_ff, d), lambda i: (layer, 0, 0), pipeline_mode=resident),
        ],
        out_specs=pl.BlockSpec((tm, d), lambda i: (i, 0)),
        out_shape=jax.ShapeDtypeStruct((t, d), F32),
        compiler_params=_params("parallel"),
        name="ffn",
    )(x2, gain.reshape(1, d), w_gu, w_down)


def _gelu_tanh(x):
    c = 0.7978845608028654
    return 0.5 * x * (1.0 + jnp.tanh(c * (x + 0.044715 * (x * x * x))))


def _even_prep_kernel(x_ref, g_ref, win_ref, lng_ref, lnb_ref, sw_ref, sb_ref,
                      qt_ref, k_ref, vt_ref, osg_ref):
    tm = x_ref.shape[1]
    kb = vt_ref.shape[3]
    w = SB_WIDTH
    h = _rms(x_ref[0], g_ref[...]).astype(BF16)
    osg_ref = osg_ref.at[0]
    u_raw = _dot(h, win_ref[:, 3 * w:3 * w + SG_WIDTH])
    g_raw = _dot(h, win_ref[:, 3 * w + SG_WIDTH:3 * w + 2 * SG_WIDTH])
    qt_ref[0] = (_dot(h, win_ref[:, 0:w]) * (SB_HEAD_DIM ** -0.5 * LOG2E)).T.astype(BF16)
    u = _gelu_tanh(u_raw)
    k_ref[0] = _dot(h, win_ref[:, w:2 * w]).astype(BF16)
    g = _gelu_tanh(g_raw)
    mu = jnp.mean(g, axis=-1, keepdims=True)
    gc = g - mu
    var = jnp.mean(gc * gc, axis=-1, keepdims=True)
    gn = (gc * lax.rsqrt(var + EPS) * lng_ref[...] + lnb_ref[...]).astype(BF16)
    v = _dot(h, win_ref[:, 2 * w:3 * w])
    for c in range(tm // kb):
        vt_ref[0, c] = v[c * kb:(c + 1) * kb, :].T.astype(BF16)

    row = lax.broadcasted_iota(jnp.int32, (SG_CHUNK, SG_CHUNK), 0)
    col = lax.broadcasted_iota(jnp.int32, (SG_CHUNK, SG_CHUNK), 1)
    tri = col <= row
    first_group = lax.broadcasted_iota(jnp.int32, (SG_CHUNK, LANES), 1) < SG_GROUP_DIM
    for p in range(SG_GROUPS // 2):
        lanes = slice(p * LANES, (p + 1) * LANES)
        w0 = jnp.where(tri, sw_ref[2 * p], 0.0).astype(BF16)
        w1 = jnp.where(tri, sw_ref[2 * p + 1], 0.0).astype(BF16)
        bias = sb_ref[:, lanes]
        for c in range(tm // SG_CHUNK):
            rows = slice(c * SG_CHUNK, (c + 1) * SG_CHUNK)
            gp = gn[rows, lanes]
            mixed = jnp.where(first_group, _dot(w0, gp), _dot(w1, gp)) + bias
            osg_ref[rows, lanes] = (u[rows, lanes] * mixed).astype(BF16)


def _even_prep(x, gain, w_in, ln_g, ln_b, sgu_w, sgu_bias_full, *, tm, kb):
    b, s, d = x.shape
    n_in = w_in.shape[1]
    w = SB_WIDTH
    const2 = lambda bi, i: (0, 0)
    row_out = jax.ShapeDtypeStruct((b, s, w), BF16)
    row_spec = pl.BlockSpec((1, tm, w), lambda bi, i: (bi, i, 0))
    return pl.pallas_call(
        _even_prep_kernel,
        grid=(b, s // tm),
        in_specs=[
            pl.BlockSpec((1, tm, d), lambda bi, i: (bi, i, 0)),
            pl.BlockSpec((1, d), const2),
            pl.BlockSpec((d, n_in), const2),
            pl.BlockSpec((1, SG_WIDTH), const2),
            pl.BlockSpec((1, SG_WIDTH), const2),
            pl.BlockSpec((SG_GROUPS, SG_CHUNK, SG_CHUNK), lambda bi, i: (0, 0, 0)),
            pl.BlockSpec((SG_CHUNK, SG_WIDTH), const2),
        ],
        out_specs=[pl.BlockSpec((1, w, tm), lambda bi, i: (bi, 0, i)),
                   row_spec,
                   pl.BlockSpec((1, tm // kb, w, kb), lambda bi, i: (bi, i, 0, 0)),
                   row_spec],
        out_shape=[jax.ShapeDtypeStruct((b, w, s), BF16), row_out,
                   jax.ShapeDtypeStruct((b, s // kb, w, kb), BF16), row_out],
        compiler_params=_params("parallel", "parallel"),
        name="even_prep",
    )(x, gain.reshape(1, d), w_in, ln_g.reshape(1, -1), ln_b.reshape(1, -1),
      sgu_w, sgu_bias_full)


def _sb_attn_kernel(qt_ref, k_ref, vt_ref, o_ref, acc_ref, r_ref, z0_ref, z1_ref, zc0_ref,
                    zc1_ref, t0_ref, t1_ref, bs_ref, kn_ref, *, tq, kb):
    i = pl.program_id(2)
    n_sub = tq // kb
    hd_dim = SB_HEAD_DIM
    z_refs, zc_refs, t_refs = (z0_ref, z1_ref), (zc0_ref, zc1_ref), (t0_ref, t1_ref)
    first_head = lax.broadcasted_iota(jnp.int32, (LANES, 1), 0) < hd_dim
    qt = qt_ref[0]
    zero = jnp.zeros_like(qt)
    qt_heads = (jnp.where(first_head, qt, zero), jnp.where(first_head, zero, qt))
    key = lax.broadcasted_iota(jnp.int32, (kb, tq), 0)
    query = lax.broadcasted_iota(jnp.int32, (kb, tq), 1)
    srow = lax.broadcasted_iota(jnp.int32, (kb, kb), 0)
    scol = lax.broadcasted_iota(jnp.int32, (kb, kb), 1)
    suffix = (scol >= srow).astype(BF16)

    def score(item):
        g, hd, slot = item
        start = pl.multiple_of(g * kb, kb)
        z_refs[slot][...] = _dot(k_ref[0, pl.ds(start, kb), :], qt_heads[hd])

    def stay(item, diag_sub):
        g, hd, slot = item
        z = z_refs[slot][...]
        sp = jnp.maximum(z, 0.0) + jnp.log2(1.0 + jnp.exp2(-jnp.abs(z)))
        if diag_sub is not None:
            visible = (key + diag_sub * kb) < query
            sp = jnp.where(visible, sp, 0.0)
            z = jnp.where(visible, z, MASKED)
        zc_refs[slot][...] = z
        tail = _dot(suffix, sp.astype(BF16))
        t_refs[slot][...] = tail
        bs_ref[slot] = tail[0:1, :]

    def weigh(item):
        g, hd, slot = item
        wgt = jnp.exp2(zc_refs[slot][...] - t_refs[slot][...] - r_ref[hd])
        vt = vt_ref[0, g, hd * hd_dim:(hd + 1) * hd_dim, :]
        acc_ref[hd] += _dot(vt, wgt.astype(BF16))
        r_ref[hd] += bs_ref[slot]

    def block(g, has_prev, g_next, diag_sub):
        score((g, 1, 1))
        if has_prev:
            weigh((g + 1, 1, 1))
        stay((g, 0, 0), diag_sub)
        score((g_next, 0, 0))
        weigh((g, 0, 0))
        stay((g, 1, 1), diag_sub)

    @pl.when(i == 0)
    def _():
        kf = k_ref[0].astype(F32)
        dim = lax.broadcasted_iota(jnp.int32, (LANES, LANES), 0)
        head = lax.broadcasted_iota(jnp.int32, (LANES, LANES), 1)
        select = ((dim < hd_dim) == (head == 0)) & (head < 2)
        kn_ref[...] = jnp.max(_dot((kf * kf).astype(BF16), select.astype(BF16)),
                              axis=0, keepdims=True)

    lane = lax.broadcasted_iota(jnp.int32, (1, LANES), 1)
    qf = qt.astype(F32)
    exit_level = []
    for hd in range(2):
        q_sq = jnp.sum(jnp.square(qf[hd * hd_dim:(hd + 1) * hd_dim, :]), axis=0, keepdims=True)
        k_sq = jnp.max(jnp.where(lane == hd, kn_ref[...], 0.0), axis=1, keepdims=True)
        exit_level.append(1.02 * jnp.sqrt(q_sq * k_sq) + UNDERFLOW_LOG2)

    acc_ref[...] = jnp.zeros_like(acc_ref)
    r_ref[...] = jnp.zeros_like(r_ref)
    first = i * n_sub
    score((first + n_sub - 1, 0, 0))
    for d in reversed(range(n_sub)):
        block(first + d, d < n_sub - 1, jnp.maximum(first + d - 1, 0), d)

    def more(carry):
        g, live = carry
        return (g >= 0) & (live > 0)

    def body(carry):
        g, _ = carry
        block(g, True, jnp.maximum(g - 1, 0), None)
        dead = ((jnp.min(r_ref[0] - exit_level[0]) > 0.0)
                & (jnp.min(r_ref[1] - exit_level[1]) > 0.0))
        return g - 1, jnp.where(dead, 0, 1).astype(jnp.int32)

    g_end, _ = lax.while_loop(more, body, (first - 1, jnp.int32(1)))
    weigh((g_end + 1, 1, 1))
    out_t = jnp.concatenate([acc_ref[0], acc_ref[1]], axis=0)
    o_ref[0] = out_t.T.astype(BF16)


def _sb_attention(qt, k, vt, *, tq):
    b, w, s = qt.shape
    kb = vt.shape[3]
    return pl.pallas_call(
        functools.partial(_sb_attn_kernel, tq=tq, kb=kb),
        grid=(b, w // LANES, s // tq),
        in_specs=[
            pl.BlockSpec((1, LANES, tq), lambda bi, p, i: (bi, p, i)),
            pl.BlockSpec((1, s, LANES), lambda bi, p, i: (bi, 0, p)),
            pl.BlockSpec((1, s // kb, LANES, kb), lambda bi, p, i: (bi, 0, p, 0)),
        ],
        out_specs=pl.BlockSpec((1, tq, LANES), lambda bi, p, i: (bi, i, p)),
        out_shape=jax.ShapeDtypeStruct((b, s, w), BF16),
        scratch_shapes=[pltpu.VMEM((2, SB_HEAD_DIM, tq), F32), pltpu.VMEM((2, 1, tq), F32),
                        *[pltpu.VMEM((kb, tq), F32) for _ in range(6)],
                        pltpu.VMEM((2, 1, tq), F32), pltpu.VMEM((1, LANES), F32)],
        compiler_params=_params("parallel", "parallel", "arbitrary"),
        name="sb_attn",
    )(qt, k, vt)


def _lane_tile(t, width):
    return jnp.concatenate([t] * (width // t.shape[1]), axis=1)


def _rms_rows(xt, gain):
    ms = jnp.sum(xt * xt, axis=0, keepdims=True) * (1.0 / xt.shape[0])
    return xt * lax.rsqrt(ms + EPS) * _lane_tile(gain, xt.shape[1])


def _mla_prep_kernel(x_ref, pos_ref, g_ref, win_ref, qlg_ref, kvlg_ref, wuqt_ref, wukt_ref,
                     wuvt_ref, qg_ref, kg_ref, freq_ref, qt_ref, k_ref, vt_ref):
    tm = x_ref.shape[1]
    kb = vt_ref.shape[3]
    lat = MLA_Q_LORA + MLA_KV_LORA
    half = MLA_ROPE // 2
    h = _rms(x_ref[0], g_ref[...]).astype(BF16)
    ct = _dot(h, win_ref[...]).T
    cqn = _rms_rows(ct[0:MLA_Q_LORA], qlg_ref[...]).astype(BF16)
    ckvn = _rms_rows(ct[MLA_Q_LORA:lat], kvlg_ref[...]).astype(BF16)
    k_r = ct[lat:lat + MLA_ROPE]

    group = 4

    def up_project(gi):
        rows = slice(gi * group * MLA_QK, (gi + 1) * group * MLA_QK)
        krows = slice(gi * group * MLA_NOPE, (gi + 1) * group * MLA_NOPE)
        return _dot(wuqt_ref[rows, :], cqn), _dot(wukt_ref[krows, :], ckvn)

    angle = _lane_tile(freq_ref[...], tm) * pos_ref[0].astype(F32)
    cos = jnp.cos(angle)
    sin = jnp.sin(angle)

    def rope(t):
        t1, t2 = t[0:half], t[half:]
        return jnp.concatenate([t1 * cos - t2 * sin, t1 * sin + t2 * cos], axis=0)

    qg = _lane_tile(qg_ref[...], tm)
    kg = _lane_tile(kg_ref[...], tm)
    zeros = jnp.zeros((LANES - MLA_QK, tm), F32)
    inv_n = 1.0 / MLA_QK
    kr_sq = jnp.sum(k_r * k_r, axis=0, keepdims=True)
    kr_roped = rope(k_r * kg[MLA_NOPE:])
    n_groups = MLA_HEADS // group
    pending = up_project(0)
    for gi in range(n_groups):
        q_grp, kn_grp = pending
        if gi + 1 < n_groups:
            pending = up_project(gi + 1)
        else:
            vt = _dot(wuvt_ref[...], ckvn)
            for c in range(tm // kb):
                vt_ref[0, c] = vt[:, c * kb:(c + 1) * kb].astype(BF16)
        for sub in range(group):
            hd = gi * group + sub
            qh = q_grp[sub * MLA_QK:(sub + 1) * MLA_QK]
            r = lax.rsqrt(jnp.sum(qh * qh, axis=0, keepdims=True) * inv_n + EPS)
            qn = qh * r * qg
            qt_ref[0, hd] = jnp.concatenate([qn[0:MLA_NOPE], rope(qn[MLA_NOPE:]), zeros],
                                            axis=0).astype(BF16)
            kn = kn_grp[sub * MLA_NOPE:(sub + 1) * MLA_NOPE]
            r = lax.rsqrt((jnp.sum(kn * kn, axis=0, keepdims=True) + kr_sq) * inv_n + EPS)
            kt = jnp.concatenate([kn * r * kg[0:MLA_NOPE], kr_roped * r, zeros], axis=0)
            k_ref[0, hd] = kt.T.astype(BF16)


def _mla_prep(x, positions, gain, w_in, qlg, kvlg, wuqt, wukt, wuvt, qg, kg, freq, *, tm, kb):
    b, s, d = x.shape
    vw = MLA_HEADS * MLA_V
    const2 = lambda bi, i: (0, 0)
    full = lambda a: pl.BlockSpec(a.shape, const2)
    small = [gain.reshape(1, -1), w_in, qlg, kvlg, wuqt, wukt, wuvt, qg, kg, freq]
    return pl.pallas_call(
        _mla_prep_kernel,
        grid=(b, s // tm),
        in_specs=[
            pl.BlockSpec((1, tm, d), lambda bi, i: (bi, i, 0)),
            pl.BlockSpec((1, 1, tm), lambda bi, i: (bi, 0, i)),
            *[full(a) for a in small],
        ],
        out_specs=[pl.BlockSpec((1, MLA_HEADS, LANES, tm), lambda bi, i: (bi, 0, 0, i)),
                   pl.BlockSpec((1, MLA_HEADS, tm, LANES), lambda bi, i: (bi, 0, i, 0)),
                   pl.BlockSpec((1, tm // kb, vw, kb), lambda bi, i: (bi, i, 0, 0))],
        out_shape=[jax.ShapeDtypeStruct((b, MLA_HEADS, LANES, s), BF16),
                   jax.ShapeDtypeStruct((b, MLA_HEADS, s, LANES), BF16),
                   jax.ShapeDtypeStruct((b, s // kb, vw, kb), BF16)],
        compiler_params=_params("parallel", "parallel"),
        name="mla_prep",
    )(x, positions.reshape(b, 1, s), *small)


def _mla_attn_kernel(qt_ref, k_ref, vt_ref, o_ref, acc_ref, m_ref, s0_ref, s1_ref, bm_ref,
                     p0_ref, p1_ref, kn_ref, *, tq):
    i = pl.program_id(2)
    n_heads = qt_ref.shape[1]
    s_refs, p_refs = (s0_ref, s1_ref), (p0_ref, p1_ref)
    key = lax.broadcasted_iota(jnp.int32, (tq, tq), 0)
    query = lax.broadcasted_iota(jnp.int32, (tq, tq), 1)
    causal = key <= query
    ones_rows = jnp.ones((SUM_ROWS, tq), BF16)

    def scores(g, hd, masked):
        start = pl.multiple_of(g * tq, tq)
        sc = _dot(k_ref[0, hd, pl.ds(start, tq), :], qt_ref[0, hd])
        return jnp.where(causal, sc, MASKED) if masked else sc

    def values(g, hd):
        return jnp.concatenate([vt_ref[0, g, hd * MLA_V:(hd + 1) * MLA_V, :], ones_rows], axis=0)

    def sweep(produce, consume):
        def visit(g, g_next, masked):
            for hd in range(n_heads):
                if hd + 1 < n_heads:
                    produce(g, hd + 1, masked)
                else:
                    produce(g_next, 0, False)
                consume(g, hd)

        produce(i, 0, True)
        visit(i, 0, True)

        def body(it, carry):
            visit(2 * it, 2 * it + 1, False)
            visit(2 * it + 1, jnp.minimum(2 * it + 2, i - 1), False)
            return carry

        lax.fori_loop(0, i // 2, body, 0)

        @pl.when(i % 2 == 1)
        def _():
            visit(i - 1, i - 1, False)

    @pl.when(i == 0)
    def _():
        ones = jnp.ones((LANES, LANES), BF16)
        for hd in range(n_heads):
            kf = k_ref[0, hd].astype(F32)
            kn_ref[hd] = jnp.max(_dot((kf * kf).astype(BF16), ones), axis=0, keepdims=True)

    bound = []
    for hd in range(n_heads):
        qf = qt_ref[0, hd].astype(F32)
        q_sq = jnp.sum(qf * qf, axis=0, keepdims=True)
        bound.append(1.02 * jnp.sqrt(q_sq * kn_ref[hd][:, 0:1]))
    largest = functools.reduce(jnp.maximum, [jnp.max(b) for b in bound])
    fixed_shift = largest <= FIXED_SHIFT_MAX

    acc_ref[...] = jnp.zeros_like(acc_ref)

    @pl.when(fixed_shift)
    def _():
        def produce(g, hd, masked):
            p_refs[hd % 2][...] = jnp.exp2(scores(g, hd, masked) - bound[hd]).astype(BF16)

        def consume(g, hd):
            acc_ref[hd] += _dot(values(g, hd), p_refs[hd % 2][...])

        sweep(produce, consume)

    @pl.when(jnp.logical_not(fixed_shift))
    def _():
        m_ref[...] = jnp.full_like(m_ref, MASKED)

        def produce(g, hd, masked):
            sc = scores(g, hd, masked)
            s_refs[hd % 2][...] = sc
            bm_ref[hd % 2] = jnp.max(sc, axis=0, keepdims=True)

        def consume(g, hd):
            m_old = m_ref[hd]
            m_new = jnp.maximum(m_old, bm_ref[hd % 2])
            alpha = jnp.exp2(m_old - m_new)
            p = jnp.exp2(s_refs[hd % 2][...] - m_new)
            acc_ref[hd] = alpha * acc_ref[hd] + _dot(values(g, hd), p.astype(BF16))
            m_ref[hd] = m_new

        sweep(produce, consume)

    out_t = jnp.concatenate([acc_ref[hd, 0:MLA_V, :] / acc_ref[hd, MLA_V:MLA_V + 1, :]
                             for hd in range(n_heads)], axis=0)
    o_ref[0] = out_t.T.astype(BF16)


def _mla_attention(qt, k, vt, *, tq, heads_per_step):
    b, heads, _, s = qt.shape
    kb = vt.shape[3]
    nh = heads_per_step
    assert kb == tq and nh % 2 == 0
    vw = nh * MLA_V
    return pl.pallas_call(
        functools.partial(_mla_attn_kernel, tq=tq),
        grid=(b, heads // nh, s // tq),
        in_specs=[
            pl.BlockSpec((1, nh, LANES, tq), lambda bi, p, i: (bi, p, 0, i)),
            pl.BlockSpec((1, nh, s, LANES), lambda bi, p, i: (bi, p, 0, 0)),
            pl.BlockSpec((1, s // kb, vw, kb), lambda bi, p, i: (bi, 0, p, 0)),
        ],
        out_specs=pl.BlockSpec((1, tq, vw), lambda bi, p, i: (bi, i, p)),
        out_shape=jax.ShapeDtypeStruct((b, s, heads * MLA_V), BF16),
        scratch_shapes=[pltpu.VMEM((nh, MLA_V + SUM_ROWS, tq), F32),
                        pltpu.VMEM((nh, 1, tq), F32), pltpu.VMEM((kb, tq), F32),
                        pltpu.VMEM((kb, tq), F32), pltpu.VMEM((2, 1, tq), F32),
                        pltpu.VMEM((kb, tq), BF16), pltpu.VMEM((kb, tq), BF16),
                        pltpu.VMEM((nh, 1, LANES), F32)],
        compiler_params=_params("parallel", "parallel", "arbitrary"),
        name="mla_attn",
    )(qt, k, vt)


def _mem_kv_kernel(mem_ref, g_ref, wkv_ref, kg_ref, k_ref, v_ref):
    hm = _rms(mem_ref[0], g_ref[...]).astype(BF16)
    hd_dim = kg_ref.shape[1]
    for hd in range(MEM_HEADS):
        kcols = slice(2 * hd * hd_dim, (2 * hd + 1) * hd_dim)
        vcols = slice((2 * hd + 1) * hd_dim, (2 * hd + 2) * hd_dim)
        out = slice(hd * hd_dim, (hd + 1) * hd_dim)
        k_ref[0, :, out] = _rms(_dot(hm, wkv_ref[:, kcols]), kg_ref[...]).astype(BF16)
        v_ref[0, :, out] = _dot(hm, wkv_ref[:, vcols]).astype(BF16)


def _mem_kv(mem, gain, wkv, k_gain):
    b, m, d = mem.shape
    out = jax.ShapeDtypeStruct((b, m, d), BF16)
    spec = pl.BlockSpec((1, m, d), lambda bi: (bi, 0, 0))
    return pl.pallas_call(
        _mem_kv_kernel,
        grid=(b,),
        in_specs=[spec,
                  pl.BlockSpec((1, d), lambda bi: (0, 0)),
                  pl.BlockSpec(wkv.shape, lambda bi: (0, 0)),
                  pl.BlockSpec((1, k_gain.shape[0]), lambda bi: (0, 0))],
        out_specs=[spec, spec],
        out_shape=[out, out],
        compiler_params=_params("parallel"),
        name="mem_kv",
    )(mem, gain.reshape(1, d), wkv, k_gain.reshape(1, -1))


def _mix_xattn_kernel(*refs, n_act):
    x_ref = refs[0]
    act_refs = refs[1:1 + n_act]
    w_ref, g_ref, wq_ref, qg_ref, k_ref, v_ref, wo_ref, o_ref = refs[1 + n_act:]
    act = jnp.concatenate([a_ref[0] for a_ref in act_refs], axis=-1)
    x1 = x_ref[0] + _dot(act, w_ref[...])
    h = _rms(x1, g_ref[...]).astype(BF16)
    hd_dim = qg_ref.shape[1]
    cols = [slice(hd * hd_dim, (hd + 1) * hd_dim) for hd in range(MEM_HEADS)]
    q = [_dot(h, wq_ref[:, c]) for c in cols]
    qn = [(_rms(q_h, qg_ref[...]) * (hd_dim ** -0.5)).astype(BF16) for q_h in q]
    sc = [_dot_nt(qn_h, k_ref[0, :, c]) for qn_h, c in zip(qn, cols)]
    p = [jnp.exp(s_h - jnp.max(s_h, axis=-1, keepdims=True)) for s_h in sc]
    o = [(_dot(p_h.astype(BF16), v_ref[0, :, c]) / jnp.sum(p_h, axis=-1, keepdims=True)
          ).astype(BF16) for p_h, c in zip(p, cols)]
    o_ref[0] = x1 + _dot(jnp.concatenate(o, axis=-1), wo_ref[...])


def _mix_xattn(x, acts, w_mix, gain, wq, q_gain, mem_k, mem_v, wo, *, tm):
    b, s, d = x.shape
    m = mem_k.shape[1]
    const2 = lambda bi, i: (0, 0)
    row = lambda width: pl.BlockSpec((1, tm, width), lambda bi, i: (bi, i, 0))
    mem_spec = pl.BlockSpec((1, m, d), lambda bi, i: (bi, 0, 0))
    return pl.pallas_call(
        functools.partial(_mix_xattn_kernel, n_act=len(acts)),
        grid=(b, s // tm),
        in_specs=[
            row(d),
            *[row(a.shape[-1]) for a in acts],
            pl.BlockSpec(w_mix.shape, const2),
            pl.BlockSpec((1, d), const2),
            pl.BlockSpec(wq.shape, const2),
            pl.BlockSpec((1, q_gain.shape[0]), const2),
            mem_spec, mem_spec,
            pl.BlockSpec(wo.shape, const2),
        ],
        out_specs=row(d),
        out_shape=jax.ShapeDtypeStruct((b, s, d), F32),
        compiler_params=_params("parallel", "parallel"),
        name="mix_xattn",
    )(x, *acts, w_mix, gain.reshape(1, d), wq, q_gain.reshape(1, -1), mem_k, mem_v, wo)


def _lane_bcast(vec):
    return jnp.broadcast_to(vec[:, None], (vec.shape[0], LANES))


def _mla_weights(w_in, w_uq, w_ukv, q_lora_gain, kv_lora_gain, q_gain, k_gain):
    lat = MLA_Q_LORA + MLA_KV_LORA
    w_in_ext = jnp.pad(w_in, ((0, 0), (0, lat + LANES - w_in.shape[1])))
    wukv = w_ukv.reshape(MLA_KV_LORA, MLA_HEADS, MLA_NOPE + MLA_V)
    wukt = wukv[..., :MLA_NOPE].reshape(MLA_KV_LORA, -1).T
    wuvt = wukv[..., MLA_NOPE:].reshape(MLA_KV_LORA, -1).T
    half = MLA_ROPE // 2
    inv_freq = ROPE_THETA ** (-jnp.arange(half, dtype=F32) / half)
    bf = lambda a: a.astype(BF16)
    return (bf(w_in_ext), _lane_bcast(q_lora_gain), _lane_bcast(kv_lora_gain),
            bf(w_uq.T), bf(wukt), bf(wuvt),
            _lane_bcast(q_gain * (MLA_QK ** -0.5 * LOG2E)), _lane_bcast(k_gain),
            _lane_bcast(inv_freq))


def _tile(n, pref):
    return pref if n % pref == 0 else n


def kernel(x, mem, positions, ffn_pre_norm, ffn_pre_w_gu, ffn_pre_w_down, mix_norm, sbg_w_in, sgu_ln_gain, sgu_ln_bias, sgu_w, sgu_b, sbg_w_out, mla_w_in, mla_q_lora_gain, mla_kv_lora_gain, mla_w_uq, mla_w_ukv, mla_q_gain, mla_k_gain, mla_w_out, xmem_norm, xmem_mem_norm, xmem_wq, xmem_wkv, xmem_q_gain, xmem_k_gain, xmem_wo, ffn_post_norm, ffn_post_w_gu, ffn_post_w_down):
    b, s, d = x.shape
    depth = ffn_pre_norm.shape[0]
    d_ff = ffn_pre_w_down.shape[1]
    t = b * s
    ffn_tm = _tile(t, 512)
    ffn_tf = _tile(d_ff, 256)
    row_tm = _tile(s, 512)
    mla_tq = _tile(s, 512)
    sb_tq = _tile(s, 512)
    sb_kb = _tile(sb_tq, 256)
    bf = lambda a: a.astype(BF16)

    for layer in range(depth):
        x = _ffn(x.reshape(t, d), ffn_pre_norm[layer], ffn_pre_w_gu, ffn_pre_w_down, layer,
                 tm=ffn_tm, tf=ffn_tf).reshape(b, s, d)
        if layer % 2 == 0:
            e = layer // 2
            bias_full = jnp.repeat(sgu_b[e].T, SG_GROUP_DIM, axis=1)
            qt, k, vt, o_sg = _even_prep(
                x, mix_norm[layer], bf(sbg_w_in[e]), sgu_ln_gain[e],
                sgu_ln_bias[e], sgu_w[e], bias_full, tm=row_tm, kb=sb_kb)
            acts = (_sb_attention(qt, k, vt, tq=sb_tq), o_sg)
            w_mix = bf(sbg_w_out[e])
        else:
            o = layer // 2
            mla_consts = _mla_weights(
                mla_w_in[o], mla_w_uq[o], mla_w_ukv[o], mla_q_lora_gain[o],
                mla_kv_lora_gain[o], mla_q_gain[o], mla_k_gain[o])
            qt, k, vt = _mla_prep(x, positions, mix_norm[layer], *mla_consts,
                                  tm=row_tm, kb=mla_tq)
            acts = (_mla_attention(qt, k, vt, tq=mla_tq, heads_per_step=4),)
            w_mix = bf(mla_w_out[o])
        mem_k, mem_v = _mem_kv(mem, xmem_mem_norm[layer], bf(xmem_wkv[layer]),
                               xmem_k_gain[layer])
        x = _mix_xattn(x, acts, w_mix, xmem_norm[layer], bf(xmem_wq[layer]), xmem_q_gain[layer],
                       mem_k, mem_v, bf(xmem_wo[layer]), tm=row_tm)
        x = _ffn(x.reshape(t, d), ffn_post_norm[layer], ffn_post_w_gu, ffn_post_w_down, layer,
                 tm=ffn_tm, tf=ffn_tf).reshape(b, s, d)
    return x
```

```python
import functools

import jax
import jax.numpy as jnp
from jax import lax
from jax.experimental import pallas as pl
from jax.experimental.pallas import tpu as pltpu

EPS = 1e-6
ROPE_THETA = 10000.0
LANES = 128
SUBLANES = 8
VMEM_LIMIT_BYTES = 56 * 1024 * 1024

SB_HEADS, SB_HEAD_DIM = 8, 64
SB_WIDTH = SB_HEADS * SB_HEAD_DIM
SG_GROUPS, SG_GROUP_DIM, SG_CHUNK = 8, 64, 128
SG_WIDTH = SG_GROUPS * SG_GROUP_DIM
MLA_HEADS, MLA_NOPE, MLA_ROPE, MLA_V = 16, 64, 32, 64
MLA_QK = MLA_NOPE + MLA_ROPE
MLA_Q_LORA, MLA_KV_LORA = 512, 256
MEM_HEADS = 4

BF16 = jnp.bfloat16
F32 = jnp.float32
LOG2E = 1.4426950408889634
MASKED = -1e30
SUM_ROWS = 16
FIXED_SHIFT_MAX = 60.0
UNDERFLOW_LOG2 = 160.0


def _params(*semantics):
    return pltpu.CompilerParams(dimension_semantics=semantics,
                                vmem_limit_bytes=VMEM_LIMIT_BYTES)


def _dot(a, b):
    return jnp.dot(a, b, preferred_element_type=F32)


def _dot_nt(a, b):
    return lax.dot_general(a, b, (((1,), (1,)), ((), ())), preferred_element_type=F32)


def _rms(x, gain, n=None):
    n = x.shape[-1] if n is None else n
    ms = jnp.sum(x * x, axis=-1, keepdims=True) * (1.0 / n)
    return x * lax.rsqrt(ms + EPS) * gain


def _ffn_kernel(x_ref, g_ref, wgu_ref, wd_ref, o_ref, *, tf):
    d_ff = wd_ref.shape[0]
    x = x_ref[...]
    h = _rms(x, g_ref[...]).astype(BF16)
    acc = None
    for c in range(d_ff // tf):
        cols = slice(c * tf, (c + 1) * tf)
        gate = _dot(h, wgu_ref[:, cols].astype(BF16))
        up = _dot(h, wgu_ref[:, d_ff + c * tf:d_ff + (c + 1) * tf].astype(BF16))
        act = (gate * jax.nn.sigmoid(gate) * up).astype(BF16)
        part = _dot(act, wd_ref[cols, :].astype(BF16))
        acc = part if acc is None else acc + part
    o_ref[...] = x + 0.5 * acc


def _ffn(x2, gain, w_gu, w_down, layer, *, tm, tf):
    t, d = x2.shape
    d_ff = w_down.shape[1]
    resident = pl.Buffered(1)
    return pl.pallas_call(
        functools.partial(_ffn_kernel, tf=tf),
        grid=(t // tm,),
        in_specs=[
            pl.BlockSpec((tm, d), lambda i: (i, 0)),
            pl.BlockSpec((1, d), lambda i: (0, 0)),
            pl.BlockSpec((None, d, 2 * d_ff), lambda i: (layer, 0, 0), pipeline_mode=resident),
            pl.BlockSpec((None, d_ff, d), lambda i: (layer, 0, 0), pipeline_mode=resident),
        ],
        out_specs=pl.BlockSpec((tm, d), lambda i: (i, 0)),
        out_shape=jax.ShapeDtypeStruct((t, d), F32),
        compiler_params=_params("parallel"),
        name="ffn",
    )(x2, gain.reshape(1, d), w_gu, w_down)


def _gelu_tanh(x):
    c = 0.7978845608028654
    return 0.5 * x * (1.0 + jnp.tanh(c * (x + 0.044715 * (x * x * x))))


def _even_prep_kernel(x_ref, g_ref, win_ref, lng_ref, lnb_ref, sw_ref, sb_ref,
                      qt_ref, k_ref, vt_ref, osg_ref):
    tm = x_ref.shape[1]
    kb = vt_ref.shape[3]
    w = SB_WIDTH
    h = _rms(x_ref[0], g_ref[...]).astype(BF16)
    osg_ref = osg_ref.at[0]
    u_raw = _dot(h, win_ref[:, 3 * w:3 * w + SG_WIDTH])
    g_raw = _dot(h, win_ref[:, 3 * w + SG_WIDTH:3 * w + 2 * SG_WIDTH])
    qt_ref[0] = (_dot(h, win_ref[:, 0:w]) * (SB_HEAD_DIM ** -0.5 * LOG2E)).T.astype(BF16)
    u = _gelu_tanh(u_raw)
    k_ref[0] = _dot(h, win_ref[:, w:2 * w]).astype(BF16)
    g = _gelu_tanh(g_raw)
    mu = jnp.mean(g, axis=-1, keepdims=True)
    gc = g - mu
    var = jnp.mean(gc * gc, axis=-1, keepdims=True)
    gn = (gc * lax.rsqrt(var + EPS) * lng_ref[...] + lnb_ref[...]).astype(BF16)
    v = _dot(h, win_ref[:, 2 * w:3 * w])
    for c in range(tm // kb):
        vt_ref[0, c] = v[c * kb:(c + 1) * kb, :].T.astype(BF16)

    row = lax.broadcasted_iota(jnp.int32, (SG_CHUNK, SG_CHUNK), 0)
    col = lax.broadcasted_iota(jnp.int32, (SG_CHUNK, SG_CHUNK), 1)
    tri = col <= row
    first_group = lax.broadcasted_iota(jnp.int32, (SG_CHUNK, LANES), 1) < SG_GROUP_DIM
    for p in range(SG_GROUPS // 2):
        lanes = slice(p * LANES, (p + 1) * LANES)
        w0 = jnp.where(tri, sw_ref[2 * p], 0.0).astype(BF16)
        w1 = jnp.where(tri, sw_ref[2 * p + 1], 0.0).astype(BF16)
        bias = sb_ref[:, lanes]
        for c in range(tm // SG_CHUNK):
            rows = slice(c * SG_CHUNK, (c + 1) * SG_CHUNK)
            gp = gn[rows, lanes]
            mixed = jnp.where(first_group, _dot(w0, gp), _dot(w1, gp)) + bias
            osg_ref[rows, lanes] = (u[rows, lanes] * mixed).astype(BF16)


def _even_prep(x, gain, w_in, ln_g, ln_b, sgu_w, sgu_bias_full, *, tm, kb):
    b, s, d = x.shape
    n_in = w_in.shape[1]
    w = SB_WIDTH
    const2 = lambda bi, i: (0, 0)
    row_out = jax.ShapeDtypeStruct((b, s, w), BF16)
    row_spec = pl.BlockSpec((1, tm, w), lambda bi, i: (bi, i, 0))
    return pl.pallas_call(
        _even_prep_kernel,
        grid=(b, s // tm),
        in_specs=[
            pl.BlockSpec((1, tm, d), lambda bi, i: (bi, i, 0)),
            pl.BlockSpec((1, d), const2),
            pl.BlockSpec((d, n_in), const2),
            pl.BlockSpec((1, SG_WIDTH), const2),
            pl.BlockSpec((1, SG_WIDTH), const2),
            pl.BlockSpec((SG_GROUPS, SG_CHUNK, SG_CHUNK), lambda bi, i: (0, 0, 0)),
            pl.BlockSpec((SG_CHUNK, SG_WIDTH), const2),
        ],
        out_specs=[pl.BlockSpec((1, w, tm), lambda bi, i: (bi, 0, i)),
                   row_spec,
                   pl.BlockSpec((1, tm // kb, w, kb), lambda bi, i: (bi, i, 0, 0)),
                   row_spec],
        out_shape=[jax.ShapeDtypeStruct((b, w, s), BF16), row_out,
                   jax.ShapeDtypeStruct((b, s // kb, w, kb), BF16), row_out],
        compiler_params=_params("parallel", "parallel"),
        name="even_prep",
    )(x, gain.reshape(1, d), w_in, ln_g.reshape(1, -1), ln_b.reshape(1, -1),
      sgu_w, sgu_bias_full)


def _sb_attn_kernel(qt_ref, k_ref, vt_ref, o_ref, acc_ref, r_ref, z0_ref, z1_ref, zc0_ref,
                    zc1_ref, t0_ref, t1_ref, bs_ref, kn_ref, *, tq, kb):
    i = pl.program_id(2)
    n_sub = tq // kb
    hd_dim = SB_HEAD_DIM
    z_refs, zc_refs, t_refs = (z0_ref, z1_ref), (zc0_ref, zc1_ref), (t0_ref, t1_ref)
    acc_refs = (acc_ref.at[0], acc_ref.at[1])
    first_head = lax.broadcasted_iota(jnp.int32, (LANES, 1), 0) < hd_dim
    qt = qt_ref[0]
    zero = jnp.zeros_like(qt)
    qt_heads = (jnp.where(first_head, qt, zero), jnp.where(first_head, zero, qt))
    visible_from = {
        lo: (lax.broadcasted_iota(jnp.int32, (kb, tq - lo), 0)
             < lax.broadcasted_iota(jnp.int32, (kb, tq - lo), 1))
        for lo in range(0, tq, kb)}
    srow = lax.broadcasted_iota(jnp.int32, (kb, kb), 0)
    scol = lax.broadcasted_iota(jnp.int32, (kb, kb), 1)
    suffix = (scol >= srow).astype(BF16)

    def score(item):
        g, hd, slot, lo = item
        start = pl.multiple_of(g * kb, kb)
        z_refs[slot][:, lo:] = _dot(k_ref[0, pl.ds(start, kb), :], qt_heads[hd][:, lo:])

    def stay(item, diagonal):
        g, hd, slot, lo = item
        z = z_refs[slot][:, lo:]
        sp = jnp.maximum(z, 0.0) + jnp.log2(1.0 + jnp.exp2(-jnp.abs(z)))
        if diagonal:
            visible = visible_from[lo]
            sp = jnp.where(visible, sp, 0.0)
            z = jnp.where(visible, z, MASKED)
        zc_refs[slot][:, lo:] = z
        tail = _dot(suffix, sp.astype(BF16))
        t_refs[slot][:, lo:] = tail
        block_sum = jnp.broadcast_to(tail[0:1, :], (SUBLANES, tq - lo))
        if lo:
            block_sum = jnp.concatenate([jnp.zeros((SUBLANES, lo), F32), block_sum], axis=1)
        bs_ref[slot] = block_sum

    def weigh(item):
        g, hd, slot, lo = item
        r = r_ref[hd]
        wgt = jnp.exp2(zc_refs[slot][:, lo:] - t_refs[slot][:, lo:] - r[0:1, lo:])
        vt = vt_ref[0, g, hd * hd_dim:(hd + 1) * hd_dim, :]
        acc_refs[hd][:, lo:] += _dot(vt, wgt.astype(BF16))
        r_ref[hd] = r + bs_ref[slot]

    def block(g, lo, prev_lo, g_next, next_lo, diagonal):
        score((g, 1, 1, lo))
        if prev_lo is not None:
            weigh((g + 1, 1, 1, prev_lo))
        stay((g, 0, 0, lo), diagonal)
        score((g_next, 0, 0, next_lo))
        weigh((g, 0, 0, lo))
        stay((g, 1, 1, lo), diagonal)

    @pl.when(i == 0)
    def _():
        kf = k_ref[0].astype(F32)
        dim = lax.broadcasted_iota(jnp.int32, (LANES, LANES), 0)
        head = lax.broadcasted_iota(jnp.int32, (LANES, LANES), 1)
        select = ((dim < hd_dim) == (head == 0)) & (head < 2)
        kn_ref[...] = jnp.max(_dot((kf * kf).astype(BF16), select.astype(BF16)),
                              axis=0, keepdims=True)

    lane = lax.broadcasted_iota(jnp.int32, (1, LANES), 1)
    qf = qt.astype(F32)
    exit_level = []
    for hd in range(2):
        q_sq = jnp.sum(jnp.square(qf[hd * hd_dim:(hd + 1) * hd_dim, :]), axis=0, keepdims=True)
        k_sq = jnp.max(jnp.where(lane == hd, kn_ref[...], 0.0), axis=1, keepdims=True)
        exit_level.append(1.02 * jnp.sqrt(q_sq * k_sq) + UNDERFLOW_LOG2)

    acc_ref[...] = jnp.zeros_like(acc_ref)
    r_ref[...] = jnp.zeros_like(r_ref)
    first = i * n_sub
    score((first + n_sub - 1, 0, 0, (n_sub - 1) * kb))
    for d in reversed(range(n_sub)):
        block(first + d, d * kb, (d + 1) * kb if d < n_sub - 1 else None,
              jnp.maximum(first + d - 1, 0), max(d - 1, 0) * kb, True)

    def more(carry):
        g, live = carry
        return (g >= 0) & (live > 0)

    def body(carry):
        g, _ = carry
        block(g, 0, 0, jnp.maximum(g - 1, 0), 0, False)
        dead = jnp.min(jnp.minimum(r_ref[0] - exit_level[0], r_ref[1] - exit_level[1])) > 0.0
        return g - 1, jnp.where(dead, 0, 1).astype(jnp.int32)

    g_end, _ = lax.while_loop(more, body, (first - 1, jnp.int32(1)))
    weigh((g_end + 1, 1, 1, 0))
    out_t = jnp.concatenate([acc_ref[0], acc_ref[1]], axis=0)
    o_ref[0] = out_t.T.astype(BF16)


def _sb_attention(qt, k, vt, *, tq):
    b, w, s = qt.shape
    kb = vt.shape[3]
    return pl.pallas_call(
        functools.partial(_sb_attn_kernel, tq=tq, kb=kb),
        grid=(b, w // LANES, s // tq),
        in_specs=[
            pl.BlockSpec((1, LANES, tq), lambda bi, p, i: (bi, p, i)),
            pl.BlockSpec((1, s, LANES), lambda bi, p, i: (bi, 0, p)),
            pl.BlockSpec((1, s // kb, LANES, kb), lambda bi, p, i: (bi, 0, p, 0)),
        ],
        out_specs=pl.BlockSpec((1, tq, LANES), lambda bi, p, i: (bi, i, p)),
        out_shape=jax.ShapeDtypeStruct((b, s, w), BF16),
        scratch_shapes=[pltpu.VMEM((2, SB_HEAD_DIM, tq), F32),
                        pltpu.VMEM((2, SUBLANES, tq), F32),
                        *[pltpu.VMEM((kb, tq), F32) for _ in range(6)],
                        pltpu.VMEM((2, SUBLANES, tq), F32), pltpu.VMEM((1, LANES), F32)],
        compiler_params=_params("parallel", "parallel", "arbitrary"),
        name="sb_attn",
    )(qt, k, vt)


def _lane_tile(t, width):
    return jnp.concatenate([t] * (width // t.shape[1]), axis=1)


def _rms_rows(xt, gain):
    ms = jnp.sum(xt * xt, axis=0, keepdims=True) * (1.0 / xt.shape[0])
    return xt * lax.rsqrt(ms + EPS) * _lane_tile(gain, xt.shape[1])


def _mla_prep_kernel(x_ref, pos_ref, g_ref, win_ref, qlg_ref, kvlg_ref, wuqt_ref, wukt_ref,
                     wuvt_ref, qg_ref, kg_ref, freq_ref, qt_ref, k_ref, vt_ref):
    tm = x_ref.shape[1]
    kb = vt_ref.shape[3]
    lat = MLA_Q_LORA + MLA_KV_LORA
    half = MLA_ROPE // 2
    h = _rms(x_ref[0], g_ref[...]).astype(BF16)
    ct = _dot(h, win_ref[...]).T
    cqn = _rms_rows(ct[0:MLA_Q_LORA], qlg_ref[...]).astype(BF16)
    ckvn = _rms_rows(ct[MLA_Q_LORA:lat], kvlg_ref[...]).astype(BF16)
    k_r = ct[lat:lat + MLA_ROPE]

    group = 4

    def up_project(gi):
        rows = slice(gi * group * MLA_QK, (gi + 1) * group * MLA_QK)
        krows = slice(gi * group * MLA_NOPE, (gi + 1) * group * MLA_NOPE)
        return _dot(wuqt_ref[rows, :], cqn), _dot(wukt_ref[krows, :], ckvn)

    angle = _lane_tile(freq_ref[...], tm) * pos_ref[0].astype(F32)
    cos = jnp.cos(angle)
    sin = jnp.sin(angle)

    def rope(t):
        t1, t2 = t[0:half], t[half:]
        return jnp.concatenate([t1 * cos - t2 * sin, t1 * sin + t2 * cos], axis=0)

    qg = _lane_tile(qg_ref[...], tm)
    kg = _lane_tile(kg_ref[...], tm)
    zeros = jnp.zeros((LANES - MLA_QK, tm), F32)
    inv_n = 1.0 / MLA_QK
    kr_sq = jnp.sum(k_r * k_r, axis=0, keepdims=True)
    kr_roped = rope(k_r * kg[MLA_NOPE:])
    n_groups = MLA_HEADS // group
    pending = up_project(0)
    for gi in range(n_groups):
        q_grp, kn_grp = pending
        if gi + 1 < n_groups:
            pending = up_project(gi + 1)
        else:
            vt = _dot(wuvt_ref[...], ckvn)
            for c in range(tm // kb):
                vt_ref[0, c] = vt[:, c * kb:(c + 1) * kb].astype(BF16)
        for sub in range(group):
            hd = gi * group + sub
            qh = q_grp[sub * MLA_QK:(sub + 1) * MLA_QK]
            r = lax.rsqrt(jnp.sum(qh * qh, axis=0, keepdims=True) * inv_n + EPS)
            qn = qh * r * qg
            qt_ref[0, hd] = jnp.concatenate([qn[0:MLA_NOPE], rope(qn[MLA_NOPE:]), zeros],
                                            axis=0).astype(BF16)
            kn = kn_grp[sub * MLA_NOPE:(sub + 1) * MLA_NOPE]
            r = lax.rsqrt((jnp.sum(kn * kn, axis=0, keepdims=True) + kr_sq) * inv_n + EPS)
            kt = jnp.concatenate([kn * r * kg[0:MLA_NOPE], kr_roped * r, zeros], axis=0)
            k_ref[0, hd] = kt.T.astype(BF16)


def _mla_prep(x, positions, gain, w_in, qlg, kvlg, wuqt, wukt, wuvt, qg, kg, freq, *, tm, kb):
    b, s, d = x.shape
    vw = MLA_HEADS * MLA_V
    const2 = lambda bi, i: (0, 0)
    full = lambda a: pl.BlockSpec(a.shape, const2)
    small = [gain.reshape(1, -1), w_in, qlg, kvlg, wuqt, wukt, wuvt, qg, kg, freq]
    return pl.pallas_call(
        _mla_prep_kernel,
        grid=(b, s // tm),
        in_specs=[
            pl.BlockSpec((1, tm, d), lambda bi, i: (bi, i, 0)),
            pl.BlockSpec((1, 1, tm), lambda bi, i: (bi, 0, i)),
            *[full(a) for a in small],
        ],
        out_specs=[pl.BlockSpec((1, MLA_HEADS, LANES, tm), lambda bi, i: (bi, 0, 0, i)),
                   pl.BlockSpec((1, MLA_HEADS, tm, LANES), lambda bi, i: (bi, 0, i, 0)),
                   pl.BlockSpec((1, tm // kb, vw, kb), lambda bi, i: (bi, i, 0, 0))],
        out_shape=[jax.ShapeDtypeStruct((b, MLA_HEADS, LANES, s), BF16),
                   jax.ShapeDtypeStruct((b, MLA_HEADS, s, LANES), BF16),
                   jax.ShapeDtypeStruct((b, s // kb, vw, kb), BF16)],
        compiler_params=_params("parallel", "parallel"),
        name="mla_prep",
    )(x, positions.reshape(b, 1, s), *small)


def _mla_attn_kernel(qt_ref, k_ref, vt_ref, o_ref, acc_ref, m_ref, s0_ref, s1_ref, bm_ref,
                     p0_ref, p1_ref, kn_ref, *, tq):
    i = pl.program_id(2)
    n_heads = qt_ref.shape[1]
    s_refs, p_refs = (s0_ref, s1_ref), (p0_ref, p1_ref)
    key = lax.broadcasted_iota(jnp.int32, (tq, tq), 0)
    query = lax.broadcasted_iota(jnp.int32, (tq, tq), 1)
    causal = key <= query
    ones_rows = jnp.ones((SUM_ROWS, tq), BF16)

    def scores(g, hd, masked):
        start = pl.multiple_of(g * tq, tq)
        sc = _dot(k_ref[0, hd, pl.ds(start, tq), :], qt_ref[0, hd])
        return jnp.where(causal, sc, MASKED) if masked else sc

    def values(g, hd):
        return jnp.concatenate([vt_ref[0, g, hd * MLA_V:(hd + 1) * MLA_V, :], ones_rows], axis=0)

    def sweep(produce, consume):
        def visit(g, g_next, masked):
            for hd in range(n_heads):
                if hd + 1 < n_heads:
                    produce(g, hd + 1, masked)
                else:
                    produce(g_next, 0, False)
                consume(g, hd)

        produce(i, 0, True)
        visit(i, 0, True)

        def body(it, carry):
            visit(2 * it, 2 * it + 1, False)
            visit(2 * it + 1, jnp.minimum(2 * it + 2, i - 1), False)
            return carry

        lax.fori_loop(0, i // 2, body, 0)

        @pl.when(i % 2 == 1)
        def _():
            visit(i - 1, i - 1, False)

    @pl.when(i == 0)
    def _():
        ones = jnp.ones((LANES, LANES), BF16)
        for hd in range(n_heads):
            kf = k_ref[0, hd].astype(F32)
            kn_ref[hd] = jnp.max(_dot((kf * kf).astype(BF16), ones), axis=0, keepdims=True)

    bound = []
    for hd in range(n_heads):
        qf = qt_ref[0, hd].astype(F32)
        q_sq = jnp.sum(qf * qf, axis=0, keepdims=True)
        bound.append(1.02 * jnp.sqrt(q_sq * kn_ref[hd][:, 0:1]))
    largest = functools.reduce(jnp.maximum, [jnp.max(b) for b in bound])
    fixed_shift = largest <= FIXED_SHIFT_MAX

    acc_ref[...] = jnp.zeros_like(acc_ref)

    @pl.when(fixed_shift)
    def _():
        def produce(g, hd, masked):
            p_refs[hd % 2][...] = jnp.exp2(scores(g, hd, masked) - bound[hd]).astype(BF16)

        def consume(g, hd):
            acc_ref[hd] += _dot(values(g, hd), p_refs[hd % 2][...])

        sweep(produce, consume)

    @pl.when(jnp.logical_not(fixed_shift))
    def _():
        m_ref[...] = jnp.full_like(m_ref, MASKED)

        def produce(g, hd, masked):
            sc = scores(g, hd, masked)
            s_refs[hd % 2][...] = sc
            bm_ref[hd % 2] = jnp.max(sc, axis=0, keepdims=True)

        def consume(g, hd):
            m_old = m_ref[hd]
            m_new = jnp.maximum(m_old, bm_ref[hd % 2])
            alpha = jnp.exp2(m_old - m_new)
            p = jnp.exp2(s_refs[hd % 2][...] - m_new)
            acc_ref[hd] = alpha * acc_ref[hd] + _dot(values(g, hd), p.astype(BF16))
            m_ref[hd] = m_new

        sweep(produce, consume)

    out_t = jnp.concatenate([acc_ref[hd, 0:MLA_V, :] / acc_ref[hd, MLA_V:MLA_V + 1, :]
                             for hd in range(n_heads)], axis=0)
    o_ref[0] = out_t.T.astype(BF16)


def _mla_attention(qt, k, vt, *, tq, heads_per_step):
    b, heads, _, s = qt.shape
    kb = vt.shape[3]
    nh = heads_per_step
    assert kb == tq and nh % 2 == 0
    vw = nh * MLA_V
    return pl.pallas_call(
        functools.partial(_mla_attn_kernel, tq=tq),
        grid=(b, heads // nh, s // tq),
        in_specs=[
            pl.BlockSpec((1, nh, LANES, tq), lambda bi, p, i: (bi, p, 0, i)),
            pl.BlockSpec((1, nh, s, LANES), lambda bi, p, i: (bi, p, 0, 0)),
            pl.BlockSpec((1, s // kb, vw, kb), lambda bi, p, i: (bi, 0, p, 0)),
        ],
        out_specs=pl.BlockSpec((1, tq, vw), lambda bi, p, i: (bi, i, p)),
        out_shape=jax.ShapeDtypeStruct((b, s, heads * MLA_V), BF16),
        scratch_shapes=[pltpu.VMEM((nh, MLA_V + SUM_ROWS, tq), F32),
                        pltpu.VMEM((nh, 1, tq), F32), pltpu.VMEM((kb, tq), F32),
                        pltpu.VMEM((kb, tq), F32), pltpu.VMEM((2, 1, tq), F32),
                        pltpu.VMEM((kb, tq), BF16), pltpu.VMEM((kb, tq), BF16),
                        pltpu.VMEM((nh, 1, LANES), F32)],
        compiler_params=_params("parallel", "parallel", "arbitrary"),
        name="mla_attn",
    )(qt, k, vt)


def _mem_kv_kernel(mem_ref, g_ref, wkv_ref, kg_ref, k_ref, v_ref):
    hm = _rms(mem_ref[0], g_ref[...]).astype(BF16)
    hd_dim = kg_ref.shape[1]
    for hd in range(MEM_HEADS):
        kcols = slice(2 * hd * hd_dim, (2 * hd + 1) * hd_dim)
        vcols = slice((2 * hd + 1) * hd_dim, (2 * hd + 2) * hd_dim)
        out = slice(hd * hd_dim, (hd + 1) * hd_dim)
        k_ref[0, :, out] = _rms(_dot(hm, wkv_ref[:, kcols]), kg_ref[...]).astype(BF16)
        v_ref[0, :, out] = _dot(hm, wkv_ref[:, vcols]).astype(BF16)


def _mem_kv(mem, gain, wkv, k_gain):
    b, m, d = mem.shape
    out = jax.ShapeDtypeStruct((b, m, d), BF16)
    spec = pl.BlockSpec((1, m, d), lambda bi: (bi, 0, 0))
    return pl.pallas_call(
        _mem_kv_kernel,
        grid=(b,),
        in_specs=[spec,
                  pl.BlockSpec((1, d), lambda bi: (0, 0)),
                  pl.BlockSpec(wkv.shape, lambda bi: (0, 0)),
                  pl.BlockSpec((1, k_gain.shape[0]), lambda bi: (0, 0))],
        out_specs=[spec, spec],
        out_shape=[out, out],
        compiler_params=_params("parallel"),
        name="mem_kv",
    )(mem, gain.reshape(1, d), wkv, k_gain.reshape(1, -1))


def _mix_xattn_kernel(*refs, n_act):
    x_ref = refs[0]
    act_refs = refs[1:1 + n_act]
    w_ref, g_ref, wq_ref, qg_ref, k_ref, v_ref, wo_ref, o_ref = refs[1 + n_act:]
    act = jnp.concatenate([a_ref[0] for a_ref in act_refs], axis=-1)
    x1 = x_ref[0] + _dot(act, w_ref[...])
    h = _rms(x1, g_ref[...]).astype(BF16)
    hd_dim = qg_ref.shape[1]
    cols = [slice(hd * hd_dim, (hd + 1) * hd_dim) for hd in range(MEM_HEADS)]
    q = [_dot(h, wq_ref[:, c]) for c in cols]
    qn = [(_rms(q_h, qg_ref[...]) * (hd_dim ** -0.5)).astype(BF16) for q_h in q]
    sc = [_dot_nt(qn_h, k_ref[0, :, c]) for qn_h, c in zip(qn, cols)]
    p = [jnp.exp(s_h - jnp.max(s_h, axis=-1, keepdims=True)) for s_h in sc]
    o = [(_dot(p_h.astype(BF16), v_ref[0, :, c]) / jnp.sum(p_h, axis=-1, keepdims=True)
          ).astype(BF16) for p_h, c in zip(p, cols)]
    o_ref[0] = x1 + _dot(jnp.concatenate(o, axis=-1), wo_ref[...])


def _mix_xattn(x, acts, w_mix, gain, wq, q_gain, mem_k, mem_v, wo, *, tm):
    b, s, d = x.shape
    m = mem_k.shape[1]
    const2 = lambda bi, i: (0, 0)
    row = lambda width: pl.BlockSpec((1, tm, width), lambda bi, i: (bi, i, 0))
    mem_spec = pl.BlockSpec((1, m, d), lambda bi, i: (bi, 0, 0))
    return pl.pallas_call(
        functools.partial(_mix_xattn_kernel, n_act=len(acts)),
        grid=(b, s // tm),
        in_specs=[
            row(d),
            *[row(a.shape[-1]) for a in acts],
            pl.BlockSpec(w_mix.shape, const2),
            pl.BlockSpec((1, d), const2),
            pl.BlockSpec(wq.shape, const2),
            pl.BlockSpec((1, q_gain.shape[0]), const2),
            mem_spec, mem_spec,
            pl.BlockSpec(wo.shape, const2),
        ],
        out_specs=row(d),
        out_shape=jax.ShapeDtypeStruct((b, s, d), F32),
        compiler_params=_params("parallel", "parallel"),
        name="mix_xattn",
    )(x, *acts, w_mix, gain.reshape(1, d), wq, q_gain.reshape(1, -1), mem_k, mem_v, wo)


def _lane_bcast(vec):
    return jnp.broadcast_to(vec[:, None], (vec.shape[0], LANES))


def _mla_weights(w_in, w_uq, w_ukv, q_lora_gain, kv_lora_gain, q_gain, k_gain):
    lat = MLA_Q_LORA + MLA_KV_LORA
    w_in_ext = jnp.pad(w_in, ((0, 0), (0, lat + LANES - w_in.shape[1])))
    wukv = w_ukv.reshape(MLA_KV_LORA, MLA_HEADS, MLA_NOPE + MLA_V)
    wukt = wukv[..., :MLA_NOPE].reshape(MLA_KV_LORA, -1).T
    wuvt = wukv[..., MLA_NOPE:].reshape(MLA_KV_LORA, -1).T
    half = MLA_ROPE // 2
    inv_freq = ROPE_THETA ** (-jnp.arange(half, dtype=F32) / half)
    bf = lambda a: a.astype(BF16)
    return (bf(w_in_ext), _lane_bcast(q_lora_gain), _lane_bcast(kv_lora_gain),
            bf(w_uq.T), bf(wukt), bf(wuvt),
            _lane_bcast(q_gain * (MLA_QK ** -0.5 * LOG2E)), _lane_bcast(k_gain),
            _lane_bcast(inv_freq))


def _tile(n, pref):
    return pref if n % pref == 0 else n


def kernel(x, mem, positions, ffn_pre_norm, ffn_pre_w_gu, ffn_pre_w_down, mix_norm, sbg_w_in, sgu_ln_gain, sgu_ln_bias, sgu_w, sgu_b, sbg_w_out, mla_w_in, mla_q_lora_gain, mla_kv_lora_gain, mla_w_uq, mla_w_ukv, mla_q_gain, mla_k_gain, mla_w_out, xmem_norm, xmem_mem_norm, xmem_wq, xmem_wkv, xmem_q_gain, xmem_k_gain, xmem_wo, ffn_post_norm, ffn_post_w_gu, ffn_post_w_down):
    b, s, d = x.shape
    depth = ffn_pre_norm.shape[0]
    d_ff = ffn_pre_w_down.shape[1]
    t = b * s
    ffn_tm = _tile(t, 512)
    ffn_tf = _tile(d_ff, 256)
    row_tm = _tile(s, 512)
    mla_tq = _tile(s, 512)
    sb_tq = _tile(s, 512)
    sb_kb = _tile(sb_tq, 256)
    bf = lambda a: a.astype(BF16)

    for layer in range(depth):
        x = _ffn(x.reshape(t, d), ffn_pre_norm[layer], ffn_pre_w_gu, ffn_pre_w_down, layer,
                 tm=ffn_tm, tf=ffn_tf).reshape(b, s, d)
        if layer % 2 == 0:
            e = layer // 2
            bias_full = jnp.repeat(sgu_b[e].T, SG_GROUP_DIM, axis=1)
            qt, k, vt, o_sg = _even_prep(
                x, mix_norm[layer], bf(sbg_w_in[e]), sgu_ln_gain[e],
                sgu_ln_bias[e], sgu_w[e], bias_full, tm=row_tm, kb=sb_kb)
            acts = (_sb_attention(qt, k, vt, tq=sb_tq), o_sg)
            w_mix = bf(sbg_w_out[e])
        else:
            o = layer // 2
            mla_consts = _mla_weights(
                mla_w_in[o], mla_w_uq[o], mla_w_ukv[o], mla_q_lora_gain[o],
                mla_kv_lora_gain[o], mla_q_gain[o], mla_k_gain[o])
            qt, k, vt = _mla_prep(x, positions, mix_norm[layer], *mla_consts,
                                  tm=row_tm, kb=mla_tq)
            acts = (_mla_attention(qt, k, vt, tq=mla_tq, heads_per_step=4),)
            w_mix = bf(mla_w_out[o])
        mem_k, mem_v = _mem_kv(mem, xmem_mem_norm[layer], bf(xmem_wkv[layer]),
                               xmem_k_gain[layer])
        x = _mix_xattn(x, acts, w_mix, xmem_norm[layer], bf(xmem_wq[layer]), xmem_q_gain[layer],
                       mem_k, mem_v, bf(xmem_wo[layer]), tm=row_tm)
        x = _ffn(x.reshape(t, d), ffn_post_norm[layer], ffn_post_w_gu, ffn_post_w_down, layer,
                 tm=ffn_tm, tf=ffn_tf).reshape(b, s, d)
    return x
```

```python
import functools

import jax
import jax.numpy as jnp
from jax import lax
from jax.experimental import pallas as pl
from jax.experimental.pallas import tpu as pltpu

EPS = 1e-6
ROPE_THETA = 10000.0
LANES = 128
SUBLANES = 8
VMEM_LIMIT_BYTES = 56 * 1024 * 1024

SB_HEADS, SB_HEAD_DIM = 8, 64
SB_WIDTH = SB_HEADS * SB_HEAD_DIM
SG_GROUPS, SG_GROUP_DIM, SG_CHUNK = 8, 64, 128
SG_WIDTH = SG_GROUPS * SG_GROUP_DIM
MLA_HEADS, MLA_NOPE, MLA_ROPE, MLA_V = 16, 64, 32, 64
MLA_QK = MLA_NOPE + MLA_ROPE
MLA_Q_LORA, MLA_KV_LORA = 512, 256
MEM_HEADS = 4

BF16 = jnp.bfloat16
F32 = jnp.float32
LOG2E = 1.4426950408889634
MASKED = -1e30
SUM_ROWS = 16
FIXED_SHIFT_MAX = 60.0
UNDERFLOW_LOG2 = 160.0


def _params(*semantics):
    return pltpu.CompilerParams(dimension_semantics=semantics,
                                vmem_limit_bytes=VMEM_LIMIT_BYTES)


def _dot(a, b):
    return jnp.dot(a, b, preferred_element_type=F32)


def _dot_nt(a, b):
    return lax.dot_general(a, b, (((1,), (1,)), ((), ())), preferred_element_type=F32)


def _rms(x, gain, n=None):
    n = x.shape[-1] if n is None else n
    ms = jnp.sum(x * x, axis=-1, keepdims=True) * (1.0 / n)
    return x * lax.rsqrt(ms + EPS) * gain


def _ffn_kernel(x_ref, g_ref, wgu_ref, wd_ref, o_ref, *, tf):
    d_ff = wd_ref.shape[0]
    x = x_ref[...]
    h = _rms(x, g_ref[...]).astype(BF16)
    acc = None
    for c in range(d_ff // tf):
        cols = slice(c * tf, (c + 1) * tf)
        gate = _dot(h, wgu_ref[:, cols].astype(BF16))
        up = _dot(h, wgu_ref[:, d_ff + c * tf:d_ff + (c + 1) * tf].astype(BF16))
        act = (gate * jax.nn.sigmoid(gate) * up).astype(BF16)
        part = _dot(act, wd_ref[cols, :].astype(BF16))
        acc = part if acc is None else acc + part
    o_ref[...] = x + 0.5 * acc


def _ffn(x2, gain, w_gu, w_down, layer, *, tm, tf):
    t, d = x2.shape
    d_ff = w_down.shape[1]
    resident = pl.Buffered(1)
    return pl.pallas_call(
        functools.partial(_ffn_kernel, tf=tf),
        grid=(t // tm,),
        in_specs=[
            pl.BlockSpec((tm, d), lambda i: (i, 0)),
            pl.BlockSpec((1, d), lambda i: (0, 0)),
            pl.BlockSpec((None, d, 2 * d_ff), lambda i: (layer, 0, 0), pipeline_mode=resident),
            pl.BlockSpec((None, d_ff, d), lambda i: (layer, 0, 0), pipeline_mode=resident),
        ],
        out_specs=pl.BlockSpec((tm, d), lambda i: (i, 0)),
        out_shape=jax.ShapeDtypeStruct((t, d), F32),
        compiler_params=_params("parallel"),
        name="ffn",
    )(x2, gain.reshape(1, d), w_gu, w_down)


def _gelu_tanh(x):
    c = 0.7978845608028654
    return 0.5 * x * (1.0 + jnp.tanh(c * (x + 0.044715 * (x * x * x))))


def _even_prep_kernel(x_ref, g_ref, win_ref, lng_ref, lnb_ref, sw_ref, sb_ref,
                      qt_ref, k_ref, vt_ref, osg_ref):
    tm = x_ref.shape[1]
    kb = vt_ref.shape[3]
    w = SB_WIDTH
    h = _rms(x_ref[0], g_ref[...]).astype(BF16)
    osg_ref = osg_ref.at[0]
    u_raw = _dot(h, win_ref[:, 3 * w:3 * w + SG_WIDTH])
    g_raw = _dot(h, win_ref[:, 3 * w + SG_WIDTH:3 * w + 2 * SG_WIDTH])
    qt_ref[0] = (_dot(h, win_ref[:, 0:w]) * (SB_HEAD_DIM ** -0.5 * LOG2E)).T.astype(BF16)
    u = _gelu_tanh(u_raw)
    k_ref[0] = _dot(h, win_ref[:, w:2 * w]).astype(BF16)
    g = _gelu_tanh(g_raw)
    mu = jnp.mean(g, axis=-1, keepdims=True)
    gc = g - mu
    var = jnp.mean(gc * gc, axis=-1, keepdims=True)
    gn = (gc * lax.rsqrt(var + EPS) * lng_ref[...] + lnb_ref[...]).astype(BF16)
    v = _dot(h, win_ref[:, 2 * w:3 * w])
    for c in range(tm // kb):
        vt_ref[0, c] = v[c * kb:(c + 1) * kb, :].T.astype(BF16)

    row = lax.broadcasted_iota(jnp.int32, (SG_CHUNK, SG_CHUNK), 0)
    col = lax.broadcasted_iota(jnp.int32, (SG_CHUNK, SG_CHUNK), 1)
    tri = col <= row
    first_group = lax.broadcasted_iota(jnp.int32, (SG_CHUNK, LANES), 1) < SG_GROUP_DIM
    for p in range(SG_GROUPS // 2):
        lanes = slice(p * LANES, (p + 1) * LANES)
        w0 = jnp.where(tri, sw_ref[2 * p], 0.0).astype(BF16)
        w1 = jnp.where(tri, sw_ref[2 * p + 1], 0.0).astype(BF16)
        bias = sb_ref[:, lanes]
        for c in range(tm // SG_CHUNK):
            rows = slice(c * SG_CHUNK, (c + 1) * SG_CHUNK)
            gp = gn[rows, lanes]
            mixed = jnp.where(first_group, _dot(w0, gp), _dot(w1, gp)) + bias
            osg_ref[rows, lanes] = (u[rows, lanes] * mixed).astype(BF16)


def _even_prep(x, gain, w_in, ln_g, ln_b, sgu_w, sgu_bias_full, *, tm, kb):
    b, s, d = x.shape
    n_in = w_in.shape[1]
    w = SB_WIDTH
    const2 = lambda bi, i: (0, 0)
    row_out = jax.ShapeDtypeStruct((b, s, w), BF16)
    row_spec = pl.BlockSpec((1, tm, w), lambda bi, i: (bi, i, 0))
    return pl.pallas_call(
        _even_prep_kernel,
        grid=(b, s // tm),
        in_specs=[
            pl.BlockSpec((1, tm, d), lambda bi, i: (bi, i, 0)),
            pl.BlockSpec((1, d), const2),
            pl.BlockSpec((d, n_in), const2),
            pl.BlockSpec((1, SG_WIDTH), const2),
            pl.BlockSpec((1, SG_WIDTH), const2),
            pl.BlockSpec((SG_GROUPS, SG_CHUNK, SG_CHUNK), lambda bi, i: (0, 0, 0)),
            pl.BlockSpec((SG_CHUNK, SG_WIDTH), const2),
        ],
        out_specs=[pl.BlockSpec((1, w, tm), lambda bi, i: (bi, 0, i)),
                   row_spec,
                   pl.BlockSpec((1, tm // kb, w, kb), lambda bi, i: (bi, i, 0, 0)),
                   row_spec],
        out_shape=[jax.ShapeDtypeStruct((b, w, s), BF16), row_out,
                   jax.ShapeDtypeStruct((b, s // kb, w, kb), BF16), row_out],
        compiler_params=_params("parallel", "parallel"),
        name="even_prep",
    )(x, gain.reshape(1, d), w_in, ln_g.reshape(1, -1), ln_b.reshape(1, -1),
      sgu_w, sgu_bias_full)


def _sb_attn_kernel(qt_ref, k_ref, vt_ref, o_ref, acc_ref, r_ref, z0_ref, z1_ref, zc0_ref,
                    zc1_ref, t0_ref, t1_ref, bs_ref, kn_ref, *, tq, kb):
    i = pl.program_id(2)
    n_sub = tq // kb
    hd_dim = SB_HEAD_DIM
    z_refs, zc_refs, t_refs = (z0_ref, z1_ref), (zc0_ref, zc1_ref), (t0_ref, t1_ref)
    acc_refs = (acc_ref.at[0], acc_ref.at[1])
    first_head = lax.broadcasted_iota(jnp.int32, (LANES, 1), 0) < hd_dim
    qt = qt_ref[0]
    zero = jnp.zeros_like(qt)
    qt_heads = (jnp.where(first_head, qt, zero), jnp.where(first_head, zero, qt))
    visible_from = {
        lo: (lax.broadcasted_iota(jnp.int32, (kb, tq - lo), 0)
             < lax.broadcasted_iota(jnp.int32, (kb, tq - lo), 1))
        for lo in range(0, tq, kb)}
    srow = lax.broadcasted_iota(jnp.int32, (kb, kb), 0)
    scol = lax.broadcasted_iota(jnp.int32, (kb, kb), 1)
    suffix = (scol >= srow).astype(BF16)

    def score(item):
        g, hd, slot, lo = item
        start = pl.multiple_of(g * kb, kb)
        z_refs[slot][:, lo:] = _dot(k_ref[0, pl.ds(start, kb), :], qt_heads[hd][:, lo:])

    def stay(item, mask):
        g, hd, slot, lo = item
        z = z_refs[slot][:, lo:]
        sp = jnp.maximum(z, 0.0) + jnp.log2(1.0 + jnp.exp2(-jnp.abs(z)))
        if mask is not None:
            visible = visible_from[lo] if isinstance(mask, str) else mask
            sp = jnp.where(visible, sp, 0.0)
            z = jnp.where(visible, z, MASKED)
        zc_refs[slot][:, lo:] = z
        tail = _dot(suffix, sp.astype(BF16))
        t_refs[slot][:, lo:] = tail
        block_sum = jnp.broadcast_to(tail[0:1, :], (SUBLANES, tq - lo))
        if lo:
            block_sum = jnp.concatenate([jnp.zeros((SUBLANES, lo), F32), block_sum], axis=1)
        bs_ref[slot] = block_sum

    def weigh(item):
        g, hd, slot, lo = item
        r = r_ref[hd]
        wgt = jnp.exp2(zc_refs[slot][:, lo:] - t_refs[slot][:, lo:] - r[0:1, lo:])
        vt = vt_ref[0, g, hd * hd_dim:(hd + 1) * hd_dim, :]
        acc_refs[hd][:, lo:] += _dot(vt, wgt.astype(BF16))
        r_ref[hd] = r + bs_ref[slot]

    def block(g, lo, prev, g_next, next_lo, mask):
        score((g, 1, 1, lo))
        if prev is not None:
            weigh((prev[0], 1, 1, prev[1]))
        stay((g, 0, 0, lo), mask)
        score((g_next, 0, 0, next_lo))
        weigh((g, 0, 0, lo))
        stay((g, 1, 1, lo), mask)

    @pl.when(i == 0)
    def _():
        kf = k_ref[0].astype(F32)
        dim = lax.broadcasted_iota(jnp.int32, (LANES, LANES), 0)
        head = lax.broadcasted_iota(jnp.int32, (LANES, LANES), 1)
        select = ((dim < hd_dim) == (head == 0)) & (head < 2)
        kn_ref[...] = jnp.max(_dot((kf * kf).astype(BF16), select.astype(BF16)),
                              axis=0, keepdims=True)

    lane = lax.broadcasted_iota(jnp.int32, (1, LANES), 1)
    qf = qt.astype(F32)
    exit_level = []
    for hd in range(2):
        q_sq = jnp.sum(jnp.square(qf[hd * hd_dim:(hd + 1) * hd_dim, :]), axis=0, keepdims=True)
        k_sq = jnp.max(jnp.where(lane == hd, kn_ref[...], 0.0), axis=1, keepdims=True)
        exit_level.append(1.02 * jnp.sqrt(q_sq * k_sq) + UNDERFLOW_LOG2)

    acc_ref[...] = jnp.zeros_like(acc_ref)
    r_ref[...] = jnp.zeros_like(r_ref)
    first = i * n_sub
    score((first + n_sub - 1, 0, 0, (n_sub - 1) * kb))
    for d in reversed(range(n_sub)):
        block(first + d, d * kb, (first + d + 1, (d + 1) * kb) if d < n_sub - 1 else None,
              jnp.maximum(first + d - 1, 0), max(d - 1, 0) * kb, "diagonal")

    def still_live():
        dead = jnp.min(jnp.minimum(r_ref[0] - exit_level[0], r_ref[1] - exit_level[1])) > 0.0
        return jnp.where(dead, 0, 1).astype(jnp.int32)

    below = jnp.maximum(first - 1, 0)
    block(below, 0, (first, 0), jnp.maximum(below - 1, 0), 0, i > 0)

    def more(carry):
        g, live = carry
        return (g >= 0) & (live > 0)

    def body(carry):
        g, _ = carry
        block(g, 0, (g + 1, 0), jnp.maximum(g - 1, 0), 0, None)
        return g - 1, still_live()

    g_end, _ = lax.while_loop(more, body, (below - 1, still_live()))
    weigh((g_end + 1, 1, 1, 0))
    out_t = jnp.concatenate([acc_ref[0], acc_ref[1]], axis=0)
    o_ref[0] = out_t.T.astype(BF16)


def _sb_attention(qt, k, vt, *, tq):
    b, w, s = qt.shape
    kb = vt.shape[3]
    return pl.pallas_call(
        functools.partial(_sb_attn_kernel, tq=tq, kb=kb),
        grid=(b, w // LANES, s // tq),
        in_specs=[
            pl.BlockSpec((1, LANES, tq), lambda bi, p, i: (bi, p, i)),
            pl.BlockSpec((1, s, LANES), lambda bi, p, i: (bi, 0, p)),
            pl.BlockSpec((1, s // kb, LANES, kb), lambda bi, p, i: (bi, 0, p, 0)),
        ],
        out_specs=pl.BlockSpec((1, tq, LANES), lambda bi, p, i: (bi, i, p)),
        out_shape=jax.ShapeDtypeStruct((b, s, w), BF16),
        scratch_shapes=[pltpu.VMEM((2, SB_HEAD_DIM, tq), F32),
                        pltpu.VMEM((2, SUBLANES, tq), F32),
                        *[pltpu.VMEM((kb, tq), F32) for _ in range(6)],
                        pltpu.VMEM((2, SUBLANES, tq), F32), pltpu.VMEM((1, LANES), F32)],
        compiler_params=_params("parallel", "parallel", "arbitrary"),
        name="sb_attn",
    )(qt, k, vt)


def _lane_tile(t, width):
    return jnp.concatenate([t] * (width // t.shape[1]), axis=1)


def _rms_rows(xt, gain):
    ms = jnp.sum(xt * xt, axis=0, keepdims=True) * (1.0 / xt.shape[0])
    return xt * lax.rsqrt(ms + EPS) * _lane_tile(gain, xt.shape[1])


def _mla_prep_kernel(x_ref, pos_ref, g_ref, win_ref, qlg_ref, kvlg_ref, wuqt_ref, wukt_ref,
                     wuvt_ref, qg_ref, kg_ref, freq_ref, qt_ref, k_ref, vt_ref):
    tm = x_ref.shape[1]
    kb = vt_ref.shape[3]
    lat = MLA_Q_LORA + MLA_KV_LORA
    half = MLA_ROPE // 2
    h = _rms(x_ref[0], g_ref[...]).astype(BF16)
    ct = _dot(h, win_ref[...]).T
    cqn = _rms_rows(ct[0:MLA_Q_LORA], qlg_ref[...]).astype(BF16)
    ckvn = _rms_rows(ct[MLA_Q_LORA:lat], kvlg_ref[...]).astype(BF16)
    k_r = ct[lat:lat + MLA_ROPE]

    group = 4

    def up_project(gi):
        rows = slice(gi * group * MLA_QK, (gi + 1) * group * MLA_QK)
        krows = slice(gi * group * MLA_NOPE, (gi + 1) * group * MLA_NOPE)
        return _dot(wuqt_ref[rows, :], cqn), _dot(wukt_ref[krows, :], ckvn)

    angle = _lane_tile(freq_ref[...], tm) * pos_ref[0].astype(F32)
    cos = jnp.cos(angle)
    sin = jnp.sin(angle)

    def rope(t):
        t1, t2 = t[0:half], t[half:]
        return jnp.concatenate([t1 * cos - t2 * sin, t1 * sin + t2 * cos], axis=0)

    qg = _lane_tile(qg_ref[...], tm)
    kg = _lane_tile(kg_ref[...], tm)
    zeros = jnp.zeros((LANES - MLA_QK, tm), F32)
    inv_n = 1.0 / MLA_QK
    kr_sq = jnp.sum(k_r * k_r, axis=0, keepdims=True)
    kr_roped = rope(k_r * kg[MLA_NOPE:])
    n_groups = MLA_HEADS // group
    pending = up_project(0)
    for gi in range(n_groups):
        q_grp, kn_grp = pending
        if gi + 1 < n_groups:
            pending = up_project(gi + 1)
        else:
            vt = _dot(wuvt_ref[...], ckvn)
            for c in range(tm // kb):
                vt_ref[0, c] = vt[:, c * kb:(c + 1) * kb].astype(BF16)
        for sub in range(group):
            hd = gi * group + sub
            qh = q_grp[sub * MLA_QK:(sub + 1) * MLA_QK]
            r = lax.rsqrt(jnp.sum(qh * qh, axis=0, keepdims=True) * inv_n + EPS)
            qn = qh * r * qg
            qt_ref[0, hd] = jnp.concatenate([qn[0:MLA_NOPE], rope(qn[MLA_NOPE:]), zeros],
                                            axis=0).astype(BF16)
            kn = kn_grp[sub * MLA_NOPE:(sub + 1) * MLA_NOPE]
            r = lax.rsqrt((jnp.sum(kn * kn, axis=0, keepdims=True) + kr_sq) * inv_n + EPS)
            kt = jnp.concatenate([kn * r * kg[0:MLA_NOPE], kr_roped * r, zeros], axis=0)
            k_ref[0, hd] = kt.T.astype(BF16)


def _mla_prep(x, positions, gain, w_in, qlg, kvlg, wuqt, wukt, wuvt, qg, kg, freq, *, tm, kb):
    b, s, d = x.shape
    vw = MLA_HEADS * MLA_V
    const2 = lambda bi, i: (0, 0)
    full = lambda a: pl.BlockSpec(a.shape, const2)
    small = [gain.reshape(1, -1), w_in, qlg, kvlg, wuqt, wukt, wuvt, qg, kg, freq]
    return pl.pallas_call(
        _mla_prep_kernel,
        grid=(b, s // tm),
        in_specs=[
            pl.BlockSpec((1, tm, d), lambda bi, i: (bi, i, 0)),
            pl.BlockSpec((1, 1, tm), lambda bi, i: (bi, 0, i)),
            *[full(a) for a in small],
        ],
        out_specs=[pl.BlockSpec((1, MLA_HEADS, LANES, tm), lambda bi, i: (bi, 0, 0, i)),
                   pl.BlockSpec((1, MLA_HEADS, tm, LANES), lambda bi, i: (bi, 0, i, 0)),
                   pl.BlockSpec((1, tm // kb, vw, kb), lambda bi, i: (bi, i, 0, 0))],
        out_shape=[jax.ShapeDtypeStruct((b, MLA_HEADS, LANES, s), BF16),
                   jax.ShapeDtypeStruct((b, MLA_HEADS, s, LANES), BF16),
                   jax.ShapeDtypeStruct((b, s // kb, vw, kb), BF16)],
        compiler_params=_params("parallel", "parallel"),
        name="mla_prep",
    )(x, positions.reshape(b, 1, s), *small)


def _mla_attn_kernel(qt_ref, k_ref, vt_ref, bound_ref, o_ref, acc_ref, m_ref, s0_ref, s1_ref,
                     bm_ref, p0_ref, p1_ref, *, tq, fixed_shift):
    i = pl.program_id(2)
    n_heads = qt_ref.shape[1]
    s_refs, p_refs = (s0_ref, s1_ref), (p0_ref, p1_ref)
    key = lax.broadcasted_iota(jnp.int32, (tq, tq), 0)
    query = lax.broadcasted_iota(jnp.int32, (tq, tq), 1)
    causal = key <= query
    ones_rows = jnp.ones((SUM_ROWS, tq), BF16)

    def scores(g, hd, masked):
        start = pl.multiple_of(g * tq, tq)
        sc = _dot(k_ref[0, hd, pl.ds(start, tq), :], qt_ref[0, hd])
        return jnp.where(causal, sc, MASKED) if masked else sc

    def values(g, hd):
        return jnp.concatenate([vt_ref[0, g, hd * MLA_V:(hd + 1) * MLA_V, :], ones_rows], axis=0)

    def sweep(produce, consume):
        def visit(g, g_next, masked):
            for hd in range(n_heads):
                if hd + 1 < n_heads:
                    produce(g, hd + 1, masked)
                else:
                    produce(g_next, 0, False)
                consume(g, hd)

        produce(i, 0, True)
        visit(i, 0, True)

        def body(it, carry):
            visit(2 * it, 2 * it + 1, False)
            visit(2 * it + 1, jnp.minimum(2 * it + 2, i - 1), False)
            return carry

        lax.fori_loop(0, i // 2, body, 0)

        @pl.when(i % 2 == 1)
        def _():
            visit(i - 1, i - 1, False)

    acc_ref[...] = jnp.zeros_like(acc_ref)

    if fixed_shift:
        bound = bound_ref[:, 0:1]

        def produce(g, hd, masked):
            p_refs[hd % 2][...] = jnp.exp2(scores(g, hd, masked) - bound).astype(BF16)

        def consume(g, hd):
            acc_ref[hd] += _dot(values(g, hd), p_refs[hd % 2][...])
    else:
        m_ref[...] = jnp.full_like(m_ref, MASKED)

        def produce(g, hd, masked):
            sc = scores(g, hd, masked)
            s_refs[hd % 2][...] = sc
            bm_ref[hd % 2] = jnp.max(sc, axis=0, keepdims=True)

        def consume(g, hd):
            m_old = m_ref[hd]
            m_new = jnp.maximum(m_old, bm_ref[hd % 2])
            alpha = jnp.exp2(m_old - m_new)
            p = jnp.exp2(s_refs[hd % 2][...] - m_new)
            acc_ref[hd] = alpha * acc_ref[hd] + _dot(values(g, hd), p.astype(BF16))
            m_ref[hd] = m_new

    sweep(produce, consume)

    out_t = jnp.concatenate([acc_ref[hd, 0:MLA_V, :] / acc_ref[hd, MLA_V:MLA_V + 1, :]
                             for hd in range(n_heads)], axis=0)
    o_ref[0] = out_t.T.astype(BF16)


def _mla_attention(qt, k, vt, score_bound, *, tq, heads_per_step):
    b, heads, _, s = qt.shape
    kb = vt.shape[3]
    nh = heads_per_step
    assert kb == tq and nh % 2 == 0
    vw = nh * MLA_V
    bound_lanes = jnp.full((1, LANES), score_bound, F32)

    def call(fixed_shift):
        return pl.pallas_call(
            functools.partial(_mla_attn_kernel, tq=tq, fixed_shift=fixed_shift),
            grid=(b, heads // nh, s // tq),
            in_specs=[
                pl.BlockSpec((1, nh, LANES, tq), lambda bi, p, i: (bi, p, 0, i)),
                pl.BlockSpec((1, nh, s, LANES), lambda bi, p, i: (bi, p, 0, 0)),
                pl.BlockSpec((1, s // kb, vw, kb), lambda bi, p, i: (bi, 0, p, 0)),
                pl.BlockSpec((1, LANES), lambda bi, p, i: (0, 0)),
            ],
            out_specs=pl.BlockSpec((1, tq, vw), lambda bi, p, i: (bi, i, p)),
            out_shape=jax.ShapeDtypeStruct((b, s, heads * MLA_V), BF16),
            scratch_shapes=[pltpu.VMEM((nh, MLA_V + SUM_ROWS, tq), F32),
                            pltpu.VMEM((nh, 1, tq), F32), pltpu.VMEM((kb, tq), F32),
                            pltpu.VMEM((kb, tq), F32), pltpu.VMEM((2, 1, tq), F32),
                            pltpu.VMEM((kb, tq), BF16), pltpu.VMEM((kb, tq), BF16)],
            compiler_params=_params("parallel", "parallel", "arbitrary"),
            name="mla_attn_fixed_shift" if fixed_shift else "mla_attn_online_max",
        )(qt, k, vt, bound_lanes)

    return lax.cond(score_bound <= FIXED_SHIFT_MAX, lambda: call(True), lambda: call(False))


def _mem_kv_kernel(mem_ref, g_ref, wkv_ref, kg_ref, k_ref, v_ref):
    hm = _rms(mem_ref[0], g_ref[...]).astype(BF16)
    hd_dim = kg_ref.shape[1]
    for hd in range(MEM_HEADS):
        kcols = slice(2 * hd * hd_dim, (2 * hd + 1) * hd_dim)
        vcols = slice((2 * hd + 1) * hd_dim, (2 * hd + 2) * hd_dim)
        out = slice(hd * hd_dim, (hd + 1) * hd_dim)
        k_ref[0, :, out] = _rms(_dot(hm, wkv_ref[:, kcols]), kg_ref[...]).astype(BF16)
        v_ref[0, :, out] = _dot(hm, wkv_ref[:, vcols]).astype(BF16)


def _mem_kv(mem, gain, wkv, k_gain):
    b, m, d = mem.shape
    out = jax.ShapeDtypeStruct((b, m, d), BF16)
    spec = pl.BlockSpec((1, m, d), lambda bi: (bi, 0, 0))
    return pl.pallas_call(
        _mem_kv_kernel,
        grid=(b,),
        in_specs=[spec,
                  pl.BlockSpec((1, d), lambda bi: (0, 0)),
                  pl.BlockSpec(wkv.shape, lambda bi: (0, 0)),
                  pl.BlockSpec((1, k_gain.shape[0]), lambda bi: (0, 0))],
        out_specs=[spec, spec],
        out_shape=[out, out],
        compiler_params=_params("parallel"),
        name="mem_kv",
    )(mem, gain.reshape(1, d), wkv, k_gain.reshape(1, -1))


def _mix_xattn_kernel(*refs, n_act):
    x_ref = refs[0]
    act_refs = refs[1:1 + n_act]
    w_ref, g_ref, wq_ref, qg_ref, k_ref, v_ref, wo_ref, o_ref = refs[1 + n_act:]
    act = jnp.concatenate([a_ref[0] for a_ref in act_refs], axis=-1)
    x1 = x_ref[0] + _dot(act, w_ref[...])
    h = _rms(x1, g_ref[...]).astype(BF16)
    hd_dim = qg_ref.shape[1]
    cols = [slice(hd * hd_dim, (hd + 1) * hd_dim) for hd in range(MEM_HEADS)]
    q = [_dot(h, wq_ref[:, c]) for c in cols]
    qn = [(_rms(q_h, qg_ref[...]) * (hd_dim ** -0.5)).astype(BF16) for q_h in q]
    sc = [_dot_nt(qn_h, k_ref[0, :, c]) for qn_h, c in zip(qn, cols)]
    p = [jnp.exp(s_h - jnp.max(s_h, axis=-1, keepdims=True)) for s_h in sc]
    o = [(_dot(p_h.astype(BF16), v_ref[0, :, c]) / jnp.sum(p_h, axis=-1, keepdims=True)
          ).astype(BF16) for p_h, c in zip(p, cols)]
    o_ref[0] = x1 + _dot(jnp.concatenate(o, axis=-1), wo_ref[...])


def _mix_xattn(x, acts, w_mix, gain, wq, q_gain, mem_k, mem_v, wo, *, tm):
    b, s, d = x.shape
    m = mem_k.shape[1]
    const2 = lambda bi, i: (0, 0)
    row = lambda width: pl.BlockSpec((1, tm, width), lambda bi, i: (bi, i, 0))
    mem_spec = pl.BlockSpec((1, m, d), lambda bi, i: (bi, 0, 0))
    return pl.pallas_call(
        functools.partial(_mix_xattn_kernel, n_act=len(acts)),
        grid=(b, s // tm),
        in_specs=[
            row(d),
            *[row(a.shape[-1]) for a in acts],
            pl.BlockSpec(w_mix.shape, const2),
            pl.BlockSpec((1, d), const2),
            pl.BlockSpec(wq.shape, const2),
            pl.BlockSpec((1, q_gain.shape[0]), const2),
            mem_spec, mem_spec,
            pl.BlockSpec(wo.shape, const2),
        ],
        out_specs=row(d),
        out_shape=jax.ShapeDtypeStruct((b, s, d), F32),
        compiler_params=_params("parallel", "parallel"),
        name="mix_xattn",
    )(x, *acts, w_mix, gain.reshape(1, d), wq, q_gain.reshape(1, -1), mem_k, mem_v, wo)


def _lane_bcast(vec):
    return jnp.broadcast_to(vec[:, None], (vec.shape[0], LANES))


def _mla_weights(w_in, w_uq, w_ukv, q_lora_gain, kv_lora_gain, q_gain, k_gain):
    lat = MLA_Q_LORA + MLA_KV_LORA
    w_in_ext = jnp.pad(w_in, ((0, 0), (0, lat + LANES - w_in.shape[1])))
    wukv = w_ukv.reshape(MLA_KV_LORA, MLA_HEADS, MLA_NOPE + MLA_V)
    wukt = wukv[..., :MLA_NOPE].reshape(MLA_KV_LORA, -1).T
    wuvt = wukv[..., MLA_NOPE:].reshape(MLA_KV_LORA, -1).T
    half = MLA_ROPE // 2
    inv_freq = ROPE_THETA ** (-jnp.arange(half, dtype=F32) / half)
    bf = lambda a: a.astype(BF16)
    return (bf(w_in_ext), _lane_bcast(q_lora_gain), _lane_bcast(kv_lora_gain),
            bf(w_uq.T), bf(wukt), bf(wuvt),
            _lane_bcast(q_gain * (MLA_QK ** -0.5 * LOG2E)), _lane_bcast(k_gain),
            _lane_bcast(inv_freq))


def _tile(n, pref):
    return pref if n % pref == 0 else n


def kernel(x, mem, positions, ffn_pre_norm, ffn_pre_w_gu, ffn_pre_w_down, mix_norm, sbg_w_in, sgu_ln_gain, sgu_ln_bias, sgu_w, sgu_b, sbg_w_out, mla_w_in, mla_q_lora_gain, mla_kv_lora_gain, mla_w_uq, mla_w_ukv, mla_q_gain, mla_k_gain, mla_w_out, xmem_norm, xmem_mem_norm, xmem_wq, xmem_wkv, xmem_q_gain, xmem_k_gain, xmem_wo, ffn_post_norm, ffn_post_w_gu, ffn_post_w_down):
    b, s, d = x.shape
    depth = ffn_pre_norm.shape[0]
    d_ff = ffn_pre_w_down.shape[1]
    t = b * s
    ffn_tm = _tile(t, 512)
    ffn_tf = _tile(d_ff, 256)
    row_tm = _tile(s, 512)
    mla_tq = _tile(s, 512)
    sb_tq = _tile(s, 512)
    sb_kb = _tile(sb_tq, 256)
    bf = lambda a: a.astype(BF16)

    for layer in range(depth):
        x = _ffn(x.reshape(t, d), ffn_pre_norm[layer], ffn_pre_w_gu, ffn_pre_w_down, layer,
                 tm=ffn_tm, tf=ffn_tf).reshape(b, s, d)
        if layer % 2 == 0:
            e = layer // 2
            bias_full = jnp.repeat(sgu_b[e].T, SG_GROUP_DIM, axis=1)
            qt, k, vt, o_sg = _even_prep(
                x, mix_norm[layer], bf(sbg_w_in[e]), sgu_ln_gain[e],
                sgu_ln_bias[e], sgu_w[e], bias_full, tm=row_tm, kb=sb_kb)
            acts = (_sb_attention(qt, k, vt, tq=sb_tq), o_sg)
            w_mix = bf(sbg_w_out[e])
        else:
            o = layer // 2
            mla_consts = _mla_weights(
                mla_w_in[o], mla_w_uq[o], mla_w_ukv[o], mla_q_lora_gain[o],
                mla_kv_lora_gain[o], mla_q_gain[o], mla_k_gain[o])
            qt, k, vt = _mla_prep(x, positions, mix_norm[layer], *mla_consts,
                                  tm=row_tm, kb=mla_tq)
            score_bound = (1.02 * MLA_QK * (MLA_QK ** -0.5 * LOG2E)
                           * jnp.max(jnp.abs(mla_q_gain[o])) * jnp.max(jnp.abs(mla_k_gain[o])))
            acts = (_mla_attention(qt, k, vt, score_bound, tq=mla_tq, heads_per_step=4),)
            w_mix = bf(mla_w_out[o])
        mem_k, mem_v = _mem_kv(mem, xmem_mem_norm[layer], bf(xmem_wkv[layer]),
                               xmem_k_gain[layer])
        x = _mix_xattn(x, acts, w_mix, xmem_norm[layer], bf(xmem_wq[layer]), xmem_q_gain[layer],
                       mem_k, mem_v, bf(xmem_wo[layer]), tm=row_tm)
        x = _ffn(x.reshape(t, d), ffn_post_norm[layer], ffn_post_w_gu, ffn_post_w_down, layer,
                 tm=ffn_tm, tf=ffn_tf).reshape(b, s, d)
    return x
```

```python
import functools

import jax
import jax.numpy as jnp
from jax import lax
from jax.experimental import pallas as pl
from jax.experimental.pallas import tpu as pltpu

EPS = 1e-6
ROPE_THETA = 10000.0
LANES = 128
SUBLANES = 8
VMEM_LIMIT_BYTES = 56 * 1024 * 1024

SB_HEADS, SB_HEAD_DIM = 8, 64
SB_WIDTH = SB_HEADS * SB_HEAD_DIM
SG_GROUPS, SG_GROUP_DIM, SG_CHUNK = 8, 64, 128
SG_WIDTH = SG_GROUPS * SG_GROUP_DIM
MLA_HEADS, MLA_NOPE, MLA_ROPE, MLA_V = 16, 64, 32, 64
MLA_QK = MLA_NOPE + MLA_ROPE
MLA_Q_LORA, MLA_KV_LORA = 512, 256
MEM_HEADS = 4

BF16 = jnp.bfloat16
F32 = jnp.float32
LOG2E = 1.4426950408889634
MASKED = -1e30
SUM_ROWS = 16
FIXED_SHIFT_MAX = 60.0
UNDERFLOW_LOG2 = 160.0


def _params(*semantics):
    return pltpu.CompilerParams(dimension_semantics=semantics,
                                vmem_limit_bytes=VMEM_LIMIT_BYTES)


def _dot(a, b):
    return jnp.dot(a, b, preferred_element_type=F32)


def _dot_nt(a, b):
    return lax.dot_general(a, b, (((1,), (1,)), ((), ())), preferred_element_type=F32)


def _rms(x, gain, n=None):
    n = x.shape[-1] if n is None else n
    ms = jnp.sum(x * x, axis=-1, keepdims=True) * (1.0 / n)
    return x * lax.rsqrt(ms + EPS) * gain


def _ffn_kernel(x_ref, g_ref, wgu_ref, wd_ref, o_ref, *, tf):
    d_ff = wd_ref.shape[0]
    x = x_ref[...]
    h = _rms(x, g_ref[...]).astype(BF16)
    acc = None
    for c in range(d_ff // tf):
        cols = slice(c * tf, (c + 1) * tf)
        gate = _dot(h, wgu_ref[:, cols].astype(BF16))
        up = _dot(h, wgu_ref[:, d_ff + c * tf:d_ff + (c + 1) * tf].astype(BF16))
        act = (gate * jax.nn.sigmoid(gate) * up).astype(BF16)
        part = _dot(act, wd_ref[cols, :].astype(BF16))
        acc = part if acc is None else acc + part
    o_ref[...] = x + 0.5 * acc


def _ffn(x2, gain, w_gu, w_down, layer, *, tm, tf):
    t, d = x2.shape
    d_ff = w_down.shape[1]
    resident = pl.Buffered(1)
    return pl.pallas_call(
        functools.partial(_ffn_kernel, tf=tf),
        grid=(t // tm,),
        in_specs=[
            pl.BlockSpec((tm, d), lambda i: (i, 0)),
            pl.BlockSpec((1, d), lambda i: (0, 0)),
            pl.BlockSpec((None, d, 2 * d_ff), lambda i: (layer, 0, 0), pipeline_mode=resident),
            pl.BlockSpec((None, d_ff, d), lambda i: (layer, 0, 0), pipeline_mode=resident),
        ],
        out_specs=pl.BlockSpec((tm, d), lambda i: (i, 0)),
        out_shape=jax.ShapeDtypeStruct((t, d), F32),
        compiler_params=_params("parallel"),
        name="ffn",
    )(x2, gain.reshape(1, d), w_gu, w_down)


def _gelu_tanh(x):
    c = 0.7978845608028654
    return 0.5 * x * (1.0 + jnp.tanh(c * (x + 0.044715 * (x * x * x))))


def _even_prep_kernel(x_ref, g_ref, win_ref, lng_ref, lnb_ref, sw_ref, sb_ref,
                      qt_ref, k_ref, vt_ref, osg_ref):
    tm = x_ref.shape[1]
    kb = vt_ref.shape[3]
    w = SB_WIDTH
    h = _rms(x_ref[0], g_ref[...]).astype(BF16)
    osg_ref = osg_ref.at[0]
    u_raw = _dot(h, win_ref[:, 3 * w:3 * w + SG_WIDTH])
    g_raw = _dot(h, win_ref[:, 3 * w + SG_WIDTH:3 * w + 2 * SG_WIDTH])
    qt_ref[0] = (_dot(h, win_ref[:, 0:w]) * (SB_HEAD_DIM ** -0.5 * LOG2E)).T.astype(BF16)
    u = _gelu_tanh(u_raw)
    k_ref[0] = _dot(h, win_ref[:, w:2 * w]).astype(BF16)
    g = _gelu_tanh(g_raw)
    mu = jnp.mean(g, axis=-1, keepdims=True)
    gc = g - mu
    var = jnp.mean(gc * gc, axis=-1, keepdims=True)
    gn = (gc * lax.rsqrt(var + EPS) * lng_ref[...] + lnb_ref[...]).astype(BF16)
    v = _dot(h, win_ref[:, 2 * w:3 * w])
    for c in range(tm // kb):
        vt_ref[0, c] = v[c * kb:(c + 1) * kb, :].T.astype(BF16)

    row = lax.broadcasted_iota(jnp.int32, (SG_CHUNK, SG_CHUNK), 0)
    col = lax.broadcasted_iota(jnp.int32, (SG_CHUNK, SG_CHUNK), 1)
    tri = col <= row
    first_group = lax.broadcasted_iota(jnp.int32, (SG_CHUNK, LANES), 1) < SG_GROUP_DIM
    for p in range(SG_GROUPS // 2):
        lanes = slice(p * LANES, (p + 1) * LANES)
        w0 = jnp.where(tri, sw_ref[2 * p], 0.0).astype(BF16)
        w1 = jnp.where(tri, sw_ref[2 * p + 1], 0.0).astype(BF16)
        bias = sb_ref[:, lanes]
        for c in range(tm // SG_CHUNK):
            rows = slice(c * SG_CHUNK, (c + 1) * SG_CHUNK)
            gp = gn[rows, lanes]
            mixed = jnp.where(first_group, _dot(w0, gp), _dot(w1, gp)) + bias
            osg_ref[rows, lanes] = (u[rows, lanes] * mixed).astype(BF16)


def _even_prep(x, gain, w_in, ln_g, ln_b, sgu_w, sgu_bias_full, *, tm, kb):
    b, s, d = x.shape
    n_in = w_in.shape[1]
    w = SB_WIDTH
    const2 = lambda bi, i: (0, 0)
    row_out = jax.ShapeDtypeStruct((b, s, w), BF16)
    row_spec = pl.BlockSpec((1, tm, w), lambda bi, i: (bi, i, 0))
    return pl.pallas_call(
        _even_prep_kernel,
        grid=(b, s // tm),
        in_specs=[
            pl.BlockSpec((1, tm, d), lambda bi, i: (bi, i, 0)),
            pl.BlockSpec((1, d), const2),
            pl.BlockSpec((d, n_in), const2),
            pl.BlockSpec((1, SG_WIDTH), const2),
            pl.BlockSpec((1, SG_WIDTH), const2),
            pl.BlockSpec((SG_GROUPS, SG_CHUNK, SG_CHUNK), lambda bi, i: (0, 0, 0)),
            pl.BlockSpec((SG_CHUNK, SG_WIDTH), const2),
        ],
        out_specs=[pl.BlockSpec((1, w, tm), lambda bi, i: (bi, 0, i)),
                   row_spec,
                   pl.BlockSpec((1, tm // kb, w, kb), lambda bi, i: (bi, i, 0, 0)),
                   row_spec],
        out_shape=[jax.ShapeDtypeStruct((b, w, s), BF16), row_out,
                   jax.ShapeDtypeStruct((b, s // kb, w, kb), BF16), row_out],
        compiler_params=_params("parallel", "parallel"),
        name="even_prep",
    )(x, gain.reshape(1, d), w_in, ln_g.reshape(1, -1), ln_b.reshape(1, -1),
      sgu_w, sgu_bias_full)


def _sb_attn_kernel(qt_ref, k_ref, vt_ref, o_ref, acc_ref, r_ref, z0_ref, z1_ref, zc0_ref,
                    zc1_ref, t0_ref, t1_ref, bs_ref, kn_ref, *, tq, kb):
    i = pl.program_id(2)
    n_sub = tq // kb
    hd_dim = SB_HEAD_DIM
    z_refs, zc_refs, t_refs = (z0_ref, z1_ref), (zc0_ref, zc1_ref), (t0_ref, t1_ref)
    acc_refs = (acc_ref.at[0], acc_ref.at[1])
    first_head = lax.broadcasted_iota(jnp.int32, (LANES, 1), 0) < hd_dim
    qt = qt_ref[0]
    zero = jnp.zeros_like(qt)
    qt_heads = (jnp.where(first_head, qt, zero), jnp.where(first_head, zero, qt))
    visible_from = {
        lo: (lax.broadcasted_iota(jnp.int32, (kb, tq - lo), 0)
             < lax.broadcasted_iota(jnp.int32, (kb, tq - lo), 1))
        for lo in range(0, tq, kb)}
    srow = lax.broadcasted_iota(jnp.int32, (kb, kb), 0)
    scol = lax.broadcasted_iota(jnp.int32, (kb, kb), 1)
    suffix = (scol >= srow).astype(BF16)

    not_first_window = lax.broadcasted_iota(jnp.int32, (kb, tq), 1) >= kb

    def score(item):
        g, hd, slot, lo, hi = item
        start = pl.multiple_of(g * kb, kb)
        z_refs[slot][:, lo:hi] = _dot(k_ref[0, pl.ds(start, kb), :], qt_heads[hd][:, lo:hi])

    def stay(item, mask):
        g, hd, slot, lo, hi = item
        z = z_refs[slot][:, lo:hi]
        sp = jnp.maximum(z, 0.0) + jnp.log2(1.0 + jnp.exp2(-jnp.abs(z)))
        if mask is not None:
            if isinstance(mask, str):
                visible = visible_from[lo] if mask == "diagonal" else not_first_window
            else:
                visible = mask
            sp = jnp.where(visible, sp, 0.0)
            z = jnp.where(visible, z, MASKED)
        zc_refs[slot][:, lo:hi] = z
        tail = _dot(suffix, sp.astype(BF16))
        t_refs[slot][:, lo:hi] = tail
        parts = [jnp.broadcast_to(tail[0:1, :], (SUBLANES, hi - lo))]
        if lo:
            parts.insert(0, jnp.zeros((SUBLANES, lo), F32))
        if hi < tq:
            parts.append(jnp.zeros((SUBLANES, tq - hi), F32))
        bs_ref[slot] = jnp.concatenate(parts, axis=1) if len(parts) > 1 else parts[0]

    def weigh(item):
        g, hd, slot, lo, hi = item
        r = r_ref[hd]
        wgt = jnp.exp2(zc_refs[slot][:, lo:hi] - t_refs[slot][:, lo:hi] - r[0:1, lo:hi])
        vt = vt_ref[0, g, hd * hd_dim:(hd + 1) * hd_dim, :]
        acc_refs[hd][:, lo:hi] += _dot(vt, wgt.astype(BF16))
        r_ref[hd] = r + bs_ref[slot]

    def block(g, window, prev, nxt, mask):
        score((g, 1, 1, *window))
        if prev is not None:
            weigh((prev[0], 1, 1, *prev[1]))
        stay((g, 0, 0, *window), mask)
        if nxt is not None:
            score((nxt[0], 0, 0, *nxt[1]))
        weigh((g, 0, 0, *window))
        stay((g, 1, 1, *window), mask)

    @pl.when(i == 0)
    def _():
        kf = k_ref[0].astype(F32)
        dim = lax.broadcasted_iota(jnp.int32, (LANES, LANES), 0)
        head = lax.broadcasted_iota(jnp.int32, (LANES, LANES), 1)
        select = ((dim < hd_dim) == (head == 0)) & (head < 2)
        kn_ref[...] = jnp.max(_dot((kf * kf).astype(BF16), select.astype(BF16)),
                              axis=0, keepdims=True)

    lane = lax.broadcasted_iota(jnp.int32, (1, LANES), 1)
    qf = qt.astype(F32)
    exit_level = []
    for hd in range(2):
        q_sq = jnp.sum(jnp.square(qf[hd * hd_dim:(hd + 1) * hd_dim, :]), axis=0, keepdims=True)
        k_sq = jnp.max(jnp.where(lane == hd, kn_ref[...], 0.0), axis=1, keepdims=True)
        exit_level.append(1.02 * jnp.sqrt(q_sq * k_sq) + UNDERFLOW_LOG2)

    acc_ref[...] = jnp.zeros_like(acc_ref)
    r_ref[...] = jnp.zeros_like(r_ref)
    first = i * n_sub
    full = (0, tq)
    below = jnp.maximum(first - 1, 0)
    score((first + n_sub - 1, 0, 0, (n_sub - 1) * kb, tq))
    for d in reversed(range(n_sub)):
        prev = (first + d + 1, ((d + 1) * kb, tq)) if d < n_sub - 1 else None
        nxt = (first + d - 1, ((d - 1) * kb, tq)) if d > 0 else (below, (0, kb))
        block(first + d, (d * kb, tq), prev, nxt, "diagonal")

    def still_live():
        dead = jnp.min(jnp.minimum(r_ref[0] - exit_level[0], r_ref[1] - exit_level[1])) > 0.0
        return jnp.where(dead, 0, 1).astype(jnp.int32)

    block(below, (0, kb), (first, full), None, i > 0)
    weigh((below, 1, 1, 0, kb))

    @pl.when((still_live() > 0) & (i > 0))
    def _():
        score((below, 0, 0, *full))
        block(below, full, None, (jnp.maximum(below - 1, 0), full), "later windows")

        def more(carry):
            g, live = carry
            return (g >= 0) & (live > 0)

        def body(carry):
            g, _ = carry
            block(g, full, (g + 1, full), (jnp.maximum(g - 1, 0), full), None)
            return g - 1, still_live()

        g_end, _ = lax.while_loop(more, body, (below - 1, still_live()))
        weigh((g_end + 1, 1, 1, *full))

    out_t = jnp.concatenate([acc_ref[0], acc_ref[1]], axis=0)
    o_ref[0] = out_t.T.astype(BF16)


def _sb_attention(qt, k, vt, *, tq):
    b, w, s = qt.shape
    kb = vt.shape[3]
    return pl.pallas_call(
        functools.partial(_sb_attn_kernel, tq=tq, kb=kb),
        grid=(b, w // LANES, s // tq),
        in_specs=[
            pl.BlockSpec((1, LANES, tq), lambda bi, p, i: (bi, p, i)),
            pl.BlockSpec((1, s, LANES), lambda bi, p, i: (bi, 0, p)),
            pl.BlockSpec((1, s // kb, LANES, kb), lambda bi, p, i: (bi, 0, p, 0)),
        ],
        out_specs=pl.BlockSpec((1, tq, LANES), lambda bi, p, i: (bi, i, p)),
        out_shape=jax.ShapeDtypeStruct((b, s, w), BF16),
        scratch_shapes=[pltpu.VMEM((2, SB_HEAD_DIM, tq), F32),
                        pltpu.VMEM((2, SUBLANES, tq), F32),
                        *[pltpu.VMEM((kb, tq), F32) for _ in range(6)],
                        pltpu.VMEM((2, SUBLANES, tq), F32), pltpu.VMEM((1, LANES), F32)],
        compiler_params=_params("parallel", "parallel", "arbitrary"),
        name="sb_attn",
    )(qt, k, vt)


def _lane_tile(t, width):
    return jnp.concatenate([t] * (width // t.shape[1]), axis=1)


def _rms_rows(xt, gain):
    ms = jnp.sum(xt * xt, axis=0, keepdims=True) * (1.0 / xt.shape[0])
    return xt * lax.rsqrt(ms + EPS) * _lane_tile(gain, xt.shape[1])


def _mla_prep_kernel(x_ref, pos_ref, g_ref, win_ref, qlg_ref, kvlg_ref, wuqt_ref, wukt_ref,
                     wuvt_ref, qg_ref, kg_ref, freq_ref, qt_ref, k_ref, vt_ref):
    tm = x_ref.shape[1]
    kb = vt_ref.shape[3]
    lat = MLA_Q_LORA + MLA_KV_LORA
    half = MLA_ROPE // 2
    h = _rms(x_ref[0], g_ref[...]).astype(BF16)
    ct = _dot(h, win_ref[...]).T
    cqn = _rms_rows(ct[0:MLA_Q_LORA], qlg_ref[...]).astype(BF16)
    ckvn = _rms_rows(ct[MLA_Q_LORA:lat], kvlg_ref[...]).astype(BF16)
    k_r = ct[lat:lat + MLA_ROPE]

    group = 4

    def up_project(gi):
        rows = slice(gi * group * MLA_QK, (gi + 1) * group * MLA_QK)
        krows = slice(gi * group * MLA_NOPE, (gi + 1) * group * MLA_NOPE)
        return _dot(wuqt_ref[rows, :], cqn), _dot(wukt_ref[krows, :], ckvn)

    angle = _lane_tile(freq_ref[...], tm) * pos_ref[0].astype(F32)
    cos = jnp.cos(angle)
    sin = jnp.sin(angle)

    def rope(t):
        t1, t2 = t[0:half], t[half:]
        return jnp.concatenate([t1 * cos - t2 * sin, t1 * sin + t2 * cos], axis=0)

    qg = _lane_tile(qg_ref[...], tm)
    kg = _lane_tile(kg_ref[...], tm)
    zeros = jnp.zeros((LANES - MLA_QK, tm), F32)
    inv_n = 1.0 / MLA_QK
    kr_sq = jnp.sum(k_r * k_r, axis=0, keepdims=True)
    kr_roped = rope(k_r * kg[MLA_NOPE:])
    n_groups = MLA_HEADS // group
    pending = up_project(0)
    for gi in range(n_groups):
        q_grp, kn_grp = pending
        if gi + 1 < n_groups:
            pending = up_project(gi + 1)
        else:
            vt = _dot(wuvt_ref[...], ckvn)
            for c in range(tm // kb):
                vt_ref[0, c] = vt[:, c * kb:(c + 1) * kb].astype(BF16)
        for sub in range(group):
            hd = gi * group + sub
            qh = q_grp[sub * MLA_QK:(sub + 1) * MLA_QK]
            r = lax.rsqrt(jnp.sum(qh * qh, axis=0, keepdims=True) * inv_n + EPS)
            qn = qh * r * qg
            qt_ref[0, hd] = jnp.concatenate([qn[0:MLA_NOPE], rope(qn[MLA_NOPE:]), zeros],
                                            axis=0).astype(BF16)
            kn = kn_grp[sub * MLA_NOPE:(sub + 1) * MLA_NOPE]
            r = lax.rsqrt((jnp.sum(kn * kn, axis=0, keepdims=True) + kr_sq) * inv_n + EPS)
            kt = jnp.concatenate([kn * r * kg[0:MLA_NOPE], kr_roped * r, zeros], axis=0)
            k_ref[0, hd] = kt.T.astype(BF16)


def _mla_prep(x, positions, gain, w_in, qlg, kvlg, wuqt, wukt, wuvt, qg, kg, freq, *, tm, kb):
    b, s, d = x.shape
    vw = MLA_HEADS * MLA_V
    const2 = lambda bi, i: (0, 0)
    full = lambda a: pl.BlockSpec(a.shape, const2)
    small = [gain.reshape(1, -1), w_in, qlg, kvlg, wuqt, wukt, wuvt, qg, kg, freq]
    return pl.pallas_call(
        _mla_prep_kernel,
        grid=(b, s // tm),
        in_specs=[
            pl.BlockSpec((1, tm, d), lambda bi, i: (bi, i, 0)),
            pl.BlockSpec((1, 1, tm), lambda bi, i: (bi, 0, i)),
            *[full(a) for a in small],
        ],
        out_specs=[pl.BlockSpec((1, MLA_HEADS, LANES, tm), lambda bi, i: (bi, 0, 0, i)),
                   pl.BlockSpec((1, MLA_HEADS, tm, LANES), lambda bi, i: (bi, 0, i, 0)),
                   pl.BlockSpec((1, tm // kb, vw, kb), lambda bi, i: (bi, i, 0, 0))],
        out_shape=[jax.ShapeDtypeStruct((b, MLA_HEADS, LANES, s), BF16),
                   jax.ShapeDtypeStruct((b, MLA_HEADS, s, LANES), BF16),
                   jax.ShapeDtypeStruct((b, s // kb, vw, kb), BF16)],
        compiler_params=_params("parallel", "parallel"),
        name="mla_prep",
    )(x, positions.reshape(b, 1, s), *small)


def _mla_attn_kernel(qt_ref, k_ref, vt_ref, bound_ref, o_ref, acc_ref, m_ref, s0_ref, s1_ref,
                     bm_ref, p0_ref, p1_ref, *, tq, fixed_shift):
    i = pl.program_id(2)
    n_heads = qt_ref.shape[1]
    s_refs, p_refs = (s0_ref, s1_ref), (p0_ref, p1_ref)
    key = lax.broadcasted_iota(jnp.int32, (tq, tq), 0)
    query = lax.broadcasted_iota(jnp.int32, (tq, tq), 1)
    causal = key <= query
    ones_rows = jnp.ones((SUM_ROWS, tq), BF16)

    def scores(g, hd, masked):
        start = pl.multiple_of(g * tq, tq)
        sc = _dot(k_ref[0, hd, pl.ds(start, tq), :], qt_ref[0, hd])
        return jnp.where(causal, sc, MASKED) if masked else sc

    def values(g, hd):
        return jnp.concatenate([vt_ref[0, g, hd * MLA_V:(hd + 1) * MLA_V, :], ones_rows], axis=0)

    def sweep(produce, consume):
        def visit(g, g_next, masked):
            for hd in range(n_heads):
                if hd + 1 < n_heads:
                    produce(g, hd + 1, masked)
                else:
                    produce(g_next, 0, False)
                consume(g, hd)

        produce(i, 0, True)
        visit(i, 0, True)

        def body(it, carry):
            visit(2 * it, 2 * it + 1, False)
            visit(2 * it + 1, jnp.minimum(2 * it + 2, i - 1), False)
            return carry

        lax.fori_loop(0, i // 2, body, 0)

        @pl.when(i % 2 == 1)
        def _():
            visit(i - 1, i - 1, False)

    acc_ref[...] = jnp.zeros_like(acc_ref)

    if fixed_shift:
        bound = bound_ref[:, 0:1]

        def produce(g, hd, masked):
            p_refs[hd % 2][...] = jnp.exp2(scores(g, hd, masked) - bound).astype(BF16)

        def consume(g, hd):
            acc_ref[hd] += _dot(values(g, hd), p_refs[hd % 2][...])
    else:
        m_ref[...] = jnp.full_like(m_ref, MASKED)

        def produce(g, hd, masked):
            sc = scores(g, hd, masked)
            s_refs[hd % 2][...] = sc
            bm_ref[hd % 2] = jnp.max(sc, axis=0, keepdims=True)

        def consume(g, hd):
            m_old = m_ref[hd]
            m_new = jnp.maximum(m_old, bm_ref[hd % 2])
            alpha = jnp.exp2(m_old - m_new)
            p = jnp.exp2(s_refs[hd % 2][...] - m_new)
            acc_ref[hd] = alpha * acc_ref[hd] + _dot(values(g, hd), p.astype(BF16))
            m_ref[hd] = m_new

    sweep(produce, consume)

    out_t = jnp.concatenate([acc_ref[hd, 0:MLA_V, :] / acc_ref[hd, MLA_V:MLA_V + 1, :]
                             for hd in range(n_heads)], axis=0)
    o_ref[0] = out_t.T.astype(BF16)


def _mla_attention(qt, k, vt, score_bound, *, tq, heads_per_step):
    b, heads, _, s = qt.shape
    kb = vt.shape[3]
    nh = heads_per_step
    assert kb == tq and nh % 2 == 0
    vw = nh * MLA_V
    bound_lanes = jnp.full((1, LANES), score_bound, F32)

    def call(fixed_shift):
        return pl.pallas_call(
            functools.partial(_mla_attn_kernel, tq=tq, fixed_shift=fixed_shift),
            grid=(b, heads // nh, s // tq),
            in_specs=[
                pl.BlockSpec((1, nh, LANES, tq), lambda bi, p, i: (bi, p, 0, i)),
                pl.BlockSpec((1, nh, s, LANES), lambda bi, p, i: (bi, p, 0, 0)),
                pl.BlockSpec((1, s // kb, vw, kb), lambda bi, p, i: (bi, 0, p, 0)),
                pl.BlockSpec((1, LANES), lambda bi, p, i: (0, 0)),
            ],
            out_specs=pl.BlockSpec((1, tq, vw), lambda bi, p, i: (bi, i, p)),
            out_shape=jax.ShapeDtypeStruct((b, s, heads * MLA_V), BF16),
            scratch_shapes=[pltpu.VMEM((nh, MLA_V + SUM_ROWS, tq), F32),
                            pltpu.VMEM((nh, 1, tq), F32), pltpu.VMEM((kb, tq), F32),
                            pltpu.VMEM((kb, tq), F32), pltpu.VMEM((2, 1, tq), F32),
                            pltpu.VMEM((kb, tq), BF16), pltpu.VMEM((kb, tq), BF16)],
            compiler_params=_params("parallel", "parallel", "arbitrary"),
            name="mla_attn_fixed_shift" if fixed_shift else "mla_attn_online_max",
        )(qt, k, vt, bound_lanes)

    return lax.cond(score_bound <= FIXED_SHIFT_MAX, lambda: call(True), lambda: call(False))


def _mem_kv_kernel(mem_ref, g_ref, wkv_ref, kg_ref, k_ref, v_ref):
    hm = _rms(mem_ref[0], g_ref[...]).astype(BF16)
    hd_dim = kg_ref.shape[1]
    for hd in range(MEM_HEADS):
        kcols = slice(2 * hd * hd_dim, (2 * hd + 1) * hd_dim)
        vcols = slice((2 * hd + 1) * hd_dim, (2 * hd + 2) * hd_dim)
        out = slice(hd * hd_dim, (hd + 1) * hd_dim)
        k_ref[0, :, out] = _rms(_dot(hm, wkv_ref[:, kcols]), kg_ref[...]).astype(BF16)
        v_ref[0, :, out] = _dot(hm, wkv_ref[:, vcols]).astype(BF16)


def _mem_kv(mem, gain, wkv, k_gain):
    b, m, d = mem.shape
    out = jax.ShapeDtypeStruct((b, m, d), BF16)
    spec = pl.BlockSpec((1, m, d), lambda bi: (bi, 0, 0))
    return pl.pallas_call(
        _mem_kv_kernel,
        grid=(b,),
        in_specs=[spec,
                  pl.BlockSpec((1, d), lambda bi: (0, 0)),
                  pl.BlockSpec(wkv.shape, lambda bi: (0, 0)),
                  pl.BlockSpec((1, k_gain.shape[0]), lambda bi: (0, 0))],
        out_specs=[spec, spec],
        out_shape=[out, out],
        compiler_params=_params("parallel"),
        name="mem_kv",
    )(mem, gain.reshape(1, d), wkv, k_gain.reshape(1, -1))


def _mix_xattn_kernel(*refs, n_act):
    x_ref = refs[0]
    act_refs = refs[1:1 + n_act]
    w_ref, g_ref, wq_ref, qg_ref, k_ref, v_ref, wo_ref, o_ref = refs[1 + n_act:]
    act = jnp.concatenate([a_ref[0] for a_ref in act_refs], axis=-1)
    x1 = x_ref[0] + _dot(act, w_ref[...])
    h = _rms(x1, g_ref[...]).astype(BF16)
    hd_dim = qg_ref.shape[1]
    cols = [slice(hd * hd_dim, (hd + 1) * hd_dim) for hd in range(MEM_HEADS)]
    q = [_dot(h, wq_ref[:, c]) for c in cols]
    qn = [(_rms(q_h, qg_ref[...]) * (hd_dim ** -0.5)).astype(BF16) for q_h in q]
    sc = [_dot_nt(qn_h, k_ref[0, :, c]) for qn_h, c in zip(qn, cols)]
    p = [jnp.exp(s_h - jnp.max(s_h, axis=-1, keepdims=True)) for s_h in sc]
    o = [(_dot(p_h.astype(BF16), v_ref[0, :, c]) / jnp.sum(p_h, axis=-1, keepdims=True)
          ).astype(BF16) for p_h, c in zip(p, cols)]
    o_ref[0] = x1 + _dot(jnp.concatenate(o, axis=-1), wo_ref[...])


def _mix_xattn(x, acts, w_mix, gain, wq, q_gain, mem_k, mem_v, wo, *, tm):
    b, s, d = x.shape
    m = mem_k.shape[1]
    const2 = lambda bi, i: (0, 0)
    row = lambda width: pl.BlockSpec((1, tm, width), lambda bi, i: (bi, i, 0))
    mem_spec = pl.BlockSpec((1, m, d), lambda bi, i: (bi, 0, 0))
    return pl.pallas_call(
        functools.partial(_mix_xattn_kernel, n_act=len(acts)),
        grid=(b, s // tm),
        in_specs=[
            row(d),
            *[row(a.shape[-1]) for a in acts],
            pl.BlockSpec(w_mix.shape, const2),
            pl.BlockSpec((1, d), const2),
            pl.BlockSpec(wq.shape, const2),
            pl.BlockSpec((1, q_gain.shape[0]), const2),
            mem_spec, mem_spec,
            pl.BlockSpec(wo.shape, const2),
        ],
        out_specs=row(d),
        out_shape=jax.ShapeDtypeStruct((b, s, d), F32),
        compiler_params=_params("parallel", "parallel"),
        name="mix_xattn",
    )(x, *acts, w_mix, gain.reshape(1, d), wq, q_gain.reshape(1, -1), mem_k, mem_v, wo)


def _lane_bcast(vec):
    return jnp.broadcast_to(vec[:, None], (vec.shape[0], LANES))


def _mla_weights(w_in, w_uq, w_ukv, q_lora_gain, kv_lora_gain, q_gain, k_gain):
    lat = MLA_Q_LORA + MLA_KV_LORA
    w_in_ext = jnp.pad(w_in, ((0, 0), (0, lat + LANES - w_in.shape[1])))
    wukv = w_ukv.reshape(MLA_KV_LORA, MLA_HEADS, MLA_NOPE + MLA_V)
    wukt = wukv[..., :MLA_NOPE].reshape(MLA_KV_LORA, -1).T
    wuvt = wukv[..., MLA_NOPE:].reshape(MLA_KV_LORA, -1).T
    half = MLA_ROPE // 2
    inv_freq = ROPE_THETA ** (-jnp.arange(half, dtype=F32) / half)
    bf = lambda a: a.astype(BF16)
    return (bf(w_in_ext), _lane_bcast(q_lora_gain), _lane_bcast(kv_lora_gain),
            bf(w_uq.T), bf(wukt), bf(wuvt),
            _lane_bcast(q_gain * (MLA_QK ** -0.5 * LOG2E)), _lane_bcast(k_gain),
            _lane_bcast(inv_freq))


def _tile(n, pref):
    return pref if n % pref == 0 else n


def kernel(x, mem, positions, ffn_pre_norm, ffn_pre_w_gu, ffn_pre_w_down, mix_norm, sbg_w_in, sgu_ln_gain, sgu_ln_bias, sgu_w, sgu_b, sbg_w_out, mla_w_in, mla_q_lora_gain, mla_kv_lora_gain, mla_w_uq, mla_w_ukv, mla_q_gain, mla_k_gain, mla_w_out, xmem_norm, xmem_mem_norm, xmem_wq, xmem_wkv, xmem_q_gain, xmem_k_gain, xmem_wo, ffn_post_norm, ffn_post_w_gu, ffn_post_w_down):
    b, s, d = x.shape
    depth = ffn_pre_norm.shape[0]
    d_ff = ffn_pre_w_down.shape[1]
    t = b * s
    ffn_tm = _tile(t, 512)
    ffn_tf = _tile(d_ff, 256)
    row_tm = _tile(s, 512)
    mla_tq = _tile(s, 512)
    sb_tq = _tile(s, 512)
    sb_kb = _tile(sb_tq, 256)
    bf = lambda a: a.astype(BF16)

    for layer in range(depth):
        x = _ffn(x.reshape(t, d), ffn_pre_norm[layer], ffn_pre_w_gu, ffn_pre_w_down, layer,
                 tm=ffn_tm, tf=ffn_tf).reshape(b, s, d)
        if layer % 2 == 0:
            e = layer // 2
            bias_full = jnp.repeat(sgu_b[e].T, SG_GROUP_DIM, axis=1)
            qt, k, vt, o_sg = _even_prep(
                x, mix_norm[layer], bf(sbg_w_in[e]), sgu_ln_gain[e],
                sgu_ln_bias[e], sgu_w[e], bias_full, tm=row_tm, kb=sb_kb)
            acts = (_sb_attention(qt, k, vt, tq=sb_tq), o_sg)
            w_mix = bf(sbg_w_out[e])
        else:
            o = layer // 2
            mla_consts = _mla_weights(
                mla_w_in[o], mla_w_uq[o], mla_w_ukv[o], mla_q_lora_gain[o],
                mla_kv_lora_gain[o], mla_q_gain[o], mla_k_gain[o])
            qt, k, vt = _mla_prep(x, positions, mix_norm[layer], *mla_consts,
                                  tm=row_tm, kb=mla_tq)
            score_bound = (1.02 * MLA_QK * (MLA_QK ** -0.5 * LOG2E)
                           * jnp.max(jnp.abs(mla_q_gain[o])) * jnp.max(jnp.abs(mla_k_gain[o])))
            acts = (_mla_attention(qt, k, vt, score_bound, tq=mla_tq, heads_per_step=4),)
            w_mix = bf(mla_w_out[o])
        mem_k, mem_v = _mem_kv(mem, xmem_mem_norm[layer], bf(xmem_wkv[layer]),
                               xmem_k_gain[layer])
        x = _mix_xattn(x, acts, w_mix, xmem_norm[layer], bf(xmem_wq[layer]), xmem_q_gain[layer],
                       mem_k, mem_v, bf(xmem_wo[layer]), tm=row_tm)
        x = _ffn(x.reshape(t, d), ffn_post_norm[layer], ffn_post_w_gu, ffn_post_w_down, layer,
                 tm=ffn_tm, tf=ffn_tf).reshape(b, s, d)
    return x
```

```python
import functools

import jax
import jax.numpy as jnp
from jax import lax
from jax.experimental import pallas as pl
from jax.experimental.pallas import tpu as pltpu

EPS = 1e-6
ROPE_THETA = 10000.0
LANES = 128
SUBLANES = 8
VMEM_LIMIT_BYTES = 56 * 1024 * 1024

SB_HEADS, SB_HEAD_DIM = 8, 64
SB_WIDTH = SB_HEADS * SB_HEAD_DIM
SG_GROUPS, SG_GROUP_DIM, SG_CHUNK = 8, 64, 128
SG_WIDTH = SG_GROUPS * SG_GROUP_DIM
MLA_HEADS, MLA_NOPE, MLA_ROPE, MLA_V = 16, 64, 32, 64
MLA_QK = MLA_NOPE + MLA_ROPE
MLA_Q_LORA, MLA_KV_LORA = 512, 256
MEM_HEADS = 4

BF16 = jnp.bfloat16
F32 = jnp.float32
LOG2E = 1.4426950408889634
MASKED = -1e30
SUM_ROWS = 16
FIXED_SHIFT_MAX = 60.0
UNDERFLOW_LOG2 = 160.0


def _params(*semantics):
    return pltpu.CompilerParams(dimension_semantics=semantics,
                                vmem_limit_bytes=VMEM_LIMIT_BYTES)


def _dot(a, b):
    return jnp.dot(a, b, preferred_element_type=F32)


def _dot_nt(a, b):
    return lax.dot_general(a, b, (((1,), (1,)), ((), ())), preferred_element_type=F32)


def _rms(x, gain, n=None):
    n = x.shape[-1] if n is None else n
    ms = jnp.sum(x * x, axis=-1, keepdims=True) * (1.0 / n)
    return x * lax.rsqrt(ms + EPS) * gain


def _ffn_kernel(x_ref, g_ref, wgu_ref, wd_ref, o_ref, *, tf):
    d_ff = wd_ref.shape[0]
    x = x_ref[...]
    h = _rms(x, g_ref[...]).astype(BF16)
    acc = None
    for c in range(d_ff // tf):
        cols = slice(c * tf, (c + 1) * tf)
        gate = _dot(h, wgu_ref[:, cols].astype(BF16))
        up = _dot(h, wgu_ref[:, d_ff + c * tf:d_ff + (c + 1) * tf].astype(BF16))
        act = (gate * jax.nn.sigmoid(gate) * up).astype(BF16)
        part = _dot(act, wd_ref[cols, :].astype(BF16))
        acc = part if acc is None else acc + part
    o_ref[...] = x + 0.5 * acc


def _ffn(x2, gain, w_gu, w_down, layer, *, tm, tf):
    t, d = x2.shape
    d_ff = w_down.shape[1]
    resident = pl.Buffered(1)
    return pl.pallas_call(
        functools.partial(_ffn_kernel, tf=tf),
        grid=(t // tm,),
        in_specs=[
            pl.BlockSpec((tm, d), lambda i: (i, 0)),
            pl.BlockSpec((1, d), lambda i: (0, 0)),
            pl.BlockSpec((None, d, 2 * d_ff), lambda i: (layer, 0, 0), pipeline_mode=resident),
            pl.BlockSpec((None, d_ff, d), lambda i: (layer, 0, 0), pipeline_mode=resident),
        ],
        out_specs=pl.BlockSpec((tm, d), lambda i: (i, 0)),
        out_shape=jax.ShapeDtypeStruct((t, d), F32),
        compiler_params=_params("parallel"),
        name="ffn",
    )(x2, gain.reshape(1, d), w_gu, w_down)


def _gelu_tanh(x):
    c = 0.7978845608028654
    return 0.5 * x * (1.0 + jnp.tanh(c * (x + 0.044715 * (x * x * x))))


def _even_prep_kernel(x_ref, g_ref, win_ref, lng_ref, lnb_ref, sw_ref, sb_ref,
                      qt_ref, k_ref, vt_ref, osg_ref):
    tm = x_ref.shape[1]
    kb = vt_ref.shape[3]
    w = SB_WIDTH
    h = _rms(x_ref[0], g_ref[...]).astype(BF16)
    osg_ref = osg_ref.at[0]
    u_raw = _dot(h, win_ref[:, 3 * w:3 * w + SG_WIDTH])
    g_raw = _dot(h, win_ref[:, 3 * w + SG_WIDTH:3 * w + 2 * SG_WIDTH])
    qt_ref[0] = (_dot(h, win_ref[:, 0:w]) * (SB_HEAD_DIM ** -0.5 * LOG2E)).T.astype(BF16)
    u = _gelu_tanh(u_raw)
    k_ref[0] = _dot(h, win_ref[:, w:2 * w]).astype(BF16)
    g = _gelu_tanh(g_raw)
    mu = jnp.mean(g, axis=-1, keepdims=True)
    gc = g - mu
    var = jnp.mean(gc * gc, axis=-1, keepdims=True)
    gn = (gc * lax.rsqrt(var + EPS) * lng_ref[...] + lnb_ref[...]).astype(BF16)
    v = _dot(h, win_ref[:, 2 * w:3 * w])
    for c in range(tm // kb):
        vt_ref[0, c] = v[c * kb:(c + 1) * kb, :].T.astype(BF16)

    row = lax.broadcasted_iota(jnp.int32, (SG_CHUNK, SG_CHUNK), 0)
    col = lax.broadcasted_iota(jnp.int32, (SG_CHUNK, SG_CHUNK), 1)
    tri = col <= row
    first_group = lax.broadcasted_iota(jnp.int32, (SG_CHUNK, LANES), 1) < SG_GROUP_DIM
    for p in range(SG_GROUPS // 2):
        lanes = slice(p * LANES, (p + 1) * LANES)
        w0 = jnp.where(tri, sw_ref[2 * p], 0.0).astype(BF16)
        w1 = jnp.where(tri, sw_ref[2 * p + 1], 0.0).astype(BF16)
        bias = sb_ref[:, lanes]
        for c in range(tm // SG_CHUNK):
            rows = slice(c * SG_CHUNK, (c + 1) * SG_CHUNK)
            gp = gn[rows, lanes]
            mixed = jnp.where(first_group, _dot(w0, gp), _dot(w1, gp)) + bias
            osg_ref[rows, lanes] = (u[rows, lanes] * mixed).astype(BF16)


def _even_prep(x, gain, w_in, ln_g, ln_b, sgu_w, sgu_bias_full, *, tm, kb):
    b, s, d = x.shape
    n_in = w_in.shape[1]
    w = SB_WIDTH
    const2 = lambda bi, i: (0, 0)
    row_out = jax.ShapeDtypeStruct((b, s, w), BF16)
    row_spec = pl.BlockSpec((1, tm, w), lambda bi, i: (bi, i, 0))
    return pl.pallas_call(
        _even_prep_kernel,
        grid=(b, s // tm),
        in_specs=[
            pl.BlockSpec((1, tm, d), lambda bi, i: (bi, i, 0)),
            pl.BlockSpec((1, d), const2),
            pl.BlockSpec((d, n_in), const2),
            pl.BlockSpec((1, SG_WIDTH), const2),
            pl.BlockSpec((1, SG_WIDTH), const2),
            pl.BlockSpec((SG_GROUPS, SG_CHUNK, SG_CHUNK), lambda bi, i: (0, 0, 0)),
            pl.BlockSpec((SG_CHUNK, SG_WIDTH), const2),
        ],
        out_specs=[pl.BlockSpec((1, w, tm), lambda bi, i: (bi, 0, i)),
                   row_spec,
                   pl.BlockSpec((1, tm // kb, w, kb), lambda bi, i: (bi, i, 0, 0)),
                   row_spec],
        out_shape=[jax.ShapeDtypeStruct((b, w, s), BF16), row_out,
                   jax.ShapeDtypeStruct((b, s // kb, w, kb), BF16), row_out],
        compiler_params=_params("parallel", "parallel"),
        name="even_prep",
    )(x, gain.reshape(1, d), w_in, ln_g.reshape(1, -1), ln_b.reshape(1, -1),
      sgu_w, sgu_bias_full)


def _sb_attn_kernel(qt_ref, k_ref, vt_ref, o_ref, acc_ref, r_ref, z0_ref, z1_ref, zc0_ref,
                    zc1_ref, t0_ref, t1_ref, bs_ref, kn_ref, *, tq, kb):
    i = pl.program_id(2)
    n_sub = tq // kb
    hd_dim = SB_HEAD_DIM
    z_refs, zc_refs, t_refs = (z0_ref, z1_ref), (zc0_ref, zc1_ref), (t0_ref, t1_ref)
    acc_refs = (acc_ref.at[0], acc_ref.at[1])
    first_head = lax.broadcasted_iota(jnp.int32, (LANES, 1), 0) < hd_dim
    qt = qt_ref[0]
    zero = jnp.zeros_like(qt)
    qt_heads = (jnp.where(first_head, qt, zero), jnp.where(first_head, zero, qt))
    visible_from = {
        lo: (lax.broadcasted_iota(jnp.int32, (kb, tq - lo), 0)
             < lax.broadcasted_iota(jnp.int32, (kb, tq - lo), 1))
        for lo in range(0, tq, kb)}
    srow = lax.broadcasted_iota(jnp.int32, (kb, kb), 0)
    scol = lax.broadcasted_iota(jnp.int32, (kb, kb), 1)
    suffix = (scol >= srow).astype(BF16)

    not_first_window = lax.broadcasted_iota(jnp.int32, (kb, tq), 1) >= kb

    def score(item):
        g, hd, slot, lo, hi = item
        start = pl.multiple_of(g * kb, kb)
        z_refs[slot][:, lo:hi] = _dot(k_ref[0, pl.ds(start, kb), :], qt_heads[hd][:, lo:hi])

    def stay(item, mask):
        g, hd, slot, lo, hi = item
        z = z_refs[slot][:, lo:hi]
        sp = jnp.maximum(z, 0.0) + jnp.log2(1.0 + jnp.exp2(-jnp.abs(z)))
        if mask is not None:
            if isinstance(mask, str):
                visible = visible_from[lo] if mask == "diagonal" else not_first_window
            else:
                visible = mask
            sp = jnp.where(visible, sp, 0.0)
            z = jnp.where(visible, z, MASKED)
        zc_refs[slot][:, lo:hi] = z
        tail = _dot(suffix, sp.astype(BF16))
        t_refs[slot][:, lo:hi] = tail
        parts = [jnp.broadcast_to(tail[0:1, :], (SUBLANES, hi - lo))]
        if lo:
            parts.insert(0, jnp.zeros((SUBLANES, lo), F32))
        if hi < tq:
            parts.append(jnp.zeros((SUBLANES, tq - hi), F32))
        bs_ref[slot] = jnp.concatenate(parts, axis=1) if len(parts) > 1 else parts[0]

    def weigh(item):
        g, hd, slot, lo, hi = item
        r = r_ref[hd]
        wgt = jnp.exp2(zc_refs[slot][:, lo:hi] - t_refs[slot][:, lo:hi] - r[0:1, lo:hi])
        vt = vt_ref[0, g, hd * hd_dim:(hd + 1) * hd_dim, :]
        acc_refs[hd][:, lo:hi] += _dot(vt, wgt.astype(BF16))
        r_ref[hd] = r + bs_ref[slot]

    def block(g, window, prev, nxt, mask):
        score((g, 1, 1, *window))
        if prev is not None:
            weigh((prev[0], 1, 1, *prev[1]))
        stay((g, 0, 0, *window), mask)
        if nxt is not None:
            score((nxt[0], 0, 0, *nxt[1]))
        weigh((g, 0, 0, *window))
        stay((g, 1, 1, *window), mask)

    @pl.when(i == 0)
    def _():
        kf = k_ref[0].astype(F32)
        dim = lax.broadcasted_iota(jnp.int32, (LANES, LANES), 0)
        head = lax.broadcasted_iota(jnp.int32, (LANES, LANES), 1)
        select = ((dim < hd_dim) == (head == 0)) & (head < 2)
        kn_ref[...] = jnp.max(_dot((kf * kf).astype(BF16), select.astype(BF16)),
                              axis=0, keepdims=True)

    lane = lax.broadcasted_iota(jnp.int32, (1, LANES), 1)
    qf = qt.astype(F32)
    exit_level = []
    for hd in range(2):
        q_sq = jnp.sum(jnp.square(qf[hd * hd_dim:(hd + 1) * hd_dim, :]), axis=0, keepdims=True)
        k_sq = jnp.max(jnp.where(lane == hd, kn_ref[...], 0.0), axis=1, keepdims=True)
        exit_level.append(1.02 * jnp.sqrt(q_sq * k_sq) + UNDERFLOW_LOG2)

    acc_ref[...] = jnp.zeros_like(acc_ref)
    r_ref[...] = jnp.zeros_like(r_ref)
    first = i * n_sub
    full = (0, tq)
    below = jnp.maximum(first - 1, 0)
    score((first + n_sub - 1, 0, 0, (n_sub - 1) * kb, tq))
    for d in reversed(range(n_sub)):
        prev = (first + d + 1, ((d + 1) * kb, tq)) if d < n_sub - 1 else None
        nxt = (first + d - 1, ((d - 1) * kb, tq)) if d > 0 else (below, (0, kb))
        block(first + d, (d * kb, tq), prev, nxt, "diagonal")

    def still_live():
        dead = jnp.min(jnp.minimum(r_ref[0] - exit_level[0], r_ref[1] - exit_level[1])) > 0.0
        return jnp.where(dead, 0, 1).astype(jnp.int32)

    block(below, (0, kb), (first, full), None, i > 0)
    weigh((below, 1, 1, 0, kb))

    @pl.when((still_live() > 0) & (i > 0))
    def _():
        score((below, 0, 0, *full))
        block(below, full, None, (jnp.maximum(below - 1, 0), full), "later windows")

        def more(carry):
            g, live = carry
            return (g >= 0) & (live > 0)

        def body(carry):
            g, _ = carry
            block(g, full, (g + 1, full), (jnp.maximum(g - 1, 0), full), None)
            return g - 1, still_live()

        g_end, _ = lax.while_loop(more, body, (below - 1, still_live()))
        weigh((g_end + 1, 1, 1, *full))

    out_t = jnp.concatenate([acc_ref[0], acc_ref[1]], axis=0)
    o_ref[0] = out_t.T.astype(BF16)


def _sb_attention(qt, k, vt, *, tq):
    b, w, s = qt.shape
    kb = vt.shape[3]
    return pl.pallas_call(
        functools.partial(_sb_attn_kernel, tq=tq, kb=kb),
        grid=(b, w // LANES, s // tq),
        in_specs=[
            pl.BlockSpec((1, LANES, tq), lambda bi, p, i: (bi, p, i)),
            pl.BlockSpec((1, s, LANES), lambda bi, p, i: (bi, 0, p)),
            pl.BlockSpec((1, s // kb, LANES, kb), lambda bi, p, i: (bi, 0, p, 0)),
        ],
        out_specs=pl.BlockSpec((1, tq, LANES), lambda bi, p, i: (bi, i, p)),
        out_shape=jax.ShapeDtypeStruct((b, s, w), BF16),
        scratch_shapes=[pltpu.VMEM((2, SB_HEAD_DIM, tq), F32),
                        pltpu.VMEM((2, SUBLANES, tq), F32),
                        *[pltpu.VMEM((kb, tq), F32) for _ in range(6)],
                        pltpu.VMEM((2, SUBLANES, tq), F32), pltpu.VMEM((1, LANES), F32)],
        compiler_params=_params("parallel", "parallel", "arbitrary"),
        name="sb_attn",
    )(qt, k, vt)


def _lane_tile(t, width):
    return jnp.concatenate([t] * (width // t.shape[1]), axis=1)


def _rms_rows(xt, gain):
    ms = jnp.sum(xt * xt, axis=0, keepdims=True) * (1.0 / xt.shape[0])
    return xt * lax.rsqrt(ms + EPS) * _lane_tile(gain, xt.shape[1])


def _mla_prep_kernel(x_ref, pos_ref, g_ref, win_ref, qlg_ref, kvlg_ref, wuqt_ref, wukt_ref,
                     wuvt_ref, qg_ref, kg_ref, freq_ref, qt_ref, k_ref, vt_ref):
    tm = x_ref.shape[1]
    kb = vt_ref.shape[3]
    lat = MLA_Q_LORA + MLA_KV_LORA
    half = MLA_ROPE // 2
    h = _rms(x_ref[0], g_ref[...]).astype(BF16)
    ct = _dot(h, win_ref[...]).T
    cqn = _rms_rows(ct[0:MLA_Q_LORA], qlg_ref[...]).astype(BF16)
    ckvn = _rms_rows(ct[MLA_Q_LORA:lat], kvlg_ref[...]).astype(BF16)
    k_r = ct[lat:lat + MLA_ROPE]

    group = 4

    def up_project(gi):
        rows = slice(gi * group * MLA_QK, (gi + 1) * group * MLA_QK)
        krows = slice(gi * group * MLA_NOPE, (gi + 1) * group * MLA_NOPE)
        return _dot(wuqt_ref[rows, :], cqn), _dot(wukt_ref[krows, :], ckvn)

    angle = _lane_tile(freq_ref[...], tm) * pos_ref[0].astype(F32)
    cos = jnp.cos(angle)
    sin = jnp.sin(angle)

    def rope(t):
        t1, t2 = t[0:half], t[half:]
        return jnp.concatenate([t1 * cos - t2 * sin, t1 * sin + t2 * cos], axis=0)

    qg = _lane_tile(qg_ref[...], tm)
    kg = _lane_tile(kg_ref[...], tm)
    zeros = jnp.zeros((LANES - MLA_QK, tm), F32)
    inv_n = 1.0 / MLA_QK
    kr_sq = jnp.sum(k_r * k_r, axis=0, keepdims=True)
    kr_roped = rope(k_r * kg[MLA_NOPE:])
    n_groups = MLA_HEADS // group
    pending = up_project(0)
    for gi in range(n_groups):
        q_grp, kn_grp = pending
        if gi + 1 < n_groups:
            pending = up_project(gi + 1)
        else:
            vt = _dot(wuvt_ref[...], ckvn)
            for c in range(tm // kb):
                vt_ref[0, c] = vt[:, c * kb:(c + 1) * kb].astype(BF16)
        for sub in range(group):
            hd = gi * group + sub
            qh = q_grp[sub * MLA_QK:(sub + 1) * MLA_QK]
            r = lax.rsqrt(jnp.sum(qh * qh, axis=0, keepdims=True) * inv_n + EPS)
            qn = qh * r * qg
            qt_ref[0, hd] = jnp.concatenate([qn[0:MLA_NOPE], rope(qn[MLA_NOPE:]), zeros],
                                            axis=0).astype(BF16)
            kn = kn_grp[sub * MLA_NOPE:(sub + 1) * MLA_NOPE]
            r = lax.rsqrt((jnp.sum(kn * kn, axis=0, keepdims=True) + kr_sq) * inv_n + EPS)
            kt = jnp.concatenate([kn * r * kg[0:MLA_NOPE], kr_roped * r, zeros], axis=0)
            k_ref[0, hd] = kt.T.astype(BF16)


def _mla_prep(x, positions, gain, w_in, qlg, kvlg, wuqt, wukt, wuvt, qg, kg, freq, *, tm, kb):
    b, s, d = x.shape
    vw = MLA_HEADS * MLA_V
    const2 = lambda bi, i: (0, 0)
    full = lambda a: pl.BlockSpec(a.shape, const2)
    small = [gain.reshape(1, -1), w_in, qlg, kvlg, wuqt, wukt, wuvt, qg, kg, freq]
    return pl.pallas_call(
        _mla_prep_kernel,
        grid=(b, s // tm),
        in_specs=[
            pl.BlockSpec((1, tm, d), lambda bi, i: (bi, i, 0)),
            pl.BlockSpec((1, 1, tm), lambda bi, i: (bi, 0, i)),
            *[full(a) for a in small],
        ],
        out_specs=[pl.BlockSpec((1, MLA_HEADS, LANES, tm), lambda bi, i: (bi, 0, 0, i)),
                   pl.BlockSpec((1, MLA_HEADS, tm, LANES), lambda bi, i: (bi, 0, i, 0)),
                   pl.BlockSpec((1, tm // kb, vw, kb), lambda bi, i: (bi, i, 0, 0))],
        out_shape=[jax.ShapeDtypeStruct((b, MLA_HEADS, LANES, s), BF16),
                   jax.ShapeDtypeStruct((b, MLA_HEADS, s, LANES), BF16),
                   jax.ShapeDtypeStruct((b, s // kb, vw, kb), BF16)],
        compiler_params=_params("parallel", "parallel"),
        name="mla_prep",
    )(x, positions.reshape(b, 1, s), *small)


def _mla_attn_kernel(qa_ref, qb_ref, k_ref, vt_ref, bound_ref, oa_ref, ob_ref, qt_ref, acc_ref,
                     m_ref, s0_ref, s1_ref, bm_ref, p0_ref, p1_ref, *, tq, n_tiles, fixed_shift):
    i = pl.program_id(2)
    n_heads = qa_ref.shape[1]
    s_refs, p_refs = (s0_ref, s1_ref), (p0_ref, p1_ref)
    key = lax.broadcasted_iota(jnp.int32, (tq, tq), 0)
    query = lax.broadcasted_iota(jnp.int32, (tq, tq), 1)
    causal = key <= query
    ones_rows = jnp.ones((SUM_ROWS, tq), BF16)
    qt_ref[0] = qa_ref[0]
    qt_ref[1] = qb_ref[0]

    def scores(item):
        g, hd, masked, which = item
        start = pl.multiple_of(g * tq, tq)
        sc = _dot(k_ref[0, hd, pl.ds(start, tq), :], qt_ref[which, hd])
        return jnp.where(causal, sc, MASKED) if masked else sc

    def values(item):
        g, hd = item[0], item[1]
        return jnp.concatenate([vt_ref[0, g, hd * MLA_V:(hd + 1) * MLA_V, :], ones_rows], axis=0)

    acc_ref[...] = jnp.zeros_like(acc_ref)

    if fixed_shift:
        bound = bound_ref[:, 0:1]

        def produce(item):
            p_refs[item[1] % 2][...] = jnp.exp2(scores(item) - bound).astype(BF16)

        def consume(item):
            hd, which = item[1], item[3]
            acc_ref[which, hd] += _dot(values(item), p_refs[hd % 2][...])
    else:
        m_ref[...] = jnp.full_like(m_ref, MASKED)

        def produce(item):
            sc = scores(item)
            s_refs[item[1] % 2][...] = sc
            bm_ref[item[1] % 2] = jnp.max(sc, axis=0, keepdims=True)

        def consume(item):
            hd, which = item[1], item[3]
            m_old = m_ref[which, hd]
            m_new = jnp.maximum(m_old, bm_ref[hd % 2])
            alpha = jnp.exp2(m_old - m_new)
            p = jnp.exp2(s_refs[hd % 2][...] - m_new)
            acc_ref[which, hd] = alpha * acc_ref[which, hd] + _dot(values(item), p.astype(BF16))
            m_ref[which, hd] = m_new

    visits = [(i, True, 0), (n_tiles - 1 - i, True, 1)]
    for j in range(n_tiles - 1):
        which = (j >= i).astype(jnp.int32)
        visits.append((j - i * which, False, which))
    items = [(g, hd, masked, which) for g, masked, which in visits for hd in range(n_heads)]
    produce(items[0])
    for idx, item in enumerate(items):
        if idx + 1 < len(items):
            produce(items[idx + 1])
        consume(item)

    for which, o_ref in enumerate((oa_ref, ob_ref)):
        out_t = jnp.concatenate(
            [acc_ref[which, hd, 0:MLA_V, :] / acc_ref[which, hd, MLA_V:MLA_V + 1, :]
             for hd in range(n_heads)], axis=0)
        o_ref[0] = out_t.T.astype(BF16)


def _mla_attention(qt, k, vt, score_bound, *, tq, heads_per_step):
    b, heads, _, s = qt.shape
    kb = vt.shape[3]
    nh = heads_per_step
    n_tiles = s // tq
    assert kb == tq and nh % 2 == 0 and n_tiles % 2 == 0
    half = n_tiles // 2
    vw = nh * MLA_V
    bound_lanes = jnp.full((1, LANES), score_bound, F32)
    half_out = jax.ShapeDtypeStruct((b, s // 2, heads * MLA_V), BF16)

    def call(fixed_shift):
        return pl.pallas_call(
            functools.partial(_mla_attn_kernel, tq=tq, n_tiles=n_tiles, fixed_shift=fixed_shift),
            grid=(b, heads // nh, half),
            in_specs=[
                pl.BlockSpec((1, nh, LANES, tq), lambda bi, p, i: (bi, p, 0, i)),
                pl.BlockSpec((1, nh, LANES, tq), lambda bi, p, i: (bi, p, 0, n_tiles - 1 - i)),
                pl.BlockSpec((1, nh, s, LANES), lambda bi, p, i: (bi, p, 0, 0)),
                pl.BlockSpec((1, s // kb, vw, kb), lambda bi, p, i: (bi, 0, p, 0)),
                pl.BlockSpec((1, LANES), lambda bi, p, i: (0, 0)),
            ],
            out_specs=[pl.BlockSpec((1, tq, vw), lambda bi, p, i: (bi, i, p)),
                       pl.BlockSpec((1, tq, vw), lambda bi, p, i: (bi, half - 1 - i, p))],
            out_shape=[half_out, half_out],
            scratch_shapes=[pltpu.VMEM((2, nh, LANES, tq), BF16),
                            pltpu.VMEM((2, nh, MLA_V + SUM_ROWS, tq), F32),
                            pltpu.VMEM((2, nh, 1, tq), F32), pltpu.VMEM((kb, tq), F32),
                            pltpu.VMEM((kb, tq), F32), pltpu.VMEM((2, 1, tq), F32),
                            pltpu.VMEM((kb, tq), BF16), pltpu.VMEM((kb, tq), BF16)],
            compiler_params=_params("parallel", "parallel", "arbitrary"),
            name="mla_attn_fixed_shift" if fixed_shift else "mla_attn_online_max",
        )(qt, qt, k, vt, bound_lanes)

    return lax.cond(score_bound <= FIXED_SHIFT_MAX, lambda: call(True), lambda: call(False))


def _mem_kv_kernel(mem_ref, g_ref, wkv_ref, kg_ref, k_ref, v_ref):
    hm = _rms(mem_ref[0], g_ref[...]).astype(BF16)
    hd_dim = kg_ref.shape[1]
    for hd in range(MEM_HEADS):
        kcols = slice(2 * hd * hd_dim, (2 * hd + 1) * hd_dim)
        vcols = slice((2 * hd + 1) * hd_dim, (2 * hd + 2) * hd_dim)
        out = slice(hd * hd_dim, (hd + 1) * hd_dim)
        k_ref[0, :, out] = _rms(_dot(hm, wkv_ref[:, kcols]), kg_ref[...]).astype(BF16)
        v_ref[0, :, out] = _dot(hm, wkv_ref[:, vcols]).astype(BF16)


def _mem_kv(mem, gain, wkv, k_gain):
    b, m, d = mem.shape
    out = jax.ShapeDtypeStruct((b, m, d), BF16)
    spec = pl.BlockSpec((1, m, d), lambda bi: (bi, 0, 0))
    return pl.pallas_call(
        _mem_kv_kernel,
        grid=(b,),
        in_specs=[spec,
                  pl.BlockSpec((1, d), lambda bi: (0, 0)),
                  pl.BlockSpec(wkv.shape, lambda bi: (0, 0)),
                  pl.BlockSpec((1, k_gain.shape[0]), lambda bi: (0, 0))],
        out_specs=[spec, spec],
        out_shape=[out, out],
        compiler_params=_params("parallel"),
        name="mem_kv",
    )(mem, gain.reshape(1, d), wkv, k_gain.reshape(1, -1))


def _mix_xattn_kernel(*refs, n_act, seq_halves):
    x_ref = refs[0]
    act_refs = refs[1:1 + n_act]
    w_ref, g_ref, wq_ref, qg_ref, k_ref, v_ref, wo_ref, o_ref = refs[1 + n_act:]
    if seq_halves:
        first_half = pl.program_id(1) < pl.num_programs(1) // 2
        act = jnp.where(first_half, act_refs[0][0], act_refs[1][0])
    else:
        act = jnp.concatenate([a_ref[0] for a_ref in act_refs], axis=-1)
    x1 = x_ref[0] + _dot(act, w_ref[...])
    h = _rms(x1, g_ref[...]).astype(BF16)
    hd_dim = qg_ref.shape[1]
    cols = [slice(hd * hd_dim, (hd + 1) * hd_dim) for hd in range(MEM_HEADS)]
    q = [_dot(h, wq_ref[:, c]) for c in cols]
    qn = [(_rms(q_h, qg_ref[...]) * (hd_dim ** -0.5)).astype(BF16) for q_h in q]
    sc = [_dot_nt(qn_h, k_ref[0, :, c]) for qn_h, c in zip(qn, cols)]
    p = [jnp.exp(s_h - jnp.max(s_h, axis=-1, keepdims=True)) for s_h in sc]
    o = [(_dot(p_h.astype(BF16), v_ref[0, :, c]) / jnp.sum(p_h, axis=-1, keepdims=True)
          ).astype(BF16) for p_h, c in zip(p, cols)]
    o_ref[0] = x1 + _dot(jnp.concatenate(o, axis=-1), wo_ref[...])


def _mix_xattn(x, acts, w_mix, gain, wq, q_gain, mem_k, mem_v, wo, *, tm, seq_halves=False):
    b, s, d = x.shape
    m = mem_k.shape[1]
    const2 = lambda bi, i: (0, 0)
    row = lambda width: pl.BlockSpec((1, tm, width), lambda bi, i: (bi, i, 0))
    mem_spec = pl.BlockSpec((1, m, d), lambda bi, i: (bi, 0, 0))
    if seq_halves:
        half = s // tm // 2
        width = acts[0].shape[-1]
        act_specs = [
            pl.BlockSpec((1, tm, width), lambda bi, i: (bi, jnp.minimum(i, half - 1), 0)),
            pl.BlockSpec((1, tm, width), lambda bi, i: (bi, jnp.maximum(i - half, 0), 0))]
    else:
        act_specs = [row(a.shape[-1]) for a in acts]
    return pl.pallas_call(
        functools.partial(_mix_xattn_kernel, n_act=len(acts), seq_halves=seq_halves),
        grid=(b, s // tm),
        in_specs=[
            row(d),
            *act_specs,
            pl.BlockSpec(w_mix.shape, const2),
            pl.BlockSpec((1, d), const2),
            pl.BlockSpec(wq.shape, const2),
            pl.BlockSpec((1, q_gain.shape[0]), const2),
            mem_spec, mem_spec,
            pl.BlockSpec(wo.shape, const2),
        ],
        out_specs=row(d),
        out_shape=jax.ShapeDtypeStruct((b, s, d), F32),
        compiler_params=_params("parallel", "parallel"),
        name="mix_xattn",
    )(x, *acts, w_mix, gain.reshape(1, d), wq, q_gain.reshape(1, -1), mem_k, mem_v, wo)


def _lane_bcast(vec):
    return jnp.broadcast_to(vec[:, None], (vec.shape[0], LANES))


def _mla_weights(w_in, w_uq, w_ukv, q_lora_gain, kv_lora_gain, q_gain, k_gain):
    lat = MLA_Q_LORA + MLA_KV_LORA
    w_in_ext = jnp.pad(w_in, ((0, 0), (0, lat + LANES - w_in.shape[1])))
    wukv = w_ukv.reshape(MLA_KV_LORA, MLA_HEADS, MLA_NOPE + MLA_V)
    wukt = wukv[..., :MLA_NOPE].reshape(MLA_KV_LORA, -1).T
    wuvt = wukv[..., MLA_NOPE:].reshape(MLA_KV_LORA, -1).T
    half = MLA_ROPE // 2
    inv_freq = ROPE_THETA ** (-jnp.arange(half, dtype=F32) / half)
    bf = lambda a: a.astype(BF16)
    return (bf(w_in_ext), _lane_bcast(q_lora_gain), _lane_bcast(kv_lora_gain),
            bf(w_uq.T), bf(wukt), bf(wuvt),
            _lane_bcast(q_gain * (MLA_QK ** -0.5 * LOG2E)), _lane_bcast(k_gain),
            _lane_bcast(inv_freq))


def _tile(n, pref):
    return pref if n % pref == 0 else n


def kernel(x, mem, positions, ffn_pre_norm, ffn_pre_w_gu, ffn_pre_w_down, mix_norm, sbg_w_in, sgu_ln_gain, sgu_ln_bias, sgu_w, sgu_b, sbg_w_out, mla_w_in, mla_q_lora_gain, mla_kv_lora_gain, mla_w_uq, mla_w_ukv, mla_q_gain, mla_k_gain, mla_w_out, xmem_norm, xmem_mem_norm, xmem_wq, xmem_wkv, xmem_q_gain, xmem_k_gain, xmem_wo, ffn_post_norm, ffn_post_w_gu, ffn_post_w_down):
    b, s, d = x.shape
    depth = ffn_pre_norm.shape[0]
    d_ff = ffn_pre_w_down.shape[1]
    t = b * s
    ffn_tm = _tile(t, 512)
    ffn_tf = _tile(d_ff, 256)
    row_tm = _tile(s, 512)
    mla_tq = _tile(s, 512)
    sb_tq = _tile(s, 512)
    sb_kb = _tile(sb_tq, 256)
    bf = lambda a: a.astype(BF16)

    for layer in range(depth):
        x = _ffn(x.reshape(t, d), ffn_pre_norm[layer], ffn_pre_w_gu, ffn_pre_w_down, layer,
                 tm=ffn_tm, tf=ffn_tf).reshape(b, s, d)
        if layer % 2 == 0:
            e = layer // 2
            bias_full = jnp.repeat(sgu_b[e].T, SG_GROUP_DIM, axis=1)
            qt, k, vt, o_sg = _even_prep(
                x, mix_norm[layer], bf(sbg_w_in[e]), sgu_ln_gain[e],
                sgu_ln_bias[e], sgu_w[e], bias_full, tm=row_tm, kb=sb_kb)
            acts = (_sb_attention(qt, k, vt, tq=sb_tq), o_sg)
            w_mix = bf(sbg_w_out[e])
        else:
            o = layer // 2
            mla_consts = _mla_weights(
                mla_w_in[o], mla_w_uq[o], mla_w_ukv[o], mla_q_lora_gain[o],
                mla_kv_lora_gain[o], mla_q_gain[o], mla_k_gain[o])
            qt, k, vt = _mla_prep(x, positions, mix_norm[layer], *mla_consts,
                                  tm=row_tm, kb=mla_tq)
            score_bound = (1.02 * MLA_QK * (MLA_QK ** -0.5 * LOG2E)
                           * jnp.max(jnp.abs(mla_q_gain[o])) * jnp.max(jnp.abs(mla_k_gain[o])))
            acts = _mla_attention(qt, k, vt, score_bound, tq=mla_tq, heads_per_step=4)
            w_mix = bf(mla_w_out[o])
        mem_k, mem_v = _mem_kv(mem, xmem_mem_norm[layer], bf(xmem_wkv[layer]),
                               xmem_k_gain[layer])
        x = _mix_xattn(x, acts, w_mix, xmem_norm[layer], bf(xmem_wq[layer]), xmem_q_gain[layer],
                       mem_k, mem_v, bf(xmem_wo[layer]), tm=row_tm, seq_halves=layer % 2 == 1)
        x = _ffn(x.reshape(t, d), ffn_post_norm[layer], ffn_post_w_gu, ffn_post_w_down, layer,
                 tm=ffn_tm, tf=ffn_tf).reshape(b, s, d)
    return x
```

```python
import functools

import jax
import jax.numpy as jnp
from jax import lax
from jax.experimental import pallas as pl
from jax.experimental.pallas import tpu as pltpu

EPS = 1e-6
ROPE_THETA = 10000.0
LANES = 128
SUBLANES = 8
VMEM_LIMIT_BYTES = 56 * 1024 * 1024

SB_HEADS, SB_HEAD_DIM = 8, 64
SB_WIDTH = SB_HEADS * SB_HEAD_DIM
SG_GROUPS, SG_GROUP_DIM, SG_CHUNK = 8, 64, 128
SG_WIDTH = SG_GROUPS * SG_GROUP_DIM
MLA_HEADS, MLA_NOPE, MLA_ROPE, MLA_V = 16, 64, 32, 64
MLA_QK = MLA_NOPE + MLA_ROPE
MLA_Q_LORA, MLA_KV_LORA = 512, 256
MEM_HEADS = 4

BF16 = jnp.bfloat16
F32 = jnp.float32
LOG2E = 1.4426950408889634
MASKED = -1e30
SUM_ROWS = 16
FIXED_SHIFT_MAX = 60.0
UNDERFLOW_LOG2 = 160.0


def _params(*semantics):
    return pltpu.CompilerParams(dimension_semantics=semantics,
                                vmem_limit_bytes=VMEM_LIMIT_BYTES)


def _dot(a, b):
    return jnp.dot(a, b, preferred_element_type=F32)


def _dot_nt(a, b):
    return lax.dot_general(a, b, (((1,), (1,)), ((), ())), preferred_element_type=F32)


def _rms(x, gain):
    ms = jnp.sum(x * x, axis=-1, keepdims=True) * (1.0 / x.shape[-1])
    return x * lax.rsqrt(ms + EPS) * gain


def _ffn_kernel(x_ref, g_ref, wgu_ref, wd_ref, o_ref, *, tf):
    d_ff = wd_ref.shape[0]
    x = x_ref[...]
    h = _rms(x, g_ref[...]).astype(BF16)
    acc = None
    for lo in range(0, d_ff, tf):
        hi = min(lo + tf, d_ff)
        gate = _dot(h, wgu_ref[:, lo:hi].astype(BF16))
        up = _dot(h, wgu_ref[:, d_ff + lo:d_ff + hi].astype(BF16))
        act = (gate * jax.nn.sigmoid(gate) * up).astype(BF16)
        part = _dot(act, wd_ref[lo:hi, :].astype(BF16))
        acc = part if acc is None else acc + part
    o_ref[...] = x + 0.5 * acc


def _ffn(x2, gain, w_gu, w_down, layer, *, tm, tf):
    t, d = x2.shape
    d_ff = w_down.shape[1]
    resident = pl.Buffered(1)
    return pl.pallas_call(
        functools.partial(_ffn_kernel, tf=tf),
        grid=(t // tm,),
        in_specs=[
            pl.BlockSpec((tm, d), lambda i: (i, 0)),
            pl.BlockSpec((1, d), lambda i: (0, 0)),
            pl.BlockSpec((None, d, 2 * d_ff), lambda i: (layer, 0, 0), pipeline_mode=resident),
            pl.BlockSpec((None, d_ff, d), lambda i: (layer, 0, 0), pipeline_mode=resident),
        ],
        out_specs=pl.BlockSpec((tm, d), lambda i: (i, 0)),
        out_shape=jax.ShapeDtypeStruct((t, d), F32),
        compiler_params=_params("parallel"),
        name="ffn",
    )(x2, gain.reshape(1, d), w_gu, w_down)


def _gelu_tanh(x):
    c = 0.7978845608028654
    return 0.5 * x * (1.0 + jnp.tanh(c * (x + 0.044715 * (x * x * x))))


def _even_prep_kernel(x_ref, g_ref, win_ref, lng_ref, lnb_ref, sw_ref, sb_ref,
                      qt_ref, k_ref, vt_ref, osg_ref):
    tm = x_ref.shape[1]
    kb = vt_ref.shape[3]
    w = SB_WIDTH
    h = _rms(x_ref[0], g_ref[...]).astype(BF16)
    osg_ref = osg_ref.at[0]
    u_raw = _dot(h, win_ref[:, 3 * w:3 * w + SG_WIDTH])
    g_raw = _dot(h, win_ref[:, 3 * w + SG_WIDTH:3 * w + 2 * SG_WIDTH])
    qt_ref[0] = (_dot(h, win_ref[:, 0:w]) * (SB_HEAD_DIM ** -0.5 * LOG2E)).T.astype(BF16)
    u = _gelu_tanh(u_raw)
    k_ref[0] = _dot(h, win_ref[:, w:2 * w]).astype(BF16)
    g = _gelu_tanh(g_raw)
    mu = jnp.mean(g, axis=-1, keepdims=True)
    gc = g - mu
    var = jnp.mean(gc * gc, axis=-1, keepdims=True)
    gn = (gc * lax.rsqrt(var + EPS) * lng_ref[...] + lnb_ref[...]).astype(BF16)
    v = _dot(h, win_ref[:, 2 * w:3 * w])
    for c in range(tm // kb):
        vt_ref[0, c] = v[c * kb:(c + 1) * kb, :].T.astype(BF16)

    row = lax.broadcasted_iota(jnp.int32, (SG_CHUNK, SG_CHUNK), 0)
    col = lax.broadcasted_iota(jnp.int32, (SG_CHUNK, SG_CHUNK), 1)
    tri = col <= row
    first_group = lax.broadcasted_iota(jnp.int32, (SG_CHUNK, LANES), 1) < SG_GROUP_DIM
    for p in range(SG_GROUPS // 2):
        lanes = slice(p * LANES, (p + 1) * LANES)
        w0 = jnp.where(tri, sw_ref[2 * p], 0.0).astype(BF16)
        w1 = jnp.where(tri, sw_ref[2 * p + 1], 0.0).astype(BF16)
        bias = sb_ref[:, lanes]
        for c in range(tm // SG_CHUNK):
            rows = slice(c * SG_CHUNK, (c + 1) * SG_CHUNK)
            gp = gn[rows, lanes]
            mixed = jnp.where(first_group, _dot(w0, gp), _dot(w1, gp)) + bias
            osg_ref[rows, lanes] = (u[rows, lanes] * mixed).astype(BF16)


def _even_prep(x, gain, w_in, ln_g, ln_b, sgu_w, sgu_bias_full, *, tm, kb):
    b, s, d = x.shape
    n_in = w_in.shape[1]
    w = SB_WIDTH
    const2 = lambda bi, i: (0, 0)
    row_out = jax.ShapeDtypeStruct((b, s, w), BF16)
    row_spec = pl.BlockSpec((1, tm, w), lambda bi, i: (bi, i, 0))
    return pl.pallas_call(
        _even_prep_kernel,
        grid=(b, s // tm),
        in_specs=[
            pl.BlockSpec((1, tm, d), lambda bi, i: (bi, i, 0)),
            pl.BlockSpec((1, d), const2),
            pl.BlockSpec((d, n_in), const2),
            pl.BlockSpec((1, SG_WIDTH), const2),
            pl.BlockSpec((1, SG_WIDTH), const2),
            pl.BlockSpec((SG_GROUPS, SG_CHUNK, SG_CHUNK), lambda bi, i: (0, 0, 0)),
            pl.BlockSpec((SG_CHUNK, SG_WIDTH), const2),
        ],
        out_specs=[pl.BlockSpec((1, w, tm), lambda bi, i: (bi, 0, i)),
                   row_spec,
                   pl.BlockSpec((1, tm // kb, w, kb), lambda bi, i: (bi, i, 0, 0)),
                   row_spec],
        out_shape=[jax.ShapeDtypeStruct((b, w, s), BF16), row_out,
                   jax.ShapeDtypeStruct((b, s // kb, w, kb), BF16), row_out],
        compiler_params=_params("parallel", "parallel"),
        name="even_prep",
    )(x, gain.reshape(1, d), w_in, ln_g.reshape(1, -1), ln_b.reshape(1, -1),
      sgu_w, sgu_bias_full)


def _sb_attn_kernel(qt_ref, k_ref, vt_ref, o_ref, acc_ref, r_ref, z0_ref, z1_ref, zc0_ref,
                    zc1_ref, t0_ref, t1_ref, bs_ref, kn_ref, *, tq, kb):
    i = pl.program_id(2)
    n_sub = tq // kb
    hd_dim = SB_HEAD_DIM
    z_refs, zc_refs, t_refs = (z0_ref, z1_ref), (zc0_ref, zc1_ref), (t0_ref, t1_ref)
    acc_refs = (acc_ref.at[0], acc_ref.at[1])
    first_head = lax.broadcasted_iota(jnp.int32, (LANES, 1), 0) < hd_dim
    qt = qt_ref[0]
    zero = jnp.zeros_like(qt)
    qt_heads = (jnp.where(first_head, qt, zero), jnp.where(first_head, zero, qt))
    visible_from = {
        lo: (lax.broadcasted_iota(jnp.int32, (kb, tq - lo), 0)
             < lax.broadcasted_iota(jnp.int32, (kb, tq - lo), 1))
        for lo in range(0, tq, kb)}
    srow = lax.broadcasted_iota(jnp.int32, (kb, kb), 0)
    scol = lax.broadcasted_iota(jnp.int32, (kb, kb), 1)
    suffix = (scol >= srow).astype(BF16)

    not_first_window = lax.broadcasted_iota(jnp.int32, (kb, tq), 1) >= kb

    def score(item):
        g, hd, slot, lo, hi = item
        start = pl.multiple_of(g * kb, kb)
        z_refs[slot][:, lo:hi] = _dot(k_ref[0, pl.ds(start, kb), :], qt_heads[hd][:, lo:hi])

    def stay(item, mask):
        g, hd, slot, lo, hi = item
        z = z_refs[slot][:, lo:hi]
        sp = jnp.maximum(z, 0.0) + jnp.log2(1.0 + jnp.exp2(-jnp.abs(z)))
        if mask is not None:
            if isinstance(mask, str):
                visible = visible_from[lo] if mask == "diagonal" else not_first_window
            else:
                visible = mask
            sp = jnp.where(visible, sp, 0.0)
            z = jnp.where(visible, z, MASKED)
        zc_refs[slot][:, lo:hi] = z
        tail = _dot(suffix, sp.astype(BF16))
        t_refs[slot][:, lo:hi] = tail
        parts = [jnp.broadcast_to(tail[0:1, :], (SUBLANES, hi - lo))]
        if lo:
            parts.insert(0, jnp.zeros((SUBLANES, lo), F32))
        if hi < tq:
            parts.append(jnp.zeros((SUBLANES, tq - hi), F32))
        bs_ref[slot] = jnp.concatenate(parts, axis=1) if len(parts) > 1 else parts[0]

    def weigh(item):
        g, hd, slot, lo, hi = item
        r = r_ref[hd]
        wgt = jnp.exp2(zc_refs[slot][:, lo:hi] - t_refs[slot][:, lo:hi] - r[0:1, lo:hi])
        vt = vt_ref[0, g, hd * hd_dim:(hd + 1) * hd_dim, :]
        acc_refs[hd][:, lo:hi] += _dot(vt, wgt.astype(BF16))
        r_ref[hd] = r + bs_ref[slot]

    def block(g, window, prev, nxt, mask):
        score((g, 1, 1, *window))
        if prev is not None:
            weigh((prev[0], 1, 1, *prev[1]))
        stay((g, 0, 0, *window), mask)
        if nxt is not None:
            score((nxt[0], 0, 0, *nxt[1]))
        weigh((g, 0, 0, *window))
        stay((g, 1, 1, *window), mask)

    @pl.when(i == 0)
    def _():
        kf = k_ref[0].astype(F32)
        dim = lax.broadcasted_iota(jnp.int32, (LANES, LANES), 0)
        head = lax.broadcasted_iota(jnp.int32, (LANES, LANES), 1)
        select = ((dim < hd_dim) == (head == 0)) & (head < 2)
        kn_ref[...] = jnp.max(_dot((kf * kf).astype(BF16), select.astype(BF16)),
                              axis=0, keepdims=True)

    lane = lax.broadcasted_iota(jnp.int32, (1, LANES), 1)
    qf = qt.astype(F32)
    exit_level = []
    for hd in range(2):
        q_sq = jnp.sum(jnp.square(qf[hd * hd_dim:(hd + 1) * hd_dim, :]), axis=0, keepdims=True)
        k_sq = jnp.max(jnp.where(lane == hd, kn_ref[...], 0.0), axis=1, keepdims=True)
        exit_level.append(1.02 * jnp.sqrt(q_sq * k_sq) + UNDERFLOW_LOG2)

    acc_ref[...] = jnp.zeros_like(acc_ref)
    r_ref[...] = jnp.zeros_like(r_ref)
    first = i * n_sub
    full = (0, tq)
    below = jnp.maximum(first - 1, 0)
    score((first + n_sub - 1, 0, 0, (n_sub - 1) * kb, tq))
    for d in reversed(range(n_sub)):
        prev = (first + d + 1, ((d + 1) * kb, tq)) if d < n_sub - 1 else None
        nxt = (first + d - 1, ((d - 1) * kb, tq)) if d > 0 else (below, (0, kb))
        block(first + d, (d * kb, tq), prev, nxt, "diagonal")

    def still_live():
        dead = jnp.min(jnp.minimum(r_ref[0] - exit_level[0], r_ref[1] - exit_level[1])) > 0.0
        return jnp.where(dead, 0, 1).astype(jnp.int32)

    block(below, (0, kb), (first, full), None, i > 0)
    weigh((below, 1, 1, 0, kb))

    @pl.when((still_live() > 0) & (i > 0))
    def _():
        score((below, 0, 0, *full))
        block(below, full, None, (jnp.maximum(below - 1, 0), full), "later windows")

        def more(carry):
            g, live = carry
            return (g >= 0) & (live > 0)

        def body(carry):
            g, _ = carry
            block(g, full, (g + 1, full), (jnp.maximum(g - 1, 0), full), None)
            return g - 1, still_live()

        g_end, _ = lax.while_loop(more, body, (below - 1, still_live()))
        weigh((g_end + 1, 1, 1, *full))

    out_t = jnp.concatenate([acc_ref[0], acc_ref[1]], axis=0)
    o_ref[0] = out_t.T.astype(BF16)


def _sb_attention(qt, k, vt, *, tq):
    b, w, s = qt.shape
    kb = vt.shape[3]
    return pl.pallas_call(
        functools.partial(_sb_attn_kernel, tq=tq, kb=kb),
        grid=(b, w // LANES, s // tq),
        in_specs=[
            pl.BlockSpec((1, LANES, tq), lambda bi, p, i: (bi, p, i)),
            pl.BlockSpec((1, s, LANES), lambda bi, p, i: (bi, 0, p)),
            pl.BlockSpec((1, s // kb, LANES, kb), lambda bi, p, i: (bi, 0, p, 0)),
        ],
        out_specs=pl.BlockSpec((1, tq, LANES), lambda bi, p, i: (bi, i, p)),
        out_shape=jax.ShapeDtypeStruct((b, s, w), BF16),
        scratch_shapes=[pltpu.VMEM((2, SB_HEAD_DIM, tq), F32),
                        pltpu.VMEM((2, SUBLANES, tq), F32),
                        *[pltpu.VMEM((kb, tq), F32) for _ in range(6)],
                        pltpu.VMEM((2, SUBLANES, tq), F32), pltpu.VMEM((1, LANES), F32)],
        compiler_params=_params("parallel", "parallel", "arbitrary"),
        name="sb_attn",
    )(qt, k, vt)


def _lane_tile(t, width):
    return jnp.concatenate([t] * (width // t.shape[1]), axis=1)


def _rms_rows(xt, gain):
    ms = jnp.sum(xt * xt, axis=0, keepdims=True) * (1.0 / xt.shape[0])
    return xt * lax.rsqrt(ms + EPS) * _lane_tile(gain, xt.shape[1])


def _mla_prep_kernel(x_ref, pos_ref, g_ref, win_ref, qlg_ref, kvlg_ref, wuqt_ref, wukt_ref,
                     wuvt_ref, qg_ref, kg_ref, freq_ref, qt_ref, k_ref, vt_ref):
    tm = x_ref.shape[1]
    kb = vt_ref.shape[3]
    lat = MLA_Q_LORA + MLA_KV_LORA
    half = MLA_ROPE // 2
    h = _rms(x_ref[0], g_ref[...]).astype(BF16)
    ct = _dot(h, win_ref[...]).T
    cqn = _rms_rows(ct[0:MLA_Q_LORA], qlg_ref[...]).astype(BF16)
    ckvn = _rms_rows(ct[MLA_Q_LORA:lat], kvlg_ref[...]).astype(BF16)
    k_r = ct[lat:lat + MLA_ROPE]

    group = 4

    def up_project(gi):
        rows = slice(gi * group * MLA_QK, (gi + 1) * group * MLA_QK)
        krows = slice(gi * group * MLA_NOPE, (gi + 1) * group * MLA_NOPE)
        return _dot(wuqt_ref[rows, :], cqn), _dot(wukt_ref[krows, :], ckvn)

    angle = _lane_tile(freq_ref[...], tm) * pos_ref[0].astype(F32)
    cos = jnp.cos(angle)
    sin = jnp.sin(angle)

    def rope(t):
        t1, t2 = t[0:half], t[half:]
        return jnp.concatenate([t1 * cos - t2 * sin, t1 * sin + t2 * cos], axis=0)

    qg = _lane_tile(qg_ref[...], tm)
    kg = _lane_tile(kg_ref[...], tm)
    zeros = jnp.zeros((LANES - MLA_QK, tm), F32)
    inv_n = 1.0 / MLA_QK
    kr_sq = jnp.sum(k_r * k_r, axis=0, keepdims=True)
    kr_roped = rope(k_r * kg[MLA_NOPE:])
    n_groups = MLA_HEADS // group
    pending = up_project(0)
    for gi in range(n_groups):
        q_grp, kn_grp = pending
        if gi + 1 < n_groups:
            pending = up_project(gi + 1)
        else:
            vt = _dot(wuvt_ref[...], ckvn)
            for c in range(tm // kb):
                vt_ref[0, c] = vt[:, c * kb:(c + 1) * kb].astype(BF16)
        for sub in range(group):
            hd = gi * group + sub
            qh = q_grp[sub * MLA_QK:(sub + 1) * MLA_QK]
            r = lax.rsqrt(jnp.sum(qh * qh, axis=0, keepdims=True) * inv_n + EPS)
            qn = qh * r * qg
            qt_ref[0, hd] = jnp.concatenate([qn[0:MLA_NOPE], rope(qn[MLA_NOPE:]), zeros],
                                            axis=0).astype(BF16)
            kn = kn_grp[sub * MLA_NOPE:(sub + 1) * MLA_NOPE]
            r = lax.rsqrt((jnp.sum(kn * kn, axis=0, keepdims=True) + kr_sq) * inv_n + EPS)
            kt = jnp.concatenate([kn * r * kg[0:MLA_NOPE], kr_roped * r, zeros], axis=0)
            k_ref[0, hd] = kt.T.astype(BF16)


def _mla_prep(x, positions, gain, w_in, qlg, kvlg, wuqt, wukt, wuvt, qg, kg, freq, *, tm, kb):
    b, s, d = x.shape
    vw = MLA_HEADS * MLA_V
    const2 = lambda bi, i: (0, 0)
    full = lambda a: pl.BlockSpec(a.shape, const2)
    small = [gain.reshape(1, -1), w_in, qlg, kvlg, wuqt, wukt, wuvt, qg, kg, freq]
    return pl.pallas_call(
        _mla_prep_kernel,
        grid=(b, s // tm),
        in_specs=[
            pl.BlockSpec((1, tm, d), lambda bi, i: (bi, i, 0)),
            pl.BlockSpec((1, 1, tm), lambda bi, i: (bi, 0, i)),
            *[full(a) for a in small],
        ],
        out_specs=[pl.BlockSpec((1, MLA_HEADS, LANES, tm), lambda bi, i: (bi, 0, 0, i)),
                   pl.BlockSpec((1, MLA_HEADS, tm, LANES), lambda bi, i: (bi, 0, i, 0)),
                   pl.BlockSpec((1, tm // kb, vw, kb), lambda bi, i: (bi, i, 0, 0))],
        out_shape=[jax.ShapeDtypeStruct((b, MLA_HEADS, LANES, s), BF16),
                   jax.ShapeDtypeStruct((b, MLA_HEADS, s, LANES), BF16),
                   jax.ShapeDtypeStruct((b, s // kb, vw, kb), BF16)],
        compiler_params=_params("parallel", "parallel"),
        name="mla_prep",
    )(x, positions.reshape(b, 1, s), *small)


def _mla_attn_kernel(qa_ref, qb_ref, k_ref, vt_ref, bound_ref, oa_ref, ob_ref, qt_ref, acc_ref,
                     m_ref, s0_ref, s1_ref, bm_ref, p0_ref, p1_ref, *, tq, n_tiles, fixed_shift):
    i = pl.program_id(2)
    n_heads = qa_ref.shape[1]
    s_refs, p_refs = (s0_ref, s1_ref), (p0_ref, p1_ref)
    key = lax.broadcasted_iota(jnp.int32, (tq, tq), 0)
    query = lax.broadcasted_iota(jnp.int32, (tq, tq), 1)
    causal = key <= query
    ones_rows = jnp.ones((SUM_ROWS, tq), BF16)
    qt_ref[0] = qa_ref[0]
    qt_ref[1] = qb_ref[0]

    def scores(item):
        g, hd, masked, which = item
        start = pl.multiple_of(g * tq, tq)
        sc = _dot(k_ref[0, hd, pl.ds(start, tq), :], qt_ref[which, hd])
        return jnp.where(causal, sc, MASKED) if masked else sc

    def values(item):
        g, hd = item[0], item[1]
        return jnp.concatenate([vt_ref[0, g, hd * MLA_V:(hd + 1) * MLA_V, :], ones_rows], axis=0)

    acc_ref[...] = jnp.zeros_like(acc_ref)

    if fixed_shift:
        bound = bound_ref[:, 0:1]

        def produce(item):
            p_refs[item[1] % 2][...] = jnp.exp2(scores(item) - bound).astype(BF16)

        def consume(item):
            hd, which = item[1], item[3]
            acc_ref[which, hd] += _dot(values(item), p_refs[hd % 2][...])
    else:
        m_ref[...] = jnp.full_like(m_ref, MASKED)

        def produce(item):
            sc = scores(item)
            s_refs[item[1] % 2][...] = sc
            bm_ref[item[1] % 2] = jnp.max(sc, axis=0, keepdims=True)

        def consume(item):
            hd, which = item[1], item[3]
            m_old = m_ref[which, hd]
            m_new = jnp.maximum(m_old, bm_ref[hd % 2])
            alpha = jnp.exp2(m_old - m_new)
            p = jnp.exp2(s_refs[hd % 2][...] - m_new)
            acc_ref[which, hd] = alpha * acc_ref[which, hd] + _dot(values(item), p.astype(BF16))
            m_ref[which, hd] = m_new

    visits = [(i, True, 0), (n_tiles - 1 - i, True, 1)]
    for j in range(n_tiles - 1):
        which = (j >= i).astype(jnp.int32)
        visits.append((j - i * which, False, which))
    items = [(g, hd, masked, which) for g, masked, which in visits for hd in range(n_heads)]
    produce(items[0])
    for idx, item in enumerate(items):
        if idx + 1 < len(items):
            produce(items[idx + 1])
        consume(item)

    for which, o_ref in enumerate((oa_ref, ob_ref)):
        out_t = jnp.concatenate(
            [acc_ref[which, hd, 0:MLA_V, :] / acc_ref[which, hd, MLA_V:MLA_V + 1, :]
             for hd in range(n_heads)], axis=0)
        o_ref[0] = out_t.T.astype(BF16)


def _mla_attention(qt, k, vt, score_bound, *, tq, heads_per_step):
    b, heads, _, s = qt.shape
    kb = vt.shape[3]
    nh = heads_per_step
    n_tiles = s // tq
    assert kb == tq and nh % 2 == 0 and n_tiles % 2 == 0
    half = n_tiles // 2
    vw = nh * MLA_V
    bound_lanes = jnp.full((1, LANES), score_bound, F32)
    half_out = jax.ShapeDtypeStruct((b, s // 2, heads * MLA_V), BF16)

    def call(fixed_shift):
        return pl.pallas_call(
            functools.partial(_mla_attn_kernel, tq=tq, n_tiles=n_tiles, fixed_shift=fixed_shift),
            grid=(b, heads // nh, half),
            in_specs=[
                pl.BlockSpec((1, nh, LANES, tq), lambda bi, p, i: (bi, p, 0, i)),
                pl.BlockSpec((1, nh, LANES, tq), lambda bi, p, i: (bi, p, 0, n_tiles - 1 - i)),
                pl.BlockSpec((1, nh, s, LANES), lambda bi, p, i: (bi, p, 0, 0)),
                pl.BlockSpec((1, s // kb, vw, kb), lambda bi, p, i: (bi, 0, p, 0)),
                pl.BlockSpec((1, LANES), lambda bi, p, i: (0, 0)),
            ],
            out_specs=[pl.BlockSpec((1, tq, vw), lambda bi, p, i: (bi, i, p)),
                       pl.BlockSpec((1, tq, vw), lambda bi, p, i: (bi, half - 1 - i, p))],
            out_shape=[half_out, half_out],
            scratch_shapes=[pltpu.VMEM((2, nh, LANES, tq), BF16),
                            pltpu.VMEM((2, nh, MLA_V + SUM_ROWS, tq), F32),
                            pltpu.VMEM((2, nh, 1, tq), F32), pltpu.VMEM((kb, tq), F32),
                            pltpu.VMEM((kb, tq), F32), pltpu.VMEM((2, 1, tq), F32),
                            pltpu.VMEM((kb, tq), BF16), pltpu.VMEM((kb, tq), BF16)],
            compiler_params=_params("parallel", "parallel", "arbitrary"),
            name="mla_attn_fixed_shift" if fixed_shift else "mla_attn_online_max",
        )(qt, qt, k, vt, bound_lanes)

    return lax.cond(score_bound <= FIXED_SHIFT_MAX, lambda: call(True), lambda: call(False))


def _mem_kv_kernel(mem_ref, g_ref, wkv_ref, kg_ref, k_ref, v_ref):
    hm = _rms(mem_ref[0], g_ref[...]).astype(BF16)
    hd_dim = kg_ref.shape[1]
    for hd in range(MEM_HEADS):
        kcols = slice(2 * hd * hd_dim, (2 * hd + 1) * hd_dim)
        vcols = slice((2 * hd + 1) * hd_dim, (2 * hd + 2) * hd_dim)
        out = slice(hd * hd_dim, (hd + 1) * hd_dim)
        k_ref[0, :, out] = _rms(_dot(hm, wkv_ref[:, kcols]), kg_ref[...]).astype(BF16)
        v_ref[0, :, out] = _dot(hm, wkv_ref[:, vcols]).astype(BF16)


def _mem_kv(mem, gain, wkv, k_gain):
    b, m, d = mem.shape
    out = jax.ShapeDtypeStruct((b, m, d), BF16)
    spec = pl.BlockSpec((1, m, d), lambda bi: (bi, 0, 0))
    return pl.pallas_call(
        _mem_kv_kernel,
        grid=(b,),
        in_specs=[spec,
                  pl.BlockSpec((1, d), lambda bi: (0, 0)),
                  pl.BlockSpec(wkv.shape, lambda bi: (0, 0)),
                  pl.BlockSpec((1, k_gain.shape[0]), lambda bi: (0, 0))],
        out_specs=[spec, spec],
        out_shape=[out, out],
        compiler_params=_params("parallel"),
        name="mem_kv",
    )(mem, gain.reshape(1, d), wkv, k_gain.reshape(1, -1))


def _mix_xattn_kernel(*refs, n_act, seq_halves):
    x_ref = refs[0]
    act_refs = refs[1:1 + n_act]
    w_ref, g_ref, wq_ref, qg_ref, k_ref, v_ref, wo_ref, o_ref = refs[1 + n_act:]
    if seq_halves:
        first_half = pl.program_id(1) < pl.num_programs(1) // 2
        act = jnp.where(first_half, act_refs[0][0], act_refs[1][0])
    else:
        act = jnp.concatenate([a_ref[0] for a_ref in act_refs], axis=-1)
    x1 = x_ref[0] + _dot(act, w_ref[...])
    h = _rms(x1, g_ref[...]).astype(BF16)
    hd_dim = qg_ref.shape[1]
    cols = [slice(hd * hd_dim, (hd + 1) * hd_dim) for hd in range(MEM_HEADS)]
    q = [_dot(h, wq_ref[:, c]) for c in cols]
    qn = [(_rms(q_h, qg_ref[...]) * (hd_dim ** -0.5)).astype(BF16) for q_h in q]
    sc = [_dot_nt(qn_h, k_ref[0, :, c]) for qn_h, c in zip(qn, cols)]
    p = [jnp.exp(s_h - jnp.max(s_h, axis=-1, keepdims=True)) for s_h in sc]
    o = [(_dot(p_h.astype(BF16), v_ref[0, :, c]) / jnp.sum(p_h, axis=-1, keepdims=True)
          ).astype(BF16) for p_h, c in zip(p, cols)]
    o_ref[0] = x1 + _dot(jnp.concatenate(o, axis=-1), wo_ref[...])


def _mix_xattn(x, acts, w_mix, gain, wq, q_gain, mem_k, mem_v, wo, *, tm, seq_halves=False):
    b, s, d = x.shape
    m = mem_k.shape[1]
    const2 = lambda bi, i: (0, 0)
    row = lambda width: pl.BlockSpec((1, tm, width), lambda bi, i: (bi, i, 0))
    mem_spec = pl.BlockSpec((1, m, d), lambda bi, i: (bi, 0, 0))
    if seq_halves:
        half = s // tm // 2
        width = acts[0].shape[-1]
        act_specs = [
            pl.BlockSpec((1, tm, width), lambda bi, i: (bi, jnp.minimum(i, half - 1), 0)),
            pl.BlockSpec((1, tm, width), lambda bi, i: (bi, jnp.maximum(i - half, 0), 0))]
    else:
        act_specs = [row(a.shape[-1]) for a in acts]
    return pl.pallas_call(
        functools.partial(_mix_xattn_kernel, n_act=len(acts), seq_halves=seq_halves),
        grid=(b, s // tm),
        in_specs=[
            row(d),
            *act_specs,
            pl.BlockSpec(w_mix.shape, const2),
            pl.BlockSpec((1, d), const2),
            pl.BlockSpec(wq.shape, const2),
            pl.BlockSpec((1, q_gain.shape[0]), const2),
            mem_spec, mem_spec,
            pl.BlockSpec(wo.shape, const2),
        ],
        out_specs=row(d),
        out_shape=jax.ShapeDtypeStruct((b, s, d), F32),
        compiler_params=_params("parallel", "parallel"),
        name="mix_xattn",
    )(x, *acts, w_mix, gain.reshape(1, d), wq, q_gain.reshape(1, -1), mem_k, mem_v, wo)


def _lane_bcast(vec):
    return jnp.broadcast_to(vec[:, None], (vec.shape[0], LANES))


def _mla_weights(w_in, w_uq, w_ukv, q_lora_gain, kv_lora_gain, q_gain, k_gain):
    lat = MLA_Q_LORA + MLA_KV_LORA
    w_in_ext = jnp.pad(w_in, ((0, 0), (0, lat + LANES - w_in.shape[1])))
    wukv = w_ukv.reshape(MLA_KV_LORA, MLA_HEADS, MLA_NOPE + MLA_V)
    wukt = wukv[..., :MLA_NOPE].reshape(MLA_KV_LORA, -1).T
    wuvt = wukv[..., MLA_NOPE:].reshape(MLA_KV_LORA, -1).T
    half = MLA_ROPE // 2
    inv_freq = ROPE_THETA ** (-jnp.arange(half, dtype=F32) / half)
    bf = lambda a: a.astype(BF16)
    return (bf(w_in_ext), _lane_bcast(q_lora_gain), _lane_bcast(kv_lora_gain),
            bf(w_uq.T), bf(wukt), bf(wuvt),
            _lane_bcast(q_gain * (MLA_QK ** -0.5 * LOG2E)), _lane_bcast(k_gain),
            _lane_bcast(inv_freq))


def _tile(n, pref):
    return pref if n % pref == 0 else n


def kernel(x, mem, positions, ffn_pre_norm, ffn_pre_w_gu, ffn_pre_w_down, mix_norm, sbg_w_in, sgu_ln_gain, sgu_ln_bias, sgu_w, sgu_b, sbg_w_out, mla_w_in, mla_q_lora_gain, mla_kv_lora_gain, mla_w_uq, mla_w_ukv, mla_q_gain, mla_k_gain, mla_w_out, xmem_norm, xmem_mem_norm, xmem_wq, xmem_wkv, xmem_q_gain, xmem_k_gain, xmem_wo, ffn_post_norm, ffn_post_w_gu, ffn_post_w_down):
    b, s, d = x.shape
    depth = ffn_pre_norm.shape[0]
    d_ff = ffn_pre_w_down.shape[1]
    t = b * s
    ffn_tm = _tile(t, 512)
    ffn_tf = 256
    row_tm = _tile(s, 512)
    mla_tq = _tile(s, 512)
    sb_tq = _tile(s, 512)
    sb_kb = _tile(sb_tq, 256)
    bf = lambda a: a.astype(BF16)

    for layer in range(depth):
        x = _ffn(x.reshape(t, d), ffn_pre_norm[layer], ffn_pre_w_gu, ffn_pre_w_down, layer,
                 tm=ffn_tm, tf=ffn_tf).reshape(b, s, d)
        if layer % 2 == 0:
            e = layer // 2
            bias_full = jnp.repeat(sgu_b[e].T, SG_GROUP_DIM, axis=1)
            qt, k, vt, o_sg = _even_prep(
                x, mix_norm[layer], bf(sbg_w_in[e]), sgu_ln_gain[e],
                sgu_ln_bias[e], sgu_w[e], bias_full, tm=row_tm, kb=sb_kb)
            acts = (_sb_attention(qt, k, vt, tq=sb_tq), o_sg)
            w_mix = bf(sbg_w_out[e])
        else:
            o = layer // 2
            mla_consts = _mla_weights(
                mla_w_in[o], mla_w_uq[o], mla_w_ukv[o], mla_q_lora_gain[o],
                mla_kv_lora_gain[o], mla_q_gain[o], mla_k_gain[o])
            qt, k, vt = _mla_prep(x, positions, mix_norm[layer], *mla_consts,
                                  tm=row_tm, kb=mla_tq)
            score_bound = (1.02 * MLA_QK * (MLA_QK ** -0.5 * LOG2E)
                           * jnp.max(jnp.abs(mla_q_gain[o])) * jnp.max(jnp.abs(mla_k_gain[o])))
            acts = _mla_attention(qt, k, vt, score_bound, tq=mla_tq, heads_per_step=4)
            w_mix = bf(mla_w_out[o])
        mem_k, mem_v = _mem_kv(mem, xmem_mem_norm[layer], bf(xmem_wkv[layer]),
                               xmem_k_gain[layer])
        x = _mix_xattn(x, acts, w_mix, xmem_norm[layer], bf(xmem_wq[layer]), xmem_q_gain[layer],
                       mem_k, mem_v, bf(xmem_wo[layer]), tm=row_tm, seq_halves=layer % 2 == 1)
        x = _ffn(x.reshape(t, d), ffn_post_norm[layer], ffn_post_w_gu, ffn_post_w_down, layer,
                 tm=ffn_tm, tf=ffn_tf).reshape(b, s, d)
    return x
```

```python
import functools

import jax
import jax.numpy as jnp
from jax import lax
from jax.experimental import pallas as pl
from jax.experimental.pallas import tpu as pltpu

EPS = 1e-6
ROPE_THETA = 10000.0
LANES = 128
SUBLANES = 8
VMEM_LIMIT_BYTES = 56 * 1024 * 1024

SB_HEADS, SB_HEAD_DIM = 8, 64
SB_WIDTH = SB_HEADS * SB_HEAD_DIM
SG_GROUPS, SG_GROUP_DIM, SG_CHUNK = 8, 64, 128
SG_WIDTH = SG_GROUPS * SG_GROUP_DIM
MLA_HEADS, MLA_NOPE, MLA_ROPE, MLA_V = 16, 64, 32, 64
MLA_QK = MLA_NOPE + MLA_ROPE
MLA_Q_LORA, MLA_KV_LORA = 512, 256
MEM_HEADS = 4

BF16 = jnp.bfloat16
F32 = jnp.float32
LOG2E = 1.4426950408889634
MASKED = -1e30
SUM_ROWS = 16
FIXED_SHIFT_MAX = 60.0
UNDERFLOW_LOG2 = 160.0


def _params(*semantics):
    return pltpu.CompilerParams(dimension_semantics=semantics,
                                vmem_limit_bytes=VMEM_LIMIT_BYTES)


def _dot(a, b):
    return jnp.dot(a, b, preferred_element_type=F32)


def _dot_nt(a, b):
    return lax.dot_general(a, b, (((1,), (1,)), ((), ())), preferred_element_type=F32)


def _rms(x, gain):
    ms = jnp.sum(x * x, axis=-1, keepdims=True) * (1.0 / x.shape[-1])
    return x * lax.rsqrt(ms + EPS) * gain


def _ffn_kernel(x_ref, g_ref, wgu_ref, wd_ref, o_ref, *, tf):
    d_ff = wd_ref.shape[0]
    x = x_ref[...]
    h = _rms(x, g_ref[...]).astype(BF16)
    acc = None
    for lo in range(0, d_ff, tf):
        hi = min(lo + tf, d_ff)
        gate = _dot(h, wgu_ref[:, lo:hi].astype(BF16))
        up = _dot(h, wgu_ref[:, d_ff + lo:d_ff + hi].astype(BF16))
        act = (gate * jax.nn.sigmoid(gate) * up).astype(BF16)
        part = _dot(act, wd_ref[lo:hi, :].astype(BF16))
        acc = part if acc is None else acc + part
    o_ref[...] = x + 0.5 * acc


def _ffn(x2, gain, w_gu, w_down, layer, *, tm, tf):
    t, d = x2.shape
    d_ff = w_down.shape[1]
    resident = pl.Buffered(1)
    return pl.pallas_call(
        functools.partial(_ffn_kernel, tf=tf),
        grid=(t // tm,),
        in_specs=[
            pl.BlockSpec((tm, d), lambda i: (i, 0)),
            pl.BlockSpec((1, d), lambda i: (0, 0)),
            pl.BlockSpec((None, d, 2 * d_ff), lambda i: (layer, 0, 0), pipeline_mode=resident),
            pl.BlockSpec((None, d_ff, d), lambda i: (layer, 0, 0), pipeline_mode=resident),
        ],
        out_specs=pl.BlockSpec((tm, d), lambda i: (i, 0)),
        out_shape=jax.ShapeDtypeStruct((t, d), F32),
        compiler_params=_params("parallel"),
        name="ffn",
    )(x2, gain.reshape(1, d), w_gu, w_down)


def _gelu_tanh(x):
    c = 0.7978845608028654
    return 0.5 * x * (1.0 + jnp.tanh(c * (x + 0.044715 * (x * x * x))))


def _even_prep_kernel(x_ref, g_ref, win_ref, lng_ref, lnb_ref, sw_ref, sb_ref,
                      qt_ref, k_ref, vt_ref, osg_ref):
    tm = x_ref.shape[1]
    kb = vt_ref.shape[3]
    w = SB_WIDTH
    h = _rms(x_ref[0], g_ref[...]).astype(BF16)
    osg_ref = osg_ref.at[0]
    u_raw = _dot(h, win_ref[:, 3 * w:3 * w + SG_WIDTH])
    g_raw = _dot(h, win_ref[:, 3 * w + SG_WIDTH:3 * w + 2 * SG_WIDTH])
    qt_ref[0] = (_dot(h, win_ref[:, 0:w]) * (SB_HEAD_DIM ** -0.5 * LOG2E)).T.astype(BF16)
    u = _gelu_tanh(u_raw)
    k_ref[0] = _dot(h, win_ref[:, w:2 * w]).astype(BF16)
    g = _gelu_tanh(g_raw)
    mu = jnp.mean(g, axis=-1, keepdims=True)
    gc = g - mu
    var = jnp.mean(gc * gc, axis=-1, keepdims=True)
    gn = (gc * lax.rsqrt(var + EPS) * lng_ref[...] + lnb_ref[...]).astype(BF16)
    v = _dot(h, win_ref[:, 2 * w:3 * w])
    for c in range(tm // kb):
        vt_ref[0, c] = v[c * kb:(c + 1) * kb, :].T.astype(BF16)

    row = lax.broadcasted_iota(jnp.int32, (SG_CHUNK, SG_CHUNK), 0)
    col = lax.broadcasted_iota(jnp.int32, (SG_CHUNK, SG_CHUNK), 1)
    tri = col <= row
    first_group = lax.broadcasted_iota(jnp.int32, (SG_CHUNK, LANES), 1) < SG_GROUP_DIM
    for p in range(SG_GROUPS // 2):
        lanes = slice(p * LANES, (p + 1) * LANES)
        w0 = jnp.where(tri, sw_ref[2 * p], 0.0).astype(BF16)
        w1 = jnp.where(tri, sw_ref[2 * p + 1], 0.0).astype(BF16)
        bias = sb_ref[:, lanes]
        for c in range(tm // SG_CHUNK):
            rows = slice(c * SG_CHUNK, (c + 1) * SG_CHUNK)
            gp = gn[rows, lanes]
            mixed = jnp.where(first_group, _dot(w0, gp), _dot(w1, gp)) + bias
            osg_ref[rows, lanes] = (u[rows, lanes] * mixed).astype(BF16)


def _even_prep(x, gain, w_in, ln_g, ln_b, sgu_w, sgu_bias_full, *, tm, kb):
    b, s, d = x.shape
    n_in = w_in.shape[1]
    w = SB_WIDTH
    const2 = lambda bi, i: (0, 0)
    row_out = jax.ShapeDtypeStruct((b, s, w), BF16)
    row_spec = pl.BlockSpec((1, tm, w), lambda bi, i: (bi, i, 0))
    return pl.pallas_call(
        _even_prep_kernel,
        grid=(b, s // tm),
        in_specs=[
            pl.BlockSpec((1, tm, d), lambda bi, i: (bi, i, 0)),
            pl.BlockSpec((1, d), const2),
            pl.BlockSpec((d, n_in), const2),
            pl.BlockSpec((1, SG_WIDTH), const2),
            pl.BlockSpec((1, SG_WIDTH), const2),
            pl.BlockSpec((SG_GROUPS, SG_CHUNK, SG_CHUNK), lambda bi, i: (0, 0, 0)),
            pl.BlockSpec((SG_CHUNK, SG_WIDTH), const2),
        ],
        out_specs=[pl.BlockSpec((1, w, tm), lambda bi, i: (bi, 0, i)),
                   row_spec,
                   pl.BlockSpec((1, tm // kb, w, kb), lambda bi, i: (bi, i, 0, 0)),
                   row_spec],
        out_shape=[jax.ShapeDtypeStruct((b, w, s), BF16), row_out,
                   jax.ShapeDtypeStruct((b, s // kb, w, kb), BF16), row_out],
        compiler_params=_params("parallel", "parallel"),
        name="even_prep",
    )(x, gain.reshape(1, d), w_in, ln_g.reshape(1, -1), ln_b.reshape(1, -1),
      sgu_w, sgu_bias_full)


def _sb_attn_kernel(qt_ref, k_ref, vt_ref, o_ref, acc_ref, r_ref, z0_ref, z1_ref, zc0_ref,
                    zc1_ref, t0_ref, t1_ref, bs_ref, kn_ref, *, tq, kb):
    i = pl.program_id(2)
    n_sub = tq // kb
    hd_dim = SB_HEAD_DIM
    z_refs, zc_refs, t_refs = (z0_ref, z1_ref), (zc0_ref, zc1_ref), (t0_ref, t1_ref)
    acc_refs = (acc_ref.at[0], acc_ref.at[1])
    first_head = lax.broadcasted_iota(jnp.int32, (LANES, 1), 0) < hd_dim
    qt = qt_ref[0]
    zero = jnp.zeros_like(qt)
    qt_heads = (jnp.where(first_head, qt, zero), jnp.where(first_head, zero, qt))
    visible_from = {
        lo: (lax.broadcasted_iota(jnp.int32, (kb, tq - lo), 0)
             < lax.broadcasted_iota(jnp.int32, (kb, tq - lo), 1))
        for lo in range(0, tq, kb)}
    srow = lax.broadcasted_iota(jnp.int32, (kb, kb), 0)
    scol = lax.broadcasted_iota(jnp.int32, (kb, kb), 1)
    suffix = (scol >= srow).astype(BF16)

    not_first_window = lax.broadcasted_iota(jnp.int32, (kb, tq), 1) >= kb

    def score(item):
        g, hd, slot, lo, hi = item
        start = pl.multiple_of(g * kb, kb)
        z_refs[slot][:, lo:hi] = _dot(k_ref[0, pl.ds(start, kb), :], qt_heads[hd][:, lo:hi])

    def stay(item, mask):
        g, hd, slot, lo, hi = item
        z = z_refs[slot][:, lo:hi]
        sp = jnp.maximum(z, 0.0) + jnp.log2(1.0 + jnp.exp2(-jnp.abs(z)))
        if mask is not None:
            if isinstance(mask, str):
                visible = visible_from[lo] if mask == "diagonal" else not_first_window
            else:
                visible = mask
            sp = jnp.where(visible, sp, 0.0)
            z = jnp.where(visible, z, MASKED)
        zc_refs[slot][:, lo:hi] = z
        tail = _dot(suffix, sp.astype(BF16))
        t_refs[slot][:, lo:hi] = tail
        parts = [jnp.broadcast_to(tail[0:1, :], (SUBLANES, hi - lo))]
        if lo:
            parts.insert(0, jnp.zeros((SUBLANES, lo), F32))
        if hi < tq:
            parts.append(jnp.zeros((SUBLANES, tq - hi), F32))
        bs_ref[slot] = jnp.concatenate(parts, axis=1) if len(parts) > 1 else parts[0]

    def weigh(item):
        g, hd, slot, lo, hi = item
        r = r_ref[hd]
        wgt = jnp.exp2(zc_refs[slot][:, lo:hi] - t_refs[slot][:, lo:hi] - r[0:1, lo:hi])
        vt = vt_ref[0, g, hd * hd_dim:(hd + 1) * hd_dim, :]
        acc_refs[hd][:, lo:hi] += _dot(vt, wgt.astype(BF16))
        r_ref[hd] = r + bs_ref[slot]

    def block(g, window, prev, nxt, mask):
        score((g, 1, 1, *window))
        if prev is not None:
            weigh((prev[0], 1, 1, *prev[1]))
        stay((g, 0, 0, *window), mask)
        if nxt is not None:
            score((nxt[0], 0, 0, *nxt[1]))
        weigh((g, 0, 0, *window))
        stay((g, 1, 1, *window), mask)

    @pl.when(i == 0)
    def _():
        kf = k_ref[0].astype(F32)
        dim = lax.broadcasted_iota(jnp.int32, (LANES, LANES), 0)
        head = lax.broadcasted_iota(jnp.int32, (LANES, LANES), 1)
        select = ((dim < hd_dim) == (head == 0)) & (head < 2)
        kn_ref[...] = jnp.max(_dot((kf * kf).astype(BF16), select.astype(BF16)),
                              axis=0, keepdims=True)

    lane = lax.broadcasted_iota(jnp.int32, (1, LANES), 1)
    qf = qt.astype(F32)
    exit_level = []
    for hd in range(2):
        q_sq = jnp.sum(jnp.square(qf[hd * hd_dim:(hd + 1) * hd_dim, :]), axis=0, keepdims=True)
        k_sq = jnp.max(jnp.where(lane == hd, kn_ref[...], 0.0), axis=1, keepdims=True)
        exit_level.append(1.02 * jnp.sqrt(q_sq * k_sq) + UNDERFLOW_LOG2)

    acc_ref[...] = jnp.zeros_like(acc_ref)
    r_ref[...] = jnp.zeros_like(r_ref)
    first = i * n_sub
    full = (0, tq)
    below = jnp.maximum(first - 1, 0)
    score((first + n_sub - 1, 0, 0, (n_sub - 1) * kb, tq))
    for d in reversed(range(n_sub)):
        prev = (first + d + 1, ((d + 1) * kb, tq)) if d < n_sub - 1 else None
        nxt = (first + d - 1, ((d - 1) * kb, tq)) if d > 0 else (below, (0, kb))
        block(first + d, (d * kb, tq), prev, nxt, "diagonal")

    def still_live():
        dead = jnp.min(jnp.minimum(r_ref[0] - exit_level[0], r_ref[1] - exit_level[1])) > 0.0
        return jnp.where(dead, 0, 1).astype(jnp.int32)

    block(below, (0, kb), (first, full), None, i > 0)
    weigh((below, 1, 1, 0, kb))

    @pl.when((still_live() > 0) & (i > 0))
    def _():
        score((below, 0, 0, *full))
        block(below, full, None, (jnp.maximum(below - 1, 0), full), "later windows")

        def more(carry):
            g, live = carry
            return (g >= 0) & (live > 0)

        def body(carry):
            g, _ = carry
            block(g, full, (g + 1, full), (jnp.maximum(g - 1, 0), full), None)
            return g - 1, still_live()

        g_end, _ = lax.while_loop(more, body, (below - 1, still_live()))
        weigh((g_end + 1, 1, 1, *full))

    out_t = jnp.concatenate([acc_ref[0], acc_ref[1]], axis=0)
    o_ref[0] = out_t.T.astype(BF16)


def _sb_attention(qt, k, vt, *, tq):
    b, w, s = qt.shape
    kb = vt.shape[3]
    return pl.pallas_call(
        functools.partial(_sb_attn_kernel, tq=tq, kb=kb),
        grid=(b, w // LANES, s // tq),
        in_specs=[
            pl.BlockSpec((1, LANES, tq), lambda bi, p, i: (bi, p, i)),
            pl.BlockSpec((1, s, LANES), lambda bi, p, i: (bi, 0, p)),
            pl.BlockSpec((1, s // kb, LANES, kb), lambda bi, p, i: (bi, 0, p, 0)),
        ],
        out_specs=pl.BlockSpec((1, tq, LANES), lambda bi, p, i: (bi, i, p)),
        out_shape=jax.ShapeDtypeStruct((b, s, w), BF16),
        scratch_shapes=[pltpu.VMEM((2, SB_HEAD_DIM, tq), F32),
                        pltpu.VMEM((2, SUBLANES, tq), F32),
                        *[pltpu.VMEM((kb, tq), F32) for _ in range(6)],
                        pltpu.VMEM((2, SUBLANES, tq), F32), pltpu.VMEM((1, LANES), F32)],
        compiler_params=_params("parallel", "parallel", "arbitrary"),
        name="sb_attn",
    )(qt, k, vt)


def _lane_tile(t, width):
    return jnp.concatenate([t] * (width // t.shape[1]), axis=1)


def _rms_rows(xt, gain):
    ms = jnp.sum(xt * xt, axis=0, keepdims=True) * (1.0 / xt.shape[0])
    return xt * lax.rsqrt(ms + EPS) * _lane_tile(gain, xt.shape[1])


def _mla_prep_kernel(x_ref, pos_ref, g_ref, win_ref, qlg_ref, kvlg_ref, wuqt_ref, wukt_ref,
                     wuvt_ref, qg_ref, kg_ref, freq_ref, qt_ref, k_ref, vt_ref):
    tm = x_ref.shape[1]
    kb = vt_ref.shape[3]
    lat = MLA_Q_LORA + MLA_KV_LORA
    half = MLA_ROPE // 2
    h = _rms(x_ref[0], g_ref[...]).astype(BF16)
    ct = _dot(h, win_ref[...]).T
    cqn = _rms_rows(ct[0:MLA_Q_LORA], qlg_ref[...]).astype(BF16)
    ckvn = _rms_rows(ct[MLA_Q_LORA:lat], kvlg_ref[...]).astype(BF16)
    k_r = ct[lat:lat + MLA_ROPE]

    group = 4

    def up_project(gi):
        rows = slice(gi * group * MLA_QK, (gi + 1) * group * MLA_QK)
        krows = slice(gi * group * MLA_NOPE, (gi + 1) * group * MLA_NOPE)
        return _dot(wuqt_ref[rows, :], cqn), _dot(wukt_ref[krows, :], ckvn)

    angle = _lane_tile(freq_ref[...], tm) * pos_ref[0].astype(F32)
    cos = jnp.cos(angle)
    sin = jnp.sin(angle)

    def rope(t):
        t1, t2 = t[0:half], t[half:]
        return jnp.concatenate([t1 * cos - t2 * sin, t1 * sin + t2 * cos], axis=0)

    qg = _lane_tile(qg_ref[...], tm)
    kg = _lane_tile(kg_ref[...], tm)
    zeros = jnp.zeros((LANES - MLA_QK, tm), F32)
    inv_n = 1.0 / MLA_QK
    kr_sq = jnp.sum(k_r * k_r, axis=0, keepdims=True)
    kr_roped = rope(k_r * kg[MLA_NOPE:])
    n_groups = MLA_HEADS // group
    pending = up_project(0)
    for gi in range(n_groups):
        q_grp, kn_grp = pending
        if gi + 1 < n_groups:
            pending = up_project(gi + 1)
        else:
            vt = _dot(wuvt_ref[...], ckvn)
            for c in range(tm // kb):
                vt_ref[0, c] = vt[:, c * kb:(c + 1) * kb].astype(BF16)
        for sub in range(group):
            hd = gi * group + sub
            qh = q_grp[sub * MLA_QK:(sub + 1) * MLA_QK]
            r = lax.rsqrt(jnp.sum(qh * qh, axis=0, keepdims=True) * inv_n + EPS)
            qn = qh * r * qg
            qt_ref[0, hd] = jnp.concatenate([qn[0:MLA_NOPE], rope(qn[MLA_NOPE:]), zeros],
                                            axis=0).astype(BF16)
            kn = kn_grp[sub * MLA_NOPE:(sub + 1) * MLA_NOPE]
            r = lax.rsqrt((jnp.sum(kn * kn, axis=0, keepdims=True) + kr_sq) * inv_n + EPS)
            kt = jnp.concatenate([kn * r * kg[0:MLA_NOPE], kr_roped * r, zeros], axis=0)
            k_ref[0, hd] = kt.T.astype(BF16)


def _mla_prep(x, positions, gain, w_in, qlg, kvlg, wuqt, wukt, wuvt, qg, kg, freq, *, tm, kb):
    b, s, d = x.shape
    vw = MLA_HEADS * MLA_V
    const2 = lambda bi, i: (0, 0)
    full = lambda a: pl.BlockSpec(a.shape, const2)
    small = [gain.reshape(1, -1), w_in, qlg, kvlg, wuqt, wukt, wuvt, qg, kg, freq]
    return pl.pallas_call(
        _mla_prep_kernel,
        grid=(b, s // tm),
        in_specs=[
            pl.BlockSpec((1, tm, d), lambda bi, i: (bi, i, 0)),
            pl.BlockSpec((1, 1, tm), lambda bi, i: (bi, 0, i)),
            *[full(a) for a in small],
        ],
        out_specs=[pl.BlockSpec((1, MLA_HEADS, LANES, tm), lambda bi, i: (bi, 0, 0, i)),
                   pl.BlockSpec((1, MLA_HEADS, tm, LANES), lambda bi, i: (bi, 0, i, 0)),
                   pl.BlockSpec((1, tm // kb, vw, kb), lambda bi, i: (bi, i, 0, 0))],
        out_shape=[jax.ShapeDtypeStruct((b, MLA_HEADS, LANES, s), BF16),
                   jax.ShapeDtypeStruct((b, MLA_HEADS, s, LANES), BF16),
                   jax.ShapeDtypeStruct((b, s // kb, vw, kb), BF16)],
        compiler_params=_params("parallel", "parallel"),
        name="mla_prep",
    )(x, positions.reshape(b, 1, s), *small)


def _mla_attn_kernel(qa_ref, qb_ref, k_ref, vt_ref, bound_ref, oa_ref, ob_ref, qt_ref, acc_ref,
                     m_ref, s0_ref, s1_ref, bm_ref, p0_ref, p1_ref, *, tq, n_tiles, fixed_shift):
    i = pl.program_id(2)
    n_heads = qa_ref.shape[1]
    s_refs, p_refs = (s0_ref, s1_ref), (p0_ref, p1_ref)
    key = lax.broadcasted_iota(jnp.int32, (tq, tq), 0)
    query = lax.broadcasted_iota(jnp.int32, (tq, tq), 1)
    causal = key <= query
    ones_rows = jnp.ones((SUM_ROWS, tq), BF16)
    qt_ref[0] = qa_ref[0]
    qt_ref[1] = qb_ref[0]

    def scores(item):
        g, hd, masked, which = item
        start = pl.multiple_of(g * tq, tq)
        sc = _dot(k_ref[0, hd, pl.ds(start, tq), :], qt_ref[which, hd])
        return jnp.where(causal, sc, MASKED) if masked else sc

    def values(item):
        g, hd = item[0], item[1]
        return jnp.concatenate([vt_ref[0, g, hd * MLA_V:(hd + 1) * MLA_V, :], ones_rows], axis=0)

    acc_ref[...] = jnp.zeros_like(acc_ref)

    if fixed_shift:
        bound = bound_ref[:, 0:1]

        def produce(item):
            p_refs[item[1] % 2][...] = jnp.exp2(scores(item) - bound).astype(BF16)

        def consume(item):
            hd, which = item[1], item[3]
            acc_ref[which, hd] += _dot(values(item), p_refs[hd % 2][...])
    else:
        m_ref[...] = jnp.full_like(m_ref, MASKED)

        def produce(item):
            sc = scores(item)
            s_refs[item[1] % 2][...] = sc
            bm_ref[item[1] % 2] = jnp.max(sc, axis=0, keepdims=True)

        def consume(item):
            hd, which = item[1], item[3]
            m_old = m_ref[which, hd]
            m_new = jnp.maximum(m_old, bm_ref[hd % 2])
            alpha = jnp.exp2(m_old - m_new)
            p = jnp.exp2(s_refs[hd % 2][...] - m_new)
            acc_ref[which, hd] = alpha * acc_ref[which, hd] + _dot(values(item), p.astype(BF16))
            m_ref[which, hd] = m_new

    visits = [(i, True, 0), (n_tiles - 1 - i, True, 1)]
    for j in range(n_tiles - 1):
        which = (j >= i).astype(jnp.int32)
        visits.append((j - i * which, False, which))
    items = [(g, hd, masked, which) for g, masked, which in visits for hd in range(n_heads)]
    produce(items[0])
    for idx, item in enumerate(items):
        if idx + 1 < len(items):
            produce(items[idx + 1])
        consume(item)

    for which, o_ref in enumerate((oa_ref, ob_ref)):
        out_t = jnp.concatenate(
            [acc_ref[which, hd, 0:MLA_V, :] / acc_ref[which, hd, MLA_V:MLA_V + 1, :]
             for hd in range(n_heads)], axis=0)
        o_ref[0] = out_t.T.astype(BF16)


def _mla_attention(qt, k, vt, score_bound, *, tq, heads_per_step):
    b, heads, _, s = qt.shape
    kb = vt.shape[3]
    nh = heads_per_step
    n_tiles = s // tq
    assert kb == tq and nh % 2 == 0 and n_tiles % 2 == 0
    half = n_tiles // 2
    vw = nh * MLA_V
    bound_lanes = jnp.full((1, LANES), score_bound, F32)
    half_out = jax.ShapeDtypeStruct((b, s // 2, heads * MLA_V), BF16)

    def call(fixed_shift):
        return pl.pallas_call(
            functools.partial(_mla_attn_kernel, tq=tq, n_tiles=n_tiles, fixed_shift=fixed_shift),
            grid=(b, heads // nh, half),
            in_specs=[
                pl.BlockSpec((1, nh, LANES, tq), lambda bi, p, i: (bi, p, 0, i)),
                pl.BlockSpec((1, nh, LANES, tq), lambda bi, p, i: (bi, p, 0, n_tiles - 1 - i)),
                pl.BlockSpec((1, nh, s, LANES), lambda bi, p, i: (bi, p, 0, 0)),
                pl.BlockSpec((1, s // kb, vw, kb), lambda bi, p, i: (bi, 0, p, 0)),
                pl.BlockSpec((1, LANES), lambda bi, p, i: (0, 0)),
            ],
            out_specs=[pl.BlockSpec((1, tq, vw), lambda bi, p, i: (bi, i, p)),
                       pl.BlockSpec((1, tq, vw), lambda bi, p, i: (bi, half - 1 - i, p))],
            out_shape=[half_out, half_out],
            scratch_shapes=[pltpu.VMEM((2, nh, LANES, tq), BF16),
                            pltpu.VMEM((2, nh, MLA_V + SUM_ROWS, tq), F32),
                            pltpu.VMEM((2, nh, 1, tq), F32), pltpu.VMEM((kb, tq), F32),
                            pltpu.VMEM((kb, tq), F32), pltpu.VMEM((2, 1, tq), F32),
                            pltpu.VMEM((kb, tq), BF16), pltpu.VMEM((kb, tq), BF16)],
            compiler_params=_params("parallel", "parallel", "arbitrary"),
            name="mla_attn_fixed_shift" if fixed_shift else "mla_attn_online_max",
        )(qt, qt, k, vt, bound_lanes)

    return lax.cond(score_bound <= FIXED_SHIFT_MAX, lambda: call(True), lambda: call(False))


def _mem_kv_kernel(mem_ref, g_ref, wkv_ref, kg_ref, k_ref, v_ref):
    hm = _rms(mem_ref[0], g_ref[...]).astype(BF16)
    hd_dim = kg_ref.shape[1]
    for hd in range(MEM_HEADS):
        kcols = slice(2 * hd * hd_dim, (2 * hd + 1) * hd_dim)
        vcols = slice((2 * hd + 1) * hd_dim, (2 * hd + 2) * hd_dim)
        out = slice(hd * hd_dim, (hd + 1) * hd_dim)
        k_ref[0, :, out] = _rms(_dot(hm, wkv_ref[:, kcols]), kg_ref[...]).astype(BF16)
        v_ref[0, :, out] = _dot(hm, wkv_ref[:, vcols]).astype(BF16)


def _mem_kv(mem, gain, wkv, k_gain):
    b, m, d = mem.shape
    out = jax.ShapeDtypeStruct((b, m, d), BF16)
    spec = pl.BlockSpec((1, m, d), lambda bi: (bi, 0, 0))
    return pl.pallas_call(
        _mem_kv_kernel,
        grid=(b,),
        in_specs=[spec,
                  pl.BlockSpec((1, d), lambda bi: (0, 0)),
                  pl.BlockSpec(wkv.shape, lambda bi: (0, 0)),
                  pl.BlockSpec((1, k_gain.shape[0]), lambda bi: (0, 0))],
        out_specs=[spec, spec],
        out_shape=[out, out],
        compiler_params=_params("parallel"),
        name="mem_kv",
    )(mem, gain.reshape(1, d), wkv, k_gain.reshape(1, -1))


def _mix_xattn_kernel(*refs, n_act, seq_halves):
    x_ref = refs[0]
    act_refs = refs[1:1 + n_act]
    w_ref, g_ref, wq_ref, qg_ref, k_ref, v_ref, wo_ref, o_ref = refs[1 + n_act:]
    if seq_halves:
        first_half = pl.program_id(1) < pl.num_programs(1) // 2
        act = jnp.where(first_half, act_refs[0][0], act_refs[1][0])
    else:
        act = jnp.concatenate([a_ref[0] for a_ref in act_refs], axis=-1)
    x1 = x_ref[0] + _dot(act, w_ref[...])
    h = _rms(x1, g_ref[...]).astype(BF16)
    hd_dim = qg_ref.shape[1]
    cols = [slice(hd * hd_dim, (hd + 1) * hd_dim) for hd in range(MEM_HEADS)]
    q = [_dot(h, wq_ref[:, c]) for c in cols]
    qn = [(_rms(q_h, qg_ref[...]) * (hd_dim ** -0.5)).astype(BF16) for q_h in q]
    sc = [_dot_nt(qn_h, k_ref[0, :, c]) for qn_h, c in zip(qn, cols)]
    p = [jnp.exp(s_h - jnp.max(s_h, axis=-1, keepdims=True)) for s_h in sc]
    o = [(_dot(p_h.astype(BF16), v_ref[0, :, c]) / jnp.sum(p_h, axis=-1, keepdims=True)
          ).astype(BF16) for p_h, c in zip(p, cols)]
    o_ref[0] = x1 + _dot(jnp.concatenate(o, axis=-1), wo_ref[...])


def _mix_xattn(x, acts, w_mix, gain, wq, q_gain, mem_k, mem_v, wo, *, tm, seq_halves=False):
    b, s, d = x.shape
    m = mem_k.shape[1]
    const2 = lambda bi, i: (0, 0)
    row = lambda width: pl.BlockSpec((1, tm, width), lambda bi, i: (bi, i, 0))
    mem_spec = pl.BlockSpec((1, m, d), lambda bi, i: (bi, 0, 0))
    if seq_halves:
        half = s // tm // 2
        width = acts[0].shape[-1]
        act_specs = [
            pl.BlockSpec((1, tm, width), lambda bi, i: (bi, jnp.minimum(i, half - 1), 0)),
            pl.BlockSpec((1, tm, width), lambda bi, i: (bi, jnp.maximum(i - half, 0), 0))]
    else:
        act_specs = [row(a.shape[-1]) for a in acts]
    return pl.pallas_call(
        functools.partial(_mix_xattn_kernel, n_act=len(acts), seq_halves=seq_halves),
        grid=(b, s // tm),
        in_specs=[
            row(d),
            *act_specs,
            pl.BlockSpec(w_mix.shape, const2),
            pl.BlockSpec((1, d), const2),
            pl.BlockSpec(wq.shape, const2),
            pl.BlockSpec((1, q_gain.shape[0]), const2),
            mem_spec, mem_spec,
            pl.BlockSpec(wo.shape, const2),
        ],
        out_specs=row(d),
        out_shape=jax.ShapeDtypeStruct((b, s, d), F32),
        compiler_params=_params("parallel", "parallel"),
        name="mix_xattn",
    )(x, *acts, w_mix, gain.reshape(1, d), wq, q_gain.reshape(1, -1), mem_k, mem_v, wo)


def _lane_bcast(vec):
    return jnp.broadcast_to(vec[:, None], (vec.shape[0], LANES))


def _mla_weights(w_in, w_uq, w_ukv, q_lora_gain, kv_lora_gain, q_gain, k_gain):
    lat = MLA_Q_LORA + MLA_KV_LORA
    w_in_ext = jnp.pad(w_in, ((0, 0), (0, lat + LANES - w_in.shape[1])))
    wukv = w_ukv.reshape(MLA_KV_LORA, MLA_HEADS, MLA_NOPE + MLA_V)
    wukt = wukv[..., :MLA_NOPE].reshape(MLA_KV_LORA, -1).T
    wuvt = wukv[..., MLA_NOPE:].reshape(MLA_KV_LORA, -1).T
    half = MLA_ROPE // 2
    inv_freq = ROPE_THETA ** (-jnp.arange(half, dtype=F32) / half)
    bf = lambda a: a.astype(BF16)
    return (bf(w_in_ext), _lane_bcast(q_lora_gain), _lane_bcast(kv_lora_gain),
            bf(w_uq.T), bf(wukt), bf(wuvt),
            _lane_bcast(q_gain * (MLA_QK ** -0.5 * LOG2E)), _lane_bcast(k_gain),
            _lane_bcast(inv_freq))


def _tile(n, pref):
    return pref if n % pref == 0 else n


def kernel(x, mem, positions, ffn_pre_norm, ffn_pre_w_gu, ffn_pre_w_down, mix_norm, sbg_w_in, sgu_ln_gain, sgu_ln_bias, sgu_w, sgu_b, sbg_w_out, mla_w_in, mla_q_lora_gain, mla_kv_lora_gain, mla_w_uq, mla_w_ukv, mla_q_gain, mla_k_gain, mla_w_out, xmem_norm, xmem_mem_norm, xmem_wq, xmem_wkv, xmem_q_gain, xmem_k_gain, xmem_wo, ffn_post_norm, ffn_post_w_gu, ffn_post_w_down):
    b, s, d = x.shape
    depth = ffn_pre_norm.shape[0]
    d_ff = ffn_pre_w_down.shape[1]
    t = b * s
    ffn_tm = _tile(t, 512)
    ffn_tf = 256
    row_tm = _tile(s, 1024)
    mla_tq = _tile(s, 512)
    sb_tq = _tile(s, 512)
    sb_kb = _tile(sb_tq, 256)
    bf = lambda a: a.astype(BF16)

    for layer in range(depth):
        x = _ffn(x.reshape(t, d), ffn_pre_norm[layer], ffn_pre_w_gu, ffn_pre_w_down, layer,
                 tm=ffn_tm, tf=ffn_tf).reshape(b, s, d)
        if layer % 2 == 0:
            e = layer // 2
            bias_full = jnp.repeat(sgu_b[e].T, SG_GROUP_DIM, axis=1)
            qt, k, vt, o_sg = _even_prep(
                x, mix_norm[layer], bf(sbg_w_in[e]), sgu_ln_gain[e],
                sgu_ln_bias[e], sgu_w[e], bias_full, tm=row_tm, kb=sb_kb)
            acts = (_sb_attention(qt, k, vt, tq=sb_tq), o_sg)
            w_mix = bf(sbg_w_out[e])
        else:
            o = layer // 2
            mla_consts = _mla_weights(
                mla_w_in[o], mla_w_uq[o], mla_w_ukv[o], mla_q_lora_gain[o],
                mla_kv_lora_gain[o], mla_q_gain[o], mla_k_gain[o])
            qt, k, vt = _mla_prep(x, positions, mix_norm[layer], *mla_consts,
                                  tm=row_tm, kb=mla_tq)
            score_bound = (1.02 * MLA_QK * (MLA_QK ** -0.5 * LOG2E)
                           * jnp.max(jnp.abs(mla_q_gain[o])) * jnp.max(jnp.abs(mla_k_gain[o])))
            acts = _mla_attention(qt, k, vt, score_bound, tq=mla_tq, heads_per_step=4)
            w_mix = bf(mla_w_out[o])
        mem_k, mem_v = _mem_kv(mem, xmem_mem_norm[layer], bf(xmem_wkv[layer]),
                               xmem_k_gain[layer])
        x = _mix_xattn(x, acts, w_mix, xmem_norm[layer], bf(xmem_wq[layer]), xmem_q_gain[layer],
                       mem_k, mem_v, bf(xmem_wo[layer]), tm=_tile(s, 1024),
                       seq_halves=layer % 2 == 1)
        x = _ffn(x.reshape(t, d), ffn_post_norm[layer], ffn_post_w_gu, ffn_post_w_down, layer,
                 tm=ffn_tm, tf=ffn_tf).reshape(b, s, d)
    return x
```

```python
import functools

import jax
import jax.numpy as jnp
from jax import lax
from jax.experimental import pallas as pl
from jax.experimental.pallas import tpu as pltpu

EPS = 1e-6
ROPE_THETA = 10000.0
LANES = 128
SUBLANES = 8
VMEM_LIMIT_BYTES = 56 * 1024 * 1024

SB_HEADS, SB_HEAD_DIM = 8, 64
SB_WIDTH = SB_HEADS * SB_HEAD_DIM
SG_GROUPS, SG_GROUP_DIM, SG_CHUNK = 8, 64, 128
SG_WIDTH = SG_GROUPS * SG_GROUP_DIM
MLA_HEADS, MLA_NOPE, MLA_ROPE, MLA_V = 16, 64, 32, 64
MLA_QK = MLA_NOPE + MLA_ROPE
MLA_Q_LORA, MLA_KV_LORA = 512, 256
MEM_HEADS = 4

BF16 = jnp.bfloat16
F32 = jnp.float32
LOG2E = 1.4426950408889634
MASKED = -1e30
SUM_ROWS = 16
FIXED_SHIFT_MAX = 60.0
UNDERFLOW_LOG2 = 160.0


def _params(*semantics):
    return pltpu.CompilerParams(dimension_semantics=semantics,
                                vmem_limit_bytes=VMEM_LIMIT_BYTES)


def _dot(a, b):
    return jnp.dot(a, b, preferred_element_type=F32)


def _dot_nt(a, b):
    return lax.dot_general(a, b, (((1,), (1,)), ((), ())), preferred_element_type=F32)


def _rms(x, gain):
    ms = jnp.sum(x * x, axis=-1, keepdims=True) * (1.0 / x.shape[-1])
    return x * lax.rsqrt(ms + EPS) * gain


def _ffn_kernel(x_ref, g_ref, wgu_ref, wd_ref, o_ref, *, tf):
    d_ff = wd_ref.shape[0]
    x = x_ref[...]
    h = _rms(x, g_ref[...]).astype(BF16)
    acc = None
    for lo in range(0, d_ff, tf):
        hi = min(lo + tf, d_ff)
        gate = _dot(h, wgu_ref[:, lo:hi].astype(BF16))
        up = _dot(h, wgu_ref[:, d_ff + lo:d_ff + hi].astype(BF16))
        act = (gate * jax.nn.sigmoid(gate) * up).astype(BF16)
        part = _dot(act, wd_ref[lo:hi, :].astype(BF16))
        acc = part if acc is None else acc + part
    o_ref[...] = x + 0.5 * acc


def _ffn(x2, gain, w_gu, w_down, layer, *, tm, tf):
    t, d = x2.shape
    d_ff = w_down.shape[1]
    resident = pl.Buffered(1)
    return pl.pallas_call(
        functools.partial(_ffn_kernel, tf=tf),
        grid=(t // tm,),
        in_specs=[
            pl.BlockSpec((tm, d), lambda i: (i, 0)),
            pl.BlockSpec((1, d), lambda i: (0, 0)),
            pl.BlockSpec((None, d, 2 * d_ff), lambda i: (layer, 0, 0), pipeline_mode=resident),
            pl.BlockSpec((None, d_ff, d), lambda i: (layer, 0, 0), pipeline_mode=resident),
        ],
        out_specs=pl.BlockSpec((tm, d), lambda i: (i, 0)),
        out_shape=jax.ShapeDtypeStruct((t, d), F32),
        compiler_params=_params("parallel"),
        name="ffn",
    )(x2, gain.reshape(1, d), w_gu, w_down)


def _gelu_tanh(x):
    c = 0.7978845608028654
    return 0.5 * x * (1.0 + jnp.tanh(c * (x + 0.044715 * (x * x * x))))


def _even_prep_kernel(x_ref, g_ref, win_ref, lng_ref, lnb_ref, sw_ref, sb_ref,
                      qt_ref, k_ref, vt_ref, osg_ref):
    tm = x_ref.shape[1]
    kb = vt_ref.shape[3]
    w = SB_WIDTH
    h = _rms(x_ref[0], g_ref[...]).astype(BF16)
    osg_ref = osg_ref.at[0]
    u_raw = _dot(h, win_ref[:, 3 * w:3 * w + SG_WIDTH])
    g_raw = _dot(h, win_ref[:, 3 * w + SG_WIDTH:3 * w + 2 * SG_WIDTH])
    qt_ref[0] = (_dot(h, win_ref[:, 0:w]) * (SB_HEAD_DIM ** -0.5 * LOG2E)).T.astype(BF16)
    u = _gelu_tanh(u_raw)
    k_ref[0] = _dot(h, win_ref[:, w:2 * w]).astype(BF16)
    g = _gelu_tanh(g_raw)
    mu = jnp.mean(g, axis=-1, keepdims=True)
    gc = g - mu
    var = jnp.mean(gc * gc, axis=-1, keepdims=True)
    gn = (gc * lax.rsqrt(var + EPS) * lng_ref[...] + lnb_ref[...]).astype(BF16)
    v = _dot(h, win_ref[:, 2 * w:3 * w])
    for c in range(tm // kb):
        vt_ref[0, c] = v[c * kb:(c + 1) * kb, :].T.astype(BF16)

    row = lax.broadcasted_iota(jnp.int32, (SG_CHUNK, SG_CHUNK), 0)
    col = lax.broadcasted_iota(jnp.int32, (SG_CHUNK, SG_CHUNK), 1)
    tri = col <= row
    first_group = lax.broadcasted_iota(jnp.int32, (SG_CHUNK, LANES), 1) < SG_GROUP_DIM
    for p in range(SG_GROUPS // 2):
        lanes = slice(p * LANES, (p + 1) * LANES)
        w0 = jnp.where(tri, sw_ref[2 * p], 0.0).astype(BF16)
        w1 = jnp.where(tri, sw_ref[2 * p + 1], 0.0).astype(BF16)
        bias = sb_ref[:, lanes]
        for c in range(tm // SG_CHUNK):
            rows = slice(c * SG_CHUNK, (c + 1) * SG_CHUNK)
            gp = gn[rows, lanes]
            mixed = jnp.where(first_group, _dot(w0, gp), _dot(w1, gp)) + bias
            osg_ref[rows, lanes] = (u[rows, lanes] * mixed).astype(BF16)


def _even_prep(x, gain, w_in, ln_g, ln_b, sgu_w, sgu_bias_full, *, tm, kb):
    b, s, d = x.shape
    n_in = w_in.shape[1]
    w = SB_WIDTH
    const2 = lambda bi, i: (0, 0)
    row_out = jax.ShapeDtypeStruct((b, s, w), BF16)
    row_spec = pl.BlockSpec((1, tm, w), lambda bi, i: (bi, i, 0))
    return pl.pallas_call(
        _even_prep_kernel,
        grid=(b, s // tm),
        in_specs=[
            pl.BlockSpec((1, tm, d), lambda bi, i: (bi, i, 0)),
            pl.BlockSpec((1, d), const2),
            pl.BlockSpec((d, n_in), const2),
            pl.BlockSpec((1, SG_WIDTH), const2),
            pl.BlockSpec((1, SG_WIDTH), const2),
            pl.BlockSpec((SG_GROUPS, SG_CHUNK, SG_CHUNK), lambda bi, i: (0, 0, 0)),
            pl.BlockSpec((SG_CHUNK, SG_WIDTH), const2),
        ],
        out_specs=[pl.BlockSpec((1, w, tm), lambda bi, i: (bi, 0, i)),
                   row_spec,
                   pl.BlockSpec((1, tm // kb, w, kb), lambda bi, i: (bi, i, 0, 0)),
                   row_spec],
        out_shape=[jax.ShapeDtypeStruct((b, w, s), BF16), row_out,
                   jax.ShapeDtypeStruct((b, s // kb, w, kb), BF16), row_out],
        compiler_params=_params("parallel", "parallel"),
        name="even_prep",
    )(x, gain.reshape(1, d), w_in, ln_g.reshape(1, -1), ln_b.reshape(1, -1),
      sgu_w, sgu_bias_full)


def _sb_attn_kernel(qt_ref, k_ref, vt_ref, o_ref, acc_ref, r_ref, z0_ref, z1_ref, zc0_ref,
                    zc1_ref, t0_ref, t1_ref, bs_ref, kn_ref, *, tq, kb):
    i = pl.program_id(2)
    n_sub = tq // kb
    hd_dim = SB_HEAD_DIM
    z_refs, zc_refs, t_refs = (z0_ref, z1_ref), (zc0_ref, zc1_ref), (t0_ref, t1_ref)
    acc_refs = (acc_ref.at[0], acc_ref.at[1])
    first_head = lax.broadcasted_iota(jnp.int32, (LANES, 1), 0) < hd_dim
    qt = qt_ref[0]
    zero = jnp.zeros_like(qt)
    qt_heads = (jnp.where(first_head, qt, zero), jnp.where(first_head, zero, qt))
    visible_from = {
        lo: (lax.broadcasted_iota(jnp.int32, (kb, tq - lo), 0)
             < lax.broadcasted_iota(jnp.int32, (kb, tq - lo), 1))
        for lo in range(0, tq, kb)}
    srow = lax.broadcasted_iota(jnp.int32, (kb, kb), 0)
    scol = lax.broadcasted_iota(jnp.int32, (kb, kb), 1)
    suffix = (scol >= srow).astype(BF16)

    not_first_window = lax.broadcasted_iota(jnp.int32, (kb, tq), 1) >= kb

    def score(item):
        g, hd, slot, lo, hi = item
        start = pl.multiple_of(g * kb, kb)
        z_refs[slot][:, lo:hi] = _dot(k_ref[0, pl.ds(start, kb), :], qt_heads[hd][:, lo:hi])

    def stay(item, mask):
        g, hd, slot, lo, hi = item
        z = z_refs[slot][:, lo:hi]
        sp = jnp.maximum(z, 0.0) + jnp.log2(1.0 + jnp.exp2(-jnp.abs(z)))
        if mask is not None:
            if isinstance(mask, str):
                visible = visible_from[lo] if mask == "diagonal" else not_first_window
            else:
                visible = mask
            sp = jnp.where(visible, sp, 0.0)
            z = jnp.where(visible, z, MASKED)
        zc_refs[slot][:, lo:hi] = z
        tail = _dot(suffix, sp.astype(BF16))
        t_refs[slot][:, lo:hi] = tail
        parts = [jnp.broadcast_to(tail[0:1, :], (SUBLANES, hi - lo))]
        if lo:
            parts.insert(0, jnp.zeros((SUBLANES, lo), F32))
        if hi < tq:
            parts.append(jnp.zeros((SUBLANES, tq - hi), F32))
        bs_ref[slot] = jnp.concatenate(parts, axis=1) if len(parts) > 1 else parts[0]

    def weigh(item):
        g, hd, slot, lo, hi = item
        r = r_ref[hd]
        wgt = jnp.exp2(zc_refs[slot][:, lo:hi] - t_refs[slot][:, lo:hi] - r[0:1, lo:hi])
        vt = vt_ref[0, g, hd * hd_dim:(hd + 1) * hd_dim, :]
        acc_refs[hd][:, lo:hi] += _dot(vt, wgt.astype(BF16))
        r_ref[hd] = r + bs_ref[slot]

    def block(g, window, prev, nxt, mask):
        score((g, 1, 1, *window))
        if prev is not None:
            weigh((prev[0], 1, 1, *prev[1]))
        stay((g, 0, 0, *window), mask)
        if nxt is not None:
            score((nxt[0], 0, 0, *nxt[1]))
        weigh((g, 0, 0, *window))
        stay((g, 1, 1, *window), mask)

    @pl.when(i == 0)
    def _():
        kf = k_ref[0].astype(F32)
        dim = lax.broadcasted_iota(jnp.int32, (LANES, LANES), 0)
        head = lax.broadcasted_iota(jnp.int32, (LANES, LANES), 1)
        select = ((dim < hd_dim) == (head == 0)) & (head < 2)
        kn_ref[...] = jnp.max(_dot((kf * kf).astype(BF16), select.astype(BF16)),
                              axis=0, keepdims=True)

    lane = lax.broadcasted_iota(jnp.int32, (1, LANES), 1)
    qf = qt.astype(F32)
    exit_level = []
    for hd in range(2):
        q_sq = jnp.sum(jnp.square(qf[hd * hd_dim:(hd + 1) * hd_dim, :]), axis=0, keepdims=True)
        k_sq = jnp.max(jnp.where(lane == hd, kn_ref[...], 0.0), axis=1, keepdims=True)
        exit_level.append(1.02 * jnp.sqrt(q_sq * k_sq) + UNDERFLOW_LOG2)

    acc_ref[...] = jnp.zeros_like(acc_ref)
    r_ref[...] = jnp.zeros_like(r_ref)
    first = i * n_sub
    full = (0, tq)
    below = jnp.maximum(first - 1, 0)
    score((first + n_sub - 1, 0, 0, (n_sub - 1) * kb, tq))
    for d in reversed(range(n_sub)):
        prev = (first + d + 1, ((d + 1) * kb, tq)) if d < n_sub - 1 else None
        nxt = (first + d - 1, ((d - 1) * kb, tq)) if d > 0 else (below, (0, kb))
        block(first + d, (d * kb, tq), prev, nxt, "diagonal")

    def still_live():
        dead = jnp.min(jnp.minimum(r_ref[0] - exit_level[0], r_ref[1] - exit_level[1])) > 0.0
        return jnp.where(dead, 0, 1).astype(jnp.int32)

    block(below, (0, kb), (first, full), None, i > 0)
    weigh((below, 1, 1, 0, kb))

    def write_output():
        out_t = jnp.concatenate([acc_ref[0], acc_ref[1]], axis=0)
        o_ref[0] = out_t.T.astype(BF16)

    write_output()

    @pl.when((still_live() > 0) & (i > 0))
    def _():
        score((below, 0, 0, *full))
        block(below, full, None, (jnp.maximum(below - 1, 0), full), "later windows")

        def more(carry):
            g, live = carry
            return (g >= 0) & (live > 0)

        def body(carry):
            g, _ = carry
            block(g, full, (g + 1, full), (jnp.maximum(g - 1, 0), full), None)
            return g - 1, still_live()

        g_end, _ = lax.while_loop(more, body, (below - 1, still_live()))
        weigh((g_end + 1, 1, 1, *full))
        write_output()


def _sb_attention(qt, k, vt, *, tq):
    b, w, s = qt.shape
    kb = vt.shape[3]
    return pl.pallas_call(
        functools.partial(_sb_attn_kernel, tq=tq, kb=kb),
        grid=(b, w // LANES, s // tq),
        in_specs=[
            pl.BlockSpec((1, LANES, tq), lambda bi, p, i: (bi, p, i)),
            pl.BlockSpec((1, s, LANES), lambda bi, p, i: (bi, 0, p)),
            pl.BlockSpec((1, s // kb, LANES, kb), lambda bi, p, i: (bi, 0, p, 0)),
        ],
        out_specs=pl.BlockSpec((1, tq, LANES), lambda bi, p, i: (bi, i, p)),
        out_shape=jax.ShapeDtypeStruct((b, s, w), BF16),
        scratch_shapes=[pltpu.VMEM((2, SB_HEAD_DIM, tq), F32),
                        pltpu.VMEM((2, SUBLANES, tq), F32),
                        *[pltpu.VMEM((kb, tq), F32) for _ in range(6)],
                        pltpu.VMEM((2, SUBLANES, tq), F32), pltpu.VMEM((1, LANES), F32)],
        compiler_params=_params("parallel", "parallel", "arbitrary"),
        name="sb_attn",
    )(qt, k, vt)


def _lane_tile(t, width):
    return jnp.concatenate([t] * (width // t.shape[1]), axis=1)


def _rms_rows(xt, gain):
    ms = jnp.sum(xt * xt, axis=0, keepdims=True) * (1.0 / xt.shape[0])
    return xt * lax.rsqrt(ms + EPS) * _lane_tile(gain, xt.shape[1])


def _mla_prep_kernel(x_ref, pos_ref, g_ref, win_ref, qlg_ref, kvlg_ref, wuqt_ref, wukt_ref,
                     wuvt_ref, qg_ref, kg_ref, freq_ref, qt_ref, k_ref, vt_ref):
    tm = x_ref.shape[1]
    kb = vt_ref.shape[3]
    lat = MLA_Q_LORA + MLA_KV_LORA
    half = MLA_ROPE // 2
    h = _rms(x_ref[0], g_ref[...]).astype(BF16)
    ct = _dot(h, win_ref[...]).T
    cqn = _rms_rows(ct[0:MLA_Q_LORA], qlg_ref[...]).astype(BF16)
    ckvn = _rms_rows(ct[MLA_Q_LORA:lat], kvlg_ref[...]).astype(BF16)
    k_r = ct[lat:lat + MLA_ROPE]

    group = 4

    def up_project(gi):
        rows = slice(gi * group * MLA_QK, (gi + 1) * group * MLA_QK)
        krows = slice(gi * group * MLA_NOPE, (gi + 1) * group * MLA_NOPE)
        return _dot(wuqt_ref[rows, :], cqn), _dot(wukt_ref[krows, :], ckvn)

    angle = _lane_tile(freq_ref[...], tm) * pos_ref[0].astype(F32)
    cos = jnp.cos(angle)
    sin = jnp.sin(angle)

    def rope(t):
        t1, t2 = t[0:half], t[half:]
        return jnp.concatenate([t1 * cos - t2 * sin, t1 * sin + t2 * cos], axis=0)

    qg = _lane_tile(qg_ref[...], tm)
    kg = _lane_tile(kg_ref[...], tm)
    zeros = jnp.zeros((LANES - MLA_QK, tm), F32)
    inv_n = 1.0 / MLA_QK
    kr_sq = jnp.sum(k_r * k_r, axis=0, keepdims=True)
    kr_roped = rope(k_r * kg[MLA_NOPE:])
    n_groups = MLA_HEADS // group
    pending = up_project(0)
    for gi in range(n_groups):
        q_grp, kn_grp = pending
        if gi + 1 < n_groups:
            pending = up_project(gi + 1)
        else:
            vt = _dot(wuvt_ref[...], ckvn)
            for c in range(tm // kb):
                vt_ref[0, c] = vt[:, c * kb:(c + 1) * kb].astype(BF16)
        for sub in range(group):
            hd = gi * group + sub
            qh = q_grp[sub * MLA_QK:(sub + 1) * MLA_QK]
            r = lax.rsqrt(jnp.sum(qh * qh, axis=0, keepdims=True) * inv_n + EPS)
            qn = qh * r * qg
            qt_ref[0, hd] = jnp.concatenate([qn[0:MLA_NOPE], rope(qn[MLA_NOPE:]), zeros],
                                            axis=0).astype(BF16)
            kn = kn_grp[sub * MLA_NOPE:(sub + 1) * MLA_NOPE]
            r = lax.rsqrt((jnp.sum(kn * kn, axis=0, keepdims=True) + kr_sq) * inv_n + EPS)
            kt = jnp.concatenate([kn * r * kg[0:MLA_NOPE], kr_roped * r, zeros], axis=0)
            k_ref[0, hd] = kt.T.astype(BF16)


def _mla_prep(x, positions, gain, w_in, qlg, kvlg, wuqt, wukt, wuvt, qg, kg, freq, *, tm, kb):
    b, s, d = x.shape
    vw = MLA_HEADS * MLA_V
    const2 = lambda bi, i: (0, 0)
    full = lambda a: pl.BlockSpec(a.shape, const2)
    small = [gain.reshape(1, -1), w_in, qlg, kvlg, wuqt, wukt, wuvt, qg, kg, freq]
    return pl.pallas_call(
        _mla_prep_kernel,
        grid=(b, s // tm),
        in_specs=[
            pl.BlockSpec((1, tm, d), lambda bi, i: (bi, i, 0)),
            pl.BlockSpec((1, 1, tm), lambda bi, i: (bi, 0, i)),
            *[full(a) for a in small],
        ],
        out_specs=[pl.BlockSpec((1, MLA_HEADS, LANES, tm), lambda bi, i: (bi, 0, 0, i)),
                   pl.BlockSpec((1, MLA_HEADS, tm, LANES), lambda bi, i: (bi, 0, i, 0)),
                   pl.BlockSpec((1, tm // kb, vw, kb), lambda bi, i: (bi, i, 0, 0))],
        out_shape=[jax.ShapeDtypeStruct((b, MLA_HEADS, LANES, s), BF16),
                   jax.ShapeDtypeStruct((b, MLA_HEADS, s, LANES), BF16),
                   jax.ShapeDtypeStruct((b, s // kb, vw, kb), BF16)],
        compiler_params=_params("parallel", "parallel"),
        name="mla_prep",
    )(x, positions.reshape(b, 1, s), *small)


def _mla_attn_kernel(qa_ref, qb_ref, k_ref, vt_ref, bound_ref, oa_ref, ob_ref, qt_ref, acc_ref,
                     m_ref, s0_ref, s1_ref, bm_ref, p0_ref, p1_ref, *, tq, n_tiles, fixed_shift):
    i = pl.program_id(2)
    n_heads = qa_ref.shape[1]
    s_refs, p_refs = (s0_ref, s1_ref), (p0_ref, p1_ref)
    key = lax.broadcasted_iota(jnp.int32, (tq, tq), 0)
    query = lax.broadcasted_iota(jnp.int32, (tq, tq), 1)
    causal = key <= query
    ones_rows = jnp.ones((SUM_ROWS, tq), BF16)
    qt_ref[0] = qa_ref[0]
    qt_ref[1] = qb_ref[0]

    def scores(item):
        g, hd, masked, which = item
        start = pl.multiple_of(g * tq, tq)
        sc = _dot(k_ref[0, hd, pl.ds(start, tq), :], qt_ref[which, hd])
        return jnp.where(causal, sc, MASKED) if masked else sc

    def values(item):
        g, hd = item[0], item[1]
        return jnp.concatenate([vt_ref[0, g, hd * MLA_V:(hd + 1) * MLA_V, :], ones_rows], axis=0)

    acc_ref[...] = jnp.zeros_like(acc_ref)

    if fixed_shift:
        bound = bound_ref[:, 0:1]

        def produce(item):
            p_refs[item[1] % 2][...] = jnp.exp2(scores(item) - bound).astype(BF16)

        def consume(item):
            hd, which = item[1], item[3]
            acc_ref[which, hd] += _dot(values(item), p_refs[hd % 2][...])
    else:
        m_ref[...] = jnp.full_like(m_ref, MASKED)

        def produce(item):
            sc = scores(item)
            s_refs[item[1] % 2][...] = sc
            bm_ref[item[1] % 2] = jnp.max(sc, axis=0, keepdims=True)

        def consume(item):
            hd, which = item[1], item[3]
            m_old = m_ref[which, hd]
            m_new = jnp.maximum(m_old, bm_ref[hd % 2])
            alpha = jnp.exp2(m_old - m_new)
            p = jnp.exp2(s_refs[hd % 2][...] - m_new)
            acc_ref[which, hd] = alpha * acc_ref[which, hd] + _dot(values(item), p.astype(BF16))
            m_ref[which, hd] = m_new

    visits = [(i, True, 0), (n_tiles - 1 - i, True, 1)]
    for j in range(n_tiles - 1):
        which = (j >= i).astype(jnp.int32)
        visits.append((j - i * which, False, which))
    items = [(g, hd, masked, which) for g, masked, which in visits for hd in range(n_heads)]
    produce(items[0])
    for idx, item in enumerate(items):
        if idx + 1 < len(items):
            produce(items[idx + 1])
        consume(item)

    for which, o_ref in enumerate((oa_ref, ob_ref)):
        out_t = jnp.concatenate(
            [acc_ref[which, hd, 0:MLA_V, :] / acc_ref[which, hd, MLA_V:MLA_V + 1, :]
             for hd in range(n_heads)], axis=0)
        o_ref[0] = out_t.T.astype(BF16)


def _mla_attention(qt, k, vt, score_bound, *, tq, heads_per_step):
    b, heads, _, s = qt.shape
    kb = vt.shape[3]
    nh = heads_per_step
    n_tiles = s // tq
    assert kb == tq and nh % 2 == 0 and n_tiles % 2 == 0
    half = n_tiles // 2
    vw = nh * MLA_V
    bound_lanes = jnp.full((1, LANES), score_bound, F32)
    half_out = jax.ShapeDtypeStruct((b, s // 2, heads * MLA_V), BF16)

    def call(fixed_shift):
        return pl.pallas_call(
            functools.partial(_mla_attn_kernel, tq=tq, n_tiles=n_tiles, fixed_shift=fixed_shift),
            grid=(b, heads // nh, half),
            in_specs=[
                pl.BlockSpec((1, nh, LANES, tq), lambda bi, p, i: (bi, p, 0, i)),
                pl.BlockSpec((1, nh, LANES, tq), lambda bi, p, i: (bi, p, 0, n_tiles - 1 - i)),
                pl.BlockSpec((1, nh, s, LANES), lambda bi, p, i: (bi, p, 0, 0)),
                pl.BlockSpec((1, s // kb, vw, kb), lambda bi, p, i: (bi, 0, p, 0)),
                pl.BlockSpec((1, LANES), lambda bi, p, i: (0, 0)),
            ],
            out_specs=[pl.BlockSpec((1, tq, vw), lambda bi, p, i: (bi, i, p)),
                       pl.BlockSpec((1, tq, vw), lambda bi, p, i: (bi, half - 1 - i, p))],
            out_shape=[half_out, half_out],
            scratch_shapes=[pltpu.VMEM((2, nh, LANES, tq), BF16),
                            pltpu.VMEM((2, nh, MLA_V + SUM_ROWS, tq), F32),
                            pltpu.VMEM((2, nh, 1, tq), F32), pltpu.VMEM((kb, tq), F32),
                            pltpu.VMEM((kb, tq), F32), pltpu.VMEM((2, 1, tq), F32),
                            pltpu.VMEM((kb, tq), BF16), pltpu.VMEM((kb, tq), BF16)],
            compiler_params=_params("parallel", "parallel", "arbitrary"),
            name="mla_attn_fixed_shift" if fixed_shift else "mla_attn_online_max",
        )(qt, qt, k, vt, bound_lanes)

    return lax.cond(score_bound <= FIXED_SHIFT_MAX, lambda: call(True), lambda: call(False))


def _mem_kv_kernel(mem_ref, g_ref, wkv_ref, kg_ref, k_ref, v_ref):
    hm = _rms(mem_ref[0], g_ref[...]).astype(BF16)
    hd_dim = kg_ref.shape[1]
    for hd in range(MEM_HEADS):
        kcols = slice(2 * hd * hd_dim, (2 * hd + 1) * hd_dim)
        vcols = slice((2 * hd + 1) * hd_dim, (2 * hd + 2) * hd_dim)
        out = slice(hd * hd_dim, (hd + 1) * hd_dim)
        k_ref[0, :, out] = _rms(_dot(hm, wkv_ref[:, kcols]), kg_ref[...]).astype(BF16)
        v_ref[0, :, out] = _dot(hm, wkv_ref[:, vcols]).astype(BF16)


def _mem_kv(mem, gain, wkv, k_gain):
    b, m, d = mem.shape
    out = jax.ShapeDtypeStruct((b, m, d), BF16)
    spec = pl.BlockSpec((1, m, d), lambda bi: (bi, 0, 0))
    return pl.pallas_call(
        _mem_kv_kernel,
        grid=(b,),
        in_specs=[spec,
                  pl.BlockSpec((1, d), lambda bi: (0, 0)),
                  pl.BlockSpec(wkv.shape, lambda bi: (0, 0)),
                  pl.BlockSpec((1, k_gain.shape[0]), lambda bi: (0, 0))],
        out_specs=[spec, spec],
        out_shape=[out, out],
        compiler_params=_params("parallel"),
        name="mem_kv",
    )(mem, gain.reshape(1, d), wkv, k_gain.reshape(1, -1))


def _mix_xattn_kernel(*refs, n_act, seq_halves):
    x_ref = refs[0]
    act_refs = refs[1:1 + n_act]
    w_ref, g_ref, wq_ref, qg_ref, k_ref, v_ref, wo_ref, o_ref = refs[1 + n_act:]
    if seq_halves:
        first_half = pl.program_id(1) < pl.num_programs(1) // 2
        act = jnp.where(first_half, act_refs[0][0], act_refs[1][0])
    else:
        act = jnp.concatenate([a_ref[0] for a_ref in act_refs], axis=-1)
    x1 = x_ref[0] + _dot(act, w_ref[...])
    h = _rms(x1, g_ref[...]).astype(BF16)
    hd_dim = qg_ref.shape[1]
    cols = [slice(hd * hd_dim, (hd + 1) * hd_dim) for hd in range(MEM_HEADS)]
    q = [_dot(h, wq_ref[:, c]) for c in cols]
    qn = [(_rms(q_h, qg_ref[...]) * (hd_dim ** -0.5)).astype(BF16) for q_h in q]
    sc = [_dot_nt(qn_h, k_ref[0, :, c]) for qn_h, c in zip(qn, cols)]
    p = [jnp.exp(s_h - jnp.max(s_h, axis=-1, keepdims=True)) for s_h in sc]
    o = [(_dot(p_h.astype(BF16), v_ref[0, :, c]) / jnp.sum(p_h, axis=-1, keepdims=True)
          ).astype(BF16) for p_h, c in zip(p, cols)]
    o_ref[0] = x1 + _dot(jnp.concatenate(o, axis=-1), wo_ref[...])


def _mix_xattn(x, acts, w_mix, gain, wq, q_gain, mem_k, mem_v, wo, *, tm, seq_halves=False):
    b, s, d = x.shape
    m = mem_k.shape[1]
    const2 = lambda bi, i: (0, 0)
    row = lambda width: pl.BlockSpec((1, tm, width), lambda bi, i: (bi, i, 0))
    mem_spec = pl.BlockSpec((1, m, d), lambda bi, i: (bi, 0, 0))
    if seq_halves:
        assert (s // tm) % 2 == 0, "a tile must not straddle the two sequence halves"
        half = s // tm // 2
        width = acts[0].shape[-1]
        act_specs = [
            pl.BlockSpec((1, tm, width), lambda bi, i: (bi, jnp.minimum(i, half - 1), 0)),
            pl.BlockSpec((1, tm, width), lambda bi, i: (bi, jnp.maximum(i - half, 0), 0))]
    else:
        act_specs = [row(a.shape[-1]) for a in acts]
    return pl.pallas_call(
        functools.partial(_mix_xattn_kernel, n_act=len(acts), seq_halves=seq_halves),
        grid=(b, s // tm),
        in_specs=[
            row(d),
            *act_specs,
            pl.BlockSpec(w_mix.shape, const2),
            pl.BlockSpec((1, d), const2),
            pl.BlockSpec(wq.shape, const2),
            pl.BlockSpec((1, q_gain.shape[0]), const2),
            mem_spec, mem_spec,
            pl.BlockSpec(wo.shape, const2),
        ],
        out_specs=row(d),
        out_shape=jax.ShapeDtypeStruct((b, s, d), F32),
        compiler_params=_params("parallel", "parallel"),
        name="mix_xattn",
    )(x, *acts, w_mix, gain.reshape(1, d), wq, q_gain.reshape(1, -1), mem_k, mem_v, wo)


def _lane_bcast(vec):
    return jnp.broadcast_to(vec[:, None], (vec.shape[0], LANES))


def _mla_weights(w_in, w_uq, w_ukv, q_lora_gain, kv_lora_gain, q_gain, k_gain):
    lat = MLA_Q_LORA + MLA_KV_LORA
    w_in_ext = jnp.pad(w_in, ((0, 0), (0, lat + LANES - w_in.shape[1])))
    wukv = w_ukv.reshape(MLA_KV_LORA, MLA_HEADS, MLA_NOPE + MLA_V)
    wukt = wukv[..., :MLA_NOPE].reshape(MLA_KV_LORA, -1).T
    wuvt = wukv[..., MLA_NOPE:].reshape(MLA_KV_LORA, -1).T
    half = MLA_ROPE // 2
    inv_freq = ROPE_THETA ** (-jnp.arange(half, dtype=F32) / half)
    bf = lambda a: a.astype(BF16)
    return (bf(w_in_ext), _lane_bcast(q_lora_gain), _lane_bcast(kv_lora_gain),
            bf(w_uq.T), bf(wukt), bf(wuvt),
            _lane_bcast(q_gain * (MLA_QK ** -0.5 * LOG2E)), _lane_bcast(k_gain),
            _lane_bcast(inv_freq))


def _tile(n, pref):
    return pref if n % pref == 0 else n


def kernel(x, mem, positions, ffn_pre_norm, ffn_pre_w_gu, ffn_pre_w_down, mix_norm, sbg_w_in, sgu_ln_gain, sgu_ln_bias, sgu_w, sgu_b, sbg_w_out, mla_w_in, mla_q_lora_gain, mla_kv_lora_gain, mla_w_uq, mla_w_ukv, mla_q_gain, mla_k_gain, mla_w_out, xmem_norm, xmem_mem_norm, xmem_wq, xmem_wkv, xmem_q_gain, xmem_k_gain, xmem_wo, ffn_post_norm, ffn_post_w_gu, ffn_post_w_down):
    b, s, d = x.shape
    depth = ffn_pre_norm.shape[0]
    d_ff = ffn_pre_w_down.shape[1]
    t = b * s
    ffn_tm = _tile(t, 512)
    ffn_tf = 256
    row_tm = _tile(s, 1024)
    mla_tq = _tile(s, 512)
    sb_tq = _tile(s, 512)
    sb_kb = _tile(sb_tq, 256)
    bf = lambda a: a.astype(BF16)

    for layer in range(depth):
        x = _ffn(x.reshape(t, d), ffn_pre_norm[layer], ffn_pre_w_gu, ffn_pre_w_down, layer,
                 tm=ffn_tm, tf=ffn_tf).reshape(b, s, d)
        if layer % 2 == 0:
            e = layer // 2
            bias_full = jnp.repeat(sgu_b[e].T, SG_GROUP_DIM, axis=1)
            qt, k, vt, o_sg = _even_prep(
                x, mix_norm[layer], bf(sbg_w_in[e]), sgu_ln_gain[e],
                sgu_ln_bias[e], sgu_w[e], bias_full, tm=row_tm, kb=sb_kb)
            acts = (_sb_attention(qt, k, vt, tq=sb_tq), o_sg)
            w_mix = bf(sbg_w_out[e])
        else:
            o = layer // 2
            mla_consts = _mla_weights(
                mla_w_in[o], mla_w_uq[o], mla_w_ukv[o], mla_q_lora_gain[o],
                mla_kv_lora_gain[o], mla_q_gain[o], mla_k_gain[o])
            qt, k, vt = _mla_prep(x, positions, mix_norm[layer], *mla_consts,
                                  tm=row_tm, kb=mla_tq)
            score_bound = (1.02 * MLA_QK * (MLA_QK ** -0.5 * LOG2E)
                           * jnp.max(jnp.abs(mla_q_gain[o])) * jnp.max(jnp.abs(mla_k_gain[o])))
            acts = _mla_attention(qt, k, vt, score_bound, tq=mla_tq, heads_per_step=4)
            w_mix = bf(mla_w_out[o])
        mem_k, mem_v = _mem_kv(mem, xmem_mem_norm[layer], bf(xmem_wkv[layer]),
                               xmem_k_gain[layer])
        x = _mix_xattn(x, acts, w_mix, xmem_norm[layer], bf(xmem_wq[layer]), xmem_q_gain[layer],
                       mem_k, mem_v, bf(xmem_wo[layer]), tm=_tile(s, 1024),
                       seq_halves=layer % 2 == 1)
        x = _ffn(x.reshape(t, d), ffn_post_norm[layer], ffn_post_w_gu, ffn_post_w_down, layer,
                 tm=ffn_tm, tf=ffn_tf).reshape(b, s, d)
    return x
```

```python
import functools

import jax
import jax.numpy as jnp
from jax import lax
from jax.experimental import pallas as pl
from jax.experimental.pallas import tpu as pltpu

EPS = 1e-6
ROPE_THETA = 10000.0
LANES = 128
SUBLANES = 8
VMEM_LIMIT_BYTES = 56 * 1024 * 1024

SB_HEADS, SB_HEAD_DIM = 8, 64
SB_WIDTH = SB_HEADS * SB_HEAD_DIM
SG_GROUPS, SG_GROUP_DIM, SG_CHUNK = 8, 64, 128
SG_WIDTH = SG_GROUPS * SG_GROUP_DIM
MLA_HEADS, MLA_NOPE, MLA_ROPE, MLA_V = 16, 64, 32, 64
MLA_QK = MLA_NOPE + MLA_ROPE
MLA_Q_LORA, MLA_KV_LORA = 512, 256
MEM_HEADS = 4

BF16 = jnp.bfloat16
F32 = jnp.float32
LOG2E = 1.4426950408889634
MASKED = -1e30
SUM_ROWS = 16
FIXED_SHIFT_MAX = 60.0
UNDERFLOW_LOG2 = 160.0


def _params(*semantics, fuse_inputs=None):
    return pltpu.CompilerParams(dimension_semantics=semantics,
                                vmem_limit_bytes=VMEM_LIMIT_BYTES,
                                allow_input_fusion=fuse_inputs)


def _dot(a, b):
    return jnp.dot(a, b, preferred_element_type=F32)


def _dot_nt(a, b):
    return lax.dot_general(a, b, (((1,), (1,)), ((), ())), preferred_element_type=F32)


def _rms(x, gain):
    ms = jnp.sum(x * x, axis=-1, keepdims=True) * (1.0 / x.shape[-1])
    return x * lax.rsqrt(ms + EPS) * gain


def _ffn_kernel(x_ref, g_ref, wgu_ref, wd_ref, o_ref, *, tf):
    d_ff = wd_ref.shape[0]
    x = x_ref[...]
    h = _rms(x, g_ref[...]).astype(BF16)
    acc = None
    for lo in range(0, d_ff, tf):
        hi = min(lo + tf, d_ff)
        gate = _dot(h, wgu_ref[:, lo:hi].astype(BF16))
        up = _dot(h, wgu_ref[:, d_ff + lo:d_ff + hi].astype(BF16))
        act = (gate * jax.nn.sigmoid(gate) * up).astype(BF16)
        part = _dot(act, wd_ref[lo:hi, :].astype(BF16))
        acc = part if acc is None else acc + part
    o_ref[...] = x + 0.5 * acc


def _ffn(x2, gain, w_gu, w_down, layer, *, tm, tf):
    t, d = x2.shape
    d_ff = w_down.shape[1]
    resident = pl.Buffered(1)
    return pl.pallas_call(
        functools.partial(_ffn_kernel, tf=tf),
        grid=(t // tm,),
        in_specs=[
            pl.BlockSpec((tm, d), lambda i: (i, 0)),
            pl.BlockSpec((1, d), lambda i: (0, 0)),
            pl.BlockSpec((None, d, 2 * d_ff), lambda i: (layer, 0, 0), pipeline_mode=resident),
            pl.BlockSpec((None, d_ff, d), lambda i: (layer, 0, 0), pipeline_mode=resident),
        ],
        out_specs=pl.BlockSpec((tm, d), lambda i: (i, 0)),
        out_shape=jax.ShapeDtypeStruct((t, d), F32),
        compiler_params=_params("parallel"),
        name="ffn",
    )(x2, gain.reshape(1, d), w_gu, w_down)


def _gelu_tanh(x):
    c = 0.7978845608028654
    return 0.5 * x * (1.0 + jnp.tanh(c * (x + 0.044715 * (x * x * x))))


def _even_prep_kernel(x_ref, g_ref, win_ref, lng_ref, lnb_ref, sw_ref, sb_ref,
                      qt_ref, k_ref, vt_ref, osg_ref):
    tm = x_ref.shape[1]
    kb = vt_ref.shape[3]
    w = SB_WIDTH
    h = _rms(x_ref[0], g_ref[...]).astype(BF16)
    osg_ref = osg_ref.at[0]
    u_raw = _dot(h, win_ref[:, 3 * w:3 * w + SG_WIDTH])
    g_raw = _dot(h, win_ref[:, 3 * w + SG_WIDTH:3 * w + 2 * SG_WIDTH])
    qt_ref[0] = (_dot(h, win_ref[:, 0:w]) * (SB_HEAD_DIM ** -0.5 * LOG2E)).T.astype(BF16)
    u = _gelu_tanh(u_raw)
    k_ref[0] = _dot(h, win_ref[:, w:2 * w]).astype(BF16)
    g = _gelu_tanh(g_raw)
    mu = jnp.mean(g, axis=-1, keepdims=True)
    gc = g - mu
    var = jnp.mean(gc * gc, axis=-1, keepdims=True)
    gn = (gc * lax.rsqrt(var + EPS) * lng_ref[...] + lnb_ref[...]).astype(BF16)
    v = _dot(h, win_ref[:, 2 * w:3 * w])
    for c in range(tm // kb):
        vt_ref[0, c] = v[c * kb:(c + 1) * kb, :].T.astype(BF16)

    row = lax.broadcasted_iota(jnp.int32, (SG_CHUNK, SG_CHUNK), 0)
    col = lax.broadcasted_iota(jnp.int32, (SG_CHUNK, SG_CHUNK), 1)
    tri = col <= row
    first_group = lax.broadcasted_iota(jnp.int32, (SG_CHUNK, LANES), 1) < SG_GROUP_DIM
    for p in range(SG_GROUPS // 2):
        lanes = slice(p * LANES, (p + 1) * LANES)
        w0 = jnp.where(tri, sw_ref[2 * p], 0.0).astype(BF16)
        w1 = jnp.where(tri, sw_ref[2 * p + 1], 0.0).astype(BF16)
        bias = sb_ref[:, lanes]
        for c in range(tm // SG_CHUNK):
            rows = slice(c * SG_CHUNK, (c + 1) * SG_CHUNK)
            gp = gn[rows, lanes]
            mixed = jnp.where(first_group, _dot(w0, gp), _dot(w1, gp)) + bias
            osg_ref[rows, lanes] = (u[rows, lanes] * mixed).astype(BF16)


def _even_prep(x, gain, w_in, ln_g, ln_b, sgu_w, sgu_bias_full, *, tm, kb):
    b, s, d = x.shape
    n_in = w_in.shape[1]
    w = SB_WIDTH
    const2 = lambda bi, i: (0, 0)
    row_out = jax.ShapeDtypeStruct((b, s, w), BF16)
    row_spec = pl.BlockSpec((1, tm, w), lambda bi, i: (bi, i, 0))
    return pl.pallas_call(
        _even_prep_kernel,
        grid=(b, s // tm),
        in_specs=[
            pl.BlockSpec((1, tm, d), lambda bi, i: (bi, i, 0)),
            pl.BlockSpec((1, d), const2),
            pl.BlockSpec((d, n_in), const2),
            pl.BlockSpec((1, SG_WIDTH), const2),
            pl.BlockSpec((1, SG_WIDTH), const2),
            pl.BlockSpec((SG_GROUPS, SG_CHUNK, SG_CHUNK), lambda bi, i: (0, 0, 0)),
            pl.BlockSpec((SG_CHUNK, SG_WIDTH), const2),
        ],
        out_specs=[pl.BlockSpec((1, w, tm), lambda bi, i: (bi, 0, i)),
                   row_spec,
                   pl.BlockSpec((1, tm // kb, w, kb), lambda bi, i: (bi, i, 0, 0)),
                   row_spec],
        out_shape=[jax.ShapeDtypeStruct((b, w, s), BF16), row_out,
                   jax.ShapeDtypeStruct((b, s // kb, w, kb), BF16), row_out],
        compiler_params=_params("parallel", "parallel"),
        name="even_prep",
    )(x, gain.reshape(1, d), w_in, ln_g.reshape(1, -1), ln_b.reshape(1, -1),
      sgu_w, sgu_bias_full)


def _sb_attn_kernel(qt_ref, k_ref, vt_ref, o_ref, acc_ref, r_ref, z0_ref, z1_ref, zc0_ref,
                    zc1_ref, t0_ref, t1_ref, bs_ref, kn_ref, *, tq, kb):
    i = pl.program_id(2)
    n_sub = tq // kb
    hd_dim = SB_HEAD_DIM
    z_refs, zc_refs, t_refs = (z0_ref, z1_ref), (zc0_ref, zc1_ref), (t0_ref, t1_ref)
    acc_refs = (acc_ref.at[0], acc_ref.at[1])
    first_head = lax.broadcasted_iota(jnp.int32, (LANES, 1), 0) < hd_dim
    qt = qt_ref[0]
    zero = jnp.zeros_like(qt)
    qt_heads = (jnp.where(first_head, qt, zero), jnp.where(first_head, zero, qt))
    visible_from = {
        lo: (lax.broadcasted_iota(jnp.int32, (kb, tq - lo), 0)
             < lax.broadcasted_iota(jnp.int32, (kb, tq - lo), 1))
        for lo in range(0, tq, kb)}
    srow = lax.broadcasted_iota(jnp.int32, (kb, kb), 0)
    scol = lax.broadcasted_iota(jnp.int32, (kb, kb), 1)
    suffix = (scol >= srow).astype(BF16)

    not_first_window = lax.broadcasted_iota(jnp.int32, (kb, tq), 1) >= kb

    def score(item):
        g, hd, slot, lo, hi = item
        start = pl.multiple_of(g * kb, kb)
        z_refs[slot][:, lo:hi] = _dot(k_ref[0, pl.ds(start, kb), :], qt_heads[hd][:, lo:hi])

    def stay(item, mask):
        g, hd, slot, lo, hi = item
        z = z_refs[slot][:, lo:hi]
        sp = jnp.maximum(z, 0.0) + jnp.log2(1.0 + jnp.exp2(-jnp.abs(z)))
        if mask is not None:
            if isinstance(mask, str):
                visible = visible_from[lo] if mask == "diagonal" else not_first_window
            else:
                visible = mask
            sp = jnp.where(visible, sp, 0.0)
            z = jnp.where(visible, z, MASKED)
        zc_refs[slot][:, lo:hi] = z
        tail = _dot(suffix, sp.astype(BF16))
        t_refs[slot][:, lo:hi] = tail
        parts = [jnp.broadcast_to(tail[0:1, :], (SUBLANES, hi - lo))]
        if lo:
            parts.insert(0, jnp.zeros((SUBLANES, lo), F32))
        if hi < tq:
            parts.append(jnp.zeros((SUBLANES, tq - hi), F32))
        bs_ref[slot] = jnp.concatenate(parts, axis=1) if len(parts) > 1 else parts[0]

    def weigh(item):
        g, hd, slot, lo, hi = item
        r = r_ref[hd]
        wgt = jnp.exp2(zc_refs[slot][:, lo:hi] - t_refs[slot][:, lo:hi] - r[0:1, lo:hi])
        vt = vt_ref[0, g, hd * hd_dim:(hd + 1) * hd_dim, :]
        acc_refs[hd][:, lo:hi] += _dot(vt, wgt.astype(BF16))
        r_ref[hd] = r + bs_ref[slot]

    def block(g, window, prev, nxt, mask):
        score((g, 1, 1, *window))
        if prev is not None:
            weigh((prev[0], 1, 1, *prev[1]))
        stay((g, 0, 0, *window), mask)
        if nxt is not None:
            score((nxt[0], 0, 0, *nxt[1]))
        weigh((g, 0, 0, *window))
        stay((g, 1, 1, *window), mask)

    @pl.when(i == 0)
    def _():
        kf = k_ref[0].astype(F32)
        dim = lax.broadcasted_iota(jnp.int32, (LANES, LANES), 0)
        head = lax.broadcasted_iota(jnp.int32, (LANES, LANES), 1)
        select = ((dim < hd_dim) == (head == 0)) & (head < 2)
        kn_ref[...] = jnp.max(_dot((kf * kf).astype(BF16), select.astype(BF16)),
                              axis=0, keepdims=True)

    lane = lax.broadcasted_iota(jnp.int32, (1, LANES), 1)
    qf = qt.astype(F32)
    exit_level = []
    for hd in range(2):
        q_sq = jnp.sum(jnp.square(qf[hd * hd_dim:(hd + 1) * hd_dim, :]), axis=0, keepdims=True)
        k_sq = jnp.max(jnp.where(lane == hd, kn_ref[...], 0.0), axis=1, keepdims=True)
        exit_level.append(1.02 * jnp.sqrt(q_sq * k_sq) + UNDERFLOW_LOG2)

    acc_ref[...] = jnp.zeros_like(acc_ref)
    r_ref[...] = jnp.zeros_like(r_ref)
    first = i * n_sub
    full = (0, tq)
    below = jnp.maximum(first - 1, 0)
    score((first + n_sub - 1, 0, 0, (n_sub - 1) * kb, tq))
    for d in reversed(range(n_sub)):
        prev = (first + d + 1, ((d + 1) * kb, tq)) if d < n_sub - 1 else None
        nxt = (first + d - 1, ((d - 1) * kb, tq)) if d > 0 else (below, (0, kb))
        block(first + d, (d * kb, tq), prev, nxt, "diagonal")

    def still_live():
        dead = jnp.min(jnp.minimum(r_ref[0] - exit_level[0], r_ref[1] - exit_level[1])) > 0.0
        return jnp.where(dead, 0, 1).astype(jnp.int32)

    block(below, (0, kb), (first, full), None, i > 0)
    weigh((below, 1, 1, 0, kb))

    def write_output():
        out_t = jnp.concatenate([acc_ref[0], acc_ref[1]], axis=0)
        o_ref[0] = out_t.T.astype(BF16)

    write_output()

    @pl.when((still_live() > 0) & (i > 0))
    def _():
        score((below, 0, 0, *full))
        block(below, full, None, (jnp.maximum(below - 1, 0), full), "later windows")

        def more(carry):
            g, live = carry
            return (g >= 0) & (live > 0)

        def body(carry):
            g, _ = carry
            block(g, full, (g + 1, full), (jnp.maximum(g - 1, 0), full), None)
            return g - 1, still_live()

        g_end, _ = lax.while_loop(more, body, (below - 1, still_live()))
        weigh((g_end + 1, 1, 1, *full))
        write_output()


def _sb_attention(qt, k, vt, *, tq):
    b, w, s = qt.shape
    kb = vt.shape[3]
    return pl.pallas_call(
        functools.partial(_sb_attn_kernel, tq=tq, kb=kb),
        grid=(b, w // LANES, s // tq),
        in_specs=[
            pl.BlockSpec((1, LANES, tq), lambda bi, p, i: (bi, p, i)),
            pl.BlockSpec((1, s, LANES), lambda bi, p, i: (bi, 0, p)),
            pl.BlockSpec((1, s // kb, LANES, kb), lambda bi, p, i: (bi, 0, p, 0)),
        ],
        out_specs=pl.BlockSpec((1, tq, LANES), lambda bi, p, i: (bi, i, p)),
        out_shape=jax.ShapeDtypeStruct((b, s, w), BF16),
        scratch_shapes=[pltpu.VMEM((2, SB_HEAD_DIM, tq), F32),
                        pltpu.VMEM((2, SUBLANES, tq), F32),
                        *[pltpu.VMEM((kb, tq), F32) for _ in range(6)],
                        pltpu.VMEM((2, SUBLANES, tq), F32), pltpu.VMEM((1, LANES), F32)],
        compiler_params=_params("parallel", "parallel", "arbitrary"),
        name="sb_attn",
    )(qt, k, vt)


def _lane_tile(t, width):
    return jnp.concatenate([t] * (width // t.shape[1]), axis=1)


def _rms_rows(xt, gain):
    ms = jnp.sum(xt * xt, axis=0, keepdims=True) * (1.0 / xt.shape[0])
    return xt * lax.rsqrt(ms + EPS) * _lane_tile(gain, xt.shape[1])


def _mla_prep_kernel(x_ref, pos_ref, g_ref, win_ref, qlg_ref, kvlg_ref, wuqt_ref, wukt_ref,
                     wuvt_ref, qg_ref, kg_ref, freq_ref, qt_ref, k_ref, vt_ref):
    tm = x_ref.shape[1]
    kb = vt_ref.shape[3]
    lat = MLA_Q_LORA + MLA_KV_LORA
    half = MLA_ROPE // 2
    h = _rms(x_ref[0], g_ref[...]).astype(BF16)
    ct = _dot(h, win_ref[...]).T
    cqn = _rms_rows(ct[0:MLA_Q_LORA], qlg_ref[...]).astype(BF16)
    ckvn = _rms_rows(ct[MLA_Q_LORA:lat], kvlg_ref[...]).astype(BF16)
    k_r = ct[lat:lat + MLA_ROPE]

    group = 4

    def up_project(gi):
        rows = slice(gi * group * MLA_QK, (gi + 1) * group * MLA_QK)
        krows = slice(gi * group * MLA_NOPE, (gi + 1) * group * MLA_NOPE)
        return _dot(wuqt_ref[rows, :], cqn), _dot(wukt_ref[krows, :], ckvn)

    angle = _lane_tile(freq_ref[...], tm) * pos_ref[0].astype(F32)
    cos = jnp.cos(angle)
    sin = jnp.sin(angle)

    def rope(t):
        t1, t2 = t[0:half], t[half:]
        return jnp.concatenate([t1 * cos - t2 * sin, t1 * sin + t2 * cos], axis=0)

    qg = _lane_tile(qg_ref[...], tm)
    kg = _lane_tile(kg_ref[...], tm)
    zeros = jnp.zeros((LANES - MLA_QK, tm), F32)
    inv_n = 1.0 / MLA_QK
    kr_sq = jnp.sum(k_r * k_r, axis=0, keepdims=True)
    kr_roped = rope(k_r * kg[MLA_NOPE:])
    n_groups = MLA_HEADS // group
    pending = up_project(0)
    for gi in range(n_groups):
        q_grp, kn_grp = pending
        if gi + 1 < n_groups:
            pending = up_project(gi + 1)
        else:
            vt = _dot(wuvt_ref[...], ckvn)
            for c in range(tm // kb):
                vt_ref[0, c] = vt[:, c * kb:(c + 1) * kb].astype(BF16)
        for sub in range(group):
            hd = gi * group + sub
            qh = q_grp[sub * MLA_QK:(sub + 1) * MLA_QK]
            r = lax.rsqrt(jnp.sum(qh * qh, axis=0, keepdims=True) * inv_n + EPS)
            qn = qh * r * qg
            qt_ref[0, hd] = jnp.concatenate([qn[0:MLA_NOPE], rope(qn[MLA_NOPE:]), zeros],
                                            axis=0).astype(BF16)
            kn = kn_grp[sub * MLA_NOPE:(sub + 1) * MLA_NOPE]
            r = lax.rsqrt((jnp.sum(kn * kn, axis=0, keepdims=True) + kr_sq) * inv_n + EPS)
            kt = jnp.concatenate([kn * r * kg[0:MLA_NOPE], kr_roped * r, zeros], axis=0)
            k_ref[0, hd] = kt.T.astype(BF16)


def _mla_prep(x, positions, gain, w_in, qlg, kvlg, wuqt, wukt, wuvt, qg, kg, freq, *, tm, kb):
    b, s, d = x.shape
    vw = MLA_HEADS * MLA_V
    const2 = lambda bi, i: (0, 0)
    full = lambda a: pl.BlockSpec(a.shape, const2)
    small = [gain.reshape(1, -1), w_in, qlg, kvlg, wuqt, wukt, wuvt, qg, kg, freq]
    return pl.pallas_call(
        _mla_prep_kernel,
        grid=(b, s // tm),
        in_specs=[
            pl.BlockSpec((1, tm, d), lambda bi, i: (bi, i, 0)),
            pl.BlockSpec((1, 1, tm), lambda bi, i: (bi, 0, i)),
            *[full(a) for a in small],
        ],
        out_specs=[pl.BlockSpec((1, MLA_HEADS, LANES, tm), lambda bi, i: (bi, 0, 0, i)),
                   pl.BlockSpec((1, MLA_HEADS, tm, LANES), lambda bi, i: (bi, 0, i, 0)),
                   pl.BlockSpec((1, tm // kb, vw, kb), lambda bi, i: (bi, i, 0, 0))],
        out_shape=[jax.ShapeDtypeStruct((b, MLA_HEADS, LANES, s), BF16),
                   jax.ShapeDtypeStruct((b, MLA_HEADS, s, LANES), BF16),
                   jax.ShapeDtypeStruct((b, s // kb, vw, kb), BF16)],
        compiler_params=_params("parallel", "parallel"),
        name="mla_prep",
    )(x, positions.reshape(b, 1, s), *small)


def _mla_attn_kernel(qa_ref, qb_ref, k_ref, vt_ref, bound_ref, oa_ref, ob_ref, qt_ref, acc_ref,
                     m_ref, s0_ref, s1_ref, bm_ref, p0_ref, p1_ref, *, tq, n_tiles, fixed_shift):
    i = pl.program_id(2)
    n_heads = qa_ref.shape[1]
    s_refs, p_refs = (s0_ref, s1_ref), (p0_ref, p1_ref)
    key = lax.broadcasted_iota(jnp.int32, (tq, tq), 0)
    query = lax.broadcasted_iota(jnp.int32, (tq, tq), 1)
    causal = key <= query
    ones_rows = jnp.ones((SUM_ROWS, tq), BF16)
    qt_ref[0] = qa_ref[0]
    qt_ref[1] = qb_ref[0]

    def scores(item):
        g, hd, masked, which = item
        start = pl.multiple_of(g * tq, tq)
        sc = _dot(k_ref[0, hd, pl.ds(start, tq), :], qt_ref[which, hd])
        return jnp.where(causal, sc, MASKED) if masked else sc

    def values(item):
        g, hd = item[0], item[1]
        return jnp.concatenate([vt_ref[0, g, hd * MLA_V:(hd + 1) * MLA_V, :], ones_rows], axis=0)

    acc_ref[...] = jnp.zeros_like(acc_ref)

    if fixed_shift:
        bound = bound_ref[:, 0:1]

        def produce(item):
            p_refs[item[1] % 2][...] = jnp.exp2(scores(item) - bound).astype(BF16)

        def consume(item):
            hd, which = item[1], item[3]
            acc_ref[which, hd] += _dot(values(item), p_refs[hd % 2][...])
    else:
        m_ref[...] = jnp.full_like(m_ref, MASKED)

        def produce(item):
            sc = scores(item)
            s_refs[item[1] % 2][...] = sc
            bm_ref[item[1] % 2] = jnp.max(sc, axis=0, keepdims=True)

        def consume(item):
            hd, which = item[1], item[3]
            m_old = m_ref[which, hd]
            m_new = jnp.maximum(m_old, bm_ref[hd % 2])
            alpha = jnp.exp2(m_old - m_new)
            p = jnp.exp2(s_refs[hd % 2][...] - m_new)
            acc_ref[which, hd] = alpha * acc_ref[which, hd] + _dot(values(item), p.astype(BF16))
            m_ref[which, hd] = m_new

    visits = [(i, True, 0), (n_tiles - 1 - i, True, 1)]
    for j in range(n_tiles - 1):
        which = (j >= i).astype(jnp.int32)
        visits.append((j - i * which, False, which))
    items = [(g, hd, masked, which) for g, masked, which in visits for hd in range(n_heads)]
    produce(items[0])
    for idx, item in enumerate(items):
        if idx + 1 < len(items):
            produce(items[idx + 1])
        consume(item)

    for which, o_ref in enumerate((oa_ref, ob_ref)):
        out_t = jnp.concatenate(
            [acc_ref[which, hd, 0:MLA_V, :] / acc_ref[which, hd, MLA_V:MLA_V + 1, :]
             for hd in range(n_heads)], axis=0)
        o_ref[0] = out_t.T.astype(BF16)


def _mla_attention(qt, k, vt, score_bound, *, tq, heads_per_step):
    b, heads, _, s = qt.shape
    kb = vt.shape[3]
    nh = heads_per_step
    n_tiles = s // tq
    assert kb == tq and nh % 2 == 0 and n_tiles % 2 == 0
    half = n_tiles // 2
    vw = nh * MLA_V
    bound_lanes = jnp.full((1, LANES), score_bound, F32)
    half_out = jax.ShapeDtypeStruct((b, s // 2, heads * MLA_V), BF16)

    def call(fixed_shift):
        return pl.pallas_call(
            functools.partial(_mla_attn_kernel, tq=tq, n_tiles=n_tiles, fixed_shift=fixed_shift),
            grid=(b, heads // nh, half),
            in_specs=[
                pl.BlockSpec((1, nh, LANES, tq), lambda bi, p, i: (bi, p, 0, i)),
                pl.BlockSpec((1, nh, LANES, tq), lambda bi, p, i: (bi, p, 0, n_tiles - 1 - i)),
                pl.BlockSpec((1, nh, s, LANES), lambda bi, p, i: (bi, p, 0, 0)),
                pl.BlockSpec((1, s // kb, vw, kb), lambda bi, p, i: (bi, 0, p, 0)),
                pl.BlockSpec((1, LANES), lambda bi, p, i: (0, 0)),
            ],
            out_specs=[pl.BlockSpec((1, tq, vw), lambda bi, p, i: (bi, i, p)),
                       pl.BlockSpec((1, tq, vw), lambda bi, p, i: (bi, half - 1 - i, p))],
            out_shape=[half_out, half_out],
            scratch_shapes=[pltpu.VMEM((2, nh, LANES, tq), BF16),
                            pltpu.VMEM((2, nh, MLA_V + SUM_ROWS, tq), F32),
                            pltpu.VMEM((2, nh, 1, tq), F32), pltpu.VMEM((kb, tq), F32),
                            pltpu.VMEM((kb, tq), F32), pltpu.VMEM((2, 1, tq), F32),
                            pltpu.VMEM((kb, tq), BF16), pltpu.VMEM((kb, tq), BF16)],
            compiler_params=_params("parallel", "parallel", "arbitrary"),
            name="mla_attn_fixed_shift" if fixed_shift else "mla_attn_online_max",
        )(qt, qt, k, vt, bound_lanes)

    return lax.cond(score_bound <= FIXED_SHIFT_MAX, lambda: call(True), lambda: call(False))


def _mem_kv_kernel(mem_ref, g_ref, wkv_ref, kg_ref, k_ref, v_ref):
    hm = _rms(mem_ref[0], g_ref[...]).astype(BF16)
    hd_dim = kg_ref.shape[1]
    for hd in range(MEM_HEADS):
        kcols = slice(2 * hd * hd_dim, (2 * hd + 1) * hd_dim)
        vcols = slice((2 * hd + 1) * hd_dim, (2 * hd + 2) * hd_dim)
        out = slice(hd * hd_dim, (hd + 1) * hd_dim)
        k_ref[0, :, out] = _rms(_dot(hm, wkv_ref[:, kcols]), kg_ref[...]).astype(BF16)
        v_ref[0, :, out] = _dot(hm, wkv_ref[:, vcols]).astype(BF16)


def _mem_kv(mem, gain, wkv, k_gain):
    b, m, d = mem.shape
    out = jax.ShapeDtypeStruct((b, m, d), BF16)
    spec = pl.BlockSpec((1, m, d), lambda bi: (bi, 0, 0))
    return pl.pallas_call(
        _mem_kv_kernel,
        grid=(b,),
        in_specs=[spec,
                  pl.BlockSpec((1, d), lambda bi: (0, 0)),
                  pl.BlockSpec(wkv.shape, lambda bi: (0, 0)),
                  pl.BlockSpec((1, k_gain.shape[0]), lambda bi: (0, 0))],
        out_specs=[spec, spec],
        out_shape=[out, out],
        compiler_params=_params("parallel"),
        name="mem_kv",
    )(mem, gain.reshape(1, d), wkv, k_gain.reshape(1, -1))


def _mix_xattn_kernel(*refs, n_act, seq_halves):
    x_ref = refs[0]
    act_refs = refs[1:1 + n_act]
    w_ref, g_ref, wq_ref, qg_ref, k_ref, v_ref, wo_ref, o_ref = refs[1 + n_act:]
    if seq_halves:
        first_half = pl.program_id(1) < pl.num_programs(1) // 2
        act = jnp.where(first_half, act_refs[0][0], act_refs[1][0])
    else:
        act = jnp.concatenate([a_ref[0] for a_ref in act_refs], axis=-1)
    x1 = x_ref[0] + _dot(act, w_ref[...])
    h = _rms(x1, g_ref[...]).astype(BF16)
    hd_dim = qg_ref.shape[1]
    cols = [slice(hd * hd_dim, (hd + 1) * hd_dim) for hd in range(MEM_HEADS)]
    q = [_dot(h, wq_ref[:, c]) for c in cols]
    qn = [(_rms(q_h, qg_ref[...]) * (hd_dim ** -0.5)).astype(BF16) for q_h in q]
    sc = [_dot_nt(qn_h, k_ref[0, :, c]) for qn_h, c in zip(qn, cols)]
    p = [jnp.exp(s_h - jnp.max(s_h, axis=-1, keepdims=True)) for s_h in sc]
    o = [(_dot(p_h.astype(BF16), v_ref[0, :, c]) / jnp.sum(p_h, axis=-1, keepdims=True)
          ).astype(BF16) for p_h, c in zip(p, cols)]
    o_ref[0] = x1 + _dot(jnp.concatenate(o, axis=-1), wo_ref[...])


def _mix_xattn(x, acts, w_mix, gain, wq, q_gain, mem_k, mem_v, wo, *, tm, seq_halves=False):
    b, s, d = x.shape
    m = mem_k.shape[1]
    const2 = lambda bi, i: (0, 0)
    row = lambda width: pl.BlockSpec((1, tm, width), lambda bi, i: (bi, i, 0))
    mem_spec = pl.BlockSpec((1, m, d), lambda bi, i: (bi, 0, 0))
    if seq_halves:
        assert (s // tm) % 2 == 0, "a tile must not straddle the two sequence halves"
        half = s // tm // 2
        width = acts[0].shape[-1]
        act_specs = [
            pl.BlockSpec((1, tm, width), lambda bi, i: (bi, jnp.minimum(i, half - 1), 0)),
            pl.BlockSpec((1, tm, width), lambda bi, i: (bi, jnp.maximum(i - half, 0), 0))]
    else:
        act_specs = [row(a.shape[-1]) for a in acts]
    return pl.pallas_call(
        functools.partial(_mix_xattn_kernel, n_act=len(acts), seq_halves=seq_halves),
        grid=(b, s // tm),
        in_specs=[
            row(d),
            *act_specs,
            pl.BlockSpec(w_mix.shape, const2),
            pl.BlockSpec((1, d), const2),
            pl.BlockSpec(wq.shape, const2),
            pl.BlockSpec((1, q_gain.shape[0]), const2),
            mem_spec, mem_spec,
            pl.BlockSpec(wo.shape, const2),
        ],
        out_specs=row(d),
        out_shape=jax.ShapeDtypeStruct((b, s, d), F32),
        compiler_params=_params(
            "parallel", "parallel",
            fuse_inputs=[False] * (1 + len(acts)) + [True, False, True, False, False, False, True]),
        name="mix_xattn",
    )(x, *acts, w_mix, gain.reshape(1, d), wq, q_gain.reshape(1, -1), mem_k, mem_v, wo)


def _lane_bcast(vec):
    return jnp.broadcast_to(vec[:, None], (vec.shape[0], LANES))


def _mla_weights(w_in, w_uq, w_ukv, q_lora_gain, kv_lora_gain, q_gain, k_gain):
    lat = MLA_Q_LORA + MLA_KV_LORA
    w_in_ext = jnp.pad(w_in, ((0, 0), (0, lat + LANES - w_in.shape[1])))
    wukv = w_ukv.reshape(MLA_KV_LORA, MLA_HEADS, MLA_NOPE + MLA_V)
    wukt = wukv[..., :MLA_NOPE].reshape(MLA_KV_LORA, -1).T
    wuvt = wukv[..., MLA_NOPE:].reshape(MLA_KV_LORA, -1).T
    half = MLA_ROPE // 2
    inv_freq = ROPE_THETA ** (-jnp.arange(half, dtype=F32) / half)
    bf = lambda a: a.astype(BF16)
    return (bf(w_in_ext), _lane_bcast(q_lora_gain), _lane_bcast(kv_lora_gain),
            bf(w_uq.T), bf(wukt), bf(wuvt),
            _lane_bcast(q_gain * (MLA_QK ** -0.5 * LOG2E)), _lane_bcast(k_gain),
            _lane_bcast(inv_freq))


def _tile(n, pref):
    return pref if n % pref == 0 else n


def kernel(x, mem, positions, ffn_pre_norm, ffn_pre_w_gu, ffn_pre_w_down, mix_norm, sbg_w_in, sgu_ln_gain, sgu_ln_bias, sgu_w, sgu_b, sbg_w_out, mla_w_in, mla_q_lora_gain, mla_kv_lora_gain, mla_w_uq, mla_w_ukv, mla_q_gain, mla_k_gain, mla_w_out, xmem_norm, xmem_mem_norm, xmem_wq, xmem_wkv, xmem_q_gain, xmem_k_gain, xmem_wo, ffn_post_norm, ffn_post_w_gu, ffn_post_w_down):
    b, s, d = x.shape
    depth = ffn_pre_norm.shape[0]
    d_ff = ffn_pre_w_down.shape[1]
    t = b * s
    ffn_tm = _tile(t, 512)
    ffn_tf = 256
    row_tm = _tile(s, 1024)
    mla_tq = _tile(s, 512)
    sb_tq = _tile(s, 512)
    sb_kb = _tile(sb_tq, 256)
    bf = lambda a: a.astype(BF16)

    for layer in range(depth):
        x = _ffn(x.reshape(t, d), ffn_pre_norm[layer], ffn_pre_w_gu, ffn_pre_w_down, layer,
                 tm=ffn_tm, tf=ffn_tf).reshape(b, s, d)
        if layer % 2 == 0:
            e = layer // 2
            bias_full = jnp.repeat(sgu_b[e].T, SG_GROUP_DIM, axis=1)
            qt, k, vt, o_sg = _even_prep(
                x, mix_norm[layer], bf(sbg_w_in[e]), sgu_ln_gain[e],
                sgu_ln_bias[e], sgu_w[e], bias_full, tm=row_tm, kb=sb_kb)
            acts = (_sb_attention(qt, k, vt, tq=sb_tq), o_sg)
            w_mix = bf(sbg_w_out[e])
        else:
            o = layer // 2
            mla_consts = _mla_weights(
                mla_w_in[o], mla_w_uq[o], mla_w_ukv[o], mla_q_lora_gain[o],
                mla_kv_lora_gain[o], mla_q_gain[o], mla_k_gain[o])
            qt, k, vt = _mla_prep(x, positions, mix_norm[layer], *mla_consts,
                                  tm=row_tm, kb=mla_tq)
            score_bound = (1.02 * MLA_QK * (MLA_QK ** -0.5 * LOG2E)
                           * jnp.max(jnp.abs(mla_q_gain[o])) * jnp.max(jnp.abs(mla_k_gain[o])))
            acts = _mla_attention(qt, k, vt, score_bound, tq=mla_tq, heads_per_step=4)
            w_mix = bf(mla_w_out[o])
        mem_k, mem_v = _mem_kv(mem, xmem_mem_norm[layer], bf(xmem_wkv[layer]),
                               xmem_k_gain[layer])
        x = _mix_xattn(x, acts, w_mix, xmem_norm[layer], bf(xmem_wq[layer]), xmem_q_gain[layer],
                       mem_k, mem_v, bf(xmem_wo[layer]), tm=_tile(s, 1024),
                       seq_halves=layer % 2 == 1)
        x = _ffn(x.reshape(t, d), ffn_post_norm[layer], ffn_post_w_gu, ffn_post_w_down, layer,
                 tm=ffn_tm, tf=ffn_tf).reshape(b, s, d)
    return x
```
